```python
import jax, jax.numpy as jnp
from jax import lax
import numpy as np

D_MODEL = 2048
BATCH = 1
SEQ = 16384
DEPTH = 1
DEC_BATCH = 8
DEC_SEQ = 64
PAST_LEN = 1024

CHUNK = 64
Q_BLOCK = 128
MIX_WIDTH = D_MODEL
N_ATTN_HEADS = 8
ATTN_HEAD_DIM = MIX_WIDTH // 2 // N_ATTN_HEADS
ROPE_DIM = ATTN_HEAD_DIM // 4
ROPE_THETA = 500000.0
N_IDX_HEADS = 16
IDX_DIM = 64
IDX_ROPE_DIM = IDX_DIM // 4
TOPK_MAX = 256
GLA_HEADS = 4
GLA_DV = MIX_WIDTH // 2 // GLA_HEADS
GLA_DK = GLA_DV // 2
GLA_GATE_RANK = 16
GLA_GATE_TEMP = 16.0
D_FF = 11 * D_MODEL // 4
CONV_W = 3
EPS = 1e-6
IN_SPLITS = (N_ATTN_HEADS * ATTN_HEAD_DIM, N_ATTN_HEADS * ATTN_HEAD_DIM, N_ATTN_HEADS * ATTN_HEAD_DIM,
             N_IDX_HEADS * IDX_DIM, IDX_DIM, N_IDX_HEADS,
             GLA_HEADS * GLA_DK, GLA_HEADS * GLA_DK, GLA_HEADS * GLA_DV, GLA_HEADS * GLA_DV, GLA_GATE_RANK)
IN_DIM = sum(IN_SPLITS)

kernel_name = 'hymba_dsa_gla_convffn_stream_step'


def rmsnorm(x, g):
    xf = x.astype(jnp.float32)
    y = xf * lax.rsqrt(jnp.mean(xf * xf, axis=-1, keepdims=True) + EPS)
    return (y * g.astype(jnp.float32)).astype(x.dtype)


def rope(x, pos, rot):
    half = rot // 2
    inv = ROPE_THETA ** (-jnp.arange(half, dtype=jnp.float32) / half)
    ang = pos.astype(jnp.float32)[:, None] * inv[None, :]
    cos = jnp.cos(ang)[None, :, None, :]
    sin = jnp.sin(ang)[None, :, None, :]
    xf = x.astype(jnp.float32)
    x1 = xf[..., :half]
    x2 = xf[..., half:rot]
    out = jnp.concatenate([x1 * cos - x2 * sin, x2 * cos + x1 * sin, xf[..., rot:]], axis=-1)
    return out.astype(x.dtype)


def dsa_attend(q, k, v, qi, ki, wi, q_pos, k_pos):
    B, T, H, Dh = q.shape
    L = k.shape[1]
    topk = min(TOPK_MAX, L // 4)
    qb = min(Q_BLOCK, T)
    nb = T // qb
    ki32 = ki.astype(jnp.float32)
    k_chunk = k_pos // CHUNK
    idx_scale = (IDX_DIM * N_IDX_HEADS) ** -0.5

    def blocks(a):
        return a.reshape((B, nb, qb) + a.shape[2:]).swapaxes(0, 1)

    def one_block(args):
        qs, qis, wis, qp = args
        q_chunk = qp // CHUNK
        logits = jnp.einsum('bthd,bsd->bths', qis.astype(jnp.float32), ki32)
        score = jnp.einsum('bths,bth->bts', jax.nn.relu(logits), wis.astype(jnp.float32)) * idx_scale
        admissible = k_chunk[None, :] <= q_chunk[:, None]
        score = jnp.where(admissible[None], score, -jnp.inf)
        _, idx = lax.top_k(score, topk)
        valid = jnp.take(k_chunk, idx) <= q_chunk[None, :, None]
        k_sel = jax.vmap(lambda kk, ii: kk[ii])(k, idx)
        v_sel = jax.vmap(lambda vv, ii: vv[ii])(v, idx)
        s = jnp.einsum('bthd,btkhd->bhtk', qs.astype(jnp.float32), k_sel.astype(jnp.float32)) * Dh ** -0.5
        s = jnp.where(valid[:, None], s, -jnp.inf)
        p = jax.nn.softmax(s, axis=-1)
        o = jnp.einsum('bhtk,btkhd->bthd', p, v_sel.astype(jnp.float32))
        return o.astype(q.dtype)

    out = lax.map(one_block, (blocks(q), blocks(qi), blocks(wi), q_pos.reshape(nb, qb)))
    return out.swapaxes(0, 1).reshape(B, T, H, Dh)


def gla_attend(s0, q, k, v, lg):
    B, T, H, dk = q.shape
    dv = v.shape[-1]
    C = min(CHUNK, T)
    n = T // C
    tri = jnp.tril(jnp.ones((C, C), dtype=bool))

    def to_chunks(a):
        return a.astype(jnp.float32).reshape((B, n, C) + a.shape[2:]).swapaxes(0, 1)

    def step(S, inp):
        qc, kc, vc, gc = inp
        b = jnp.cumsum(gc, axis=1)
        inter = jnp.einsum('bthk,bhkv->bthv', qc * jnp.exp(b), S)
        diff = b[:, :, None] - b[:, None, :]
        decay = jnp.exp(jnp.where(tri[None, :, :, None, None], diff, -jnp.inf))
        A = jnp.einsum('bthk,bshk,btshk->bhts', qc, kc, decay)
        intra = jnp.einsum('bhts,bshv->bthv', A, vc)
        bl = b[:, -1]
        S = jnp.exp(bl)[..., None] * S + jnp.einsum('bshk,bshv->bhkv', kc * jnp.exp(bl[:, None] - b), vc)
        return S, inter + intra

    S, o = lax.scan(step, s0.astype(jnp.float32), (to_chunks(q), to_chunks(k), to_chunks(v), to_chunks(lg)))
    o = o.swapaxes(0, 1).reshape(B, T, H, dv)
    return o, S.astype(s0.dtype)


def layer_forward(x, c, pos, past_k, past_v, past_ik, past_pos, gla_s0, conv_buf, lw):
    (w_ada, b_ada, g_mix, g_ffn, w_in, g_q, g_k, w_gate2, b_gate2, g_gla,
     w_out, w_up, w_conv, b_conv, w_down) = lw
    B, T, _ = x.shape
    mod = jax.nn.silu(c) @ w_ada + b_ada
    sh_m, sc_m, gt_m, sh_f, sc_f, gt_f = [m[:, None, :] for m in jnp.split(mod, 6, axis=-1)]

    h = rmsnorm(x, g_mix) * (1 + sc_m) + sh_m
    proj = h @ w_in
    offsets = np.cumsum(IN_SPLITS)[:-1].tolist()
    aq, ak, av, iq, ik, iw, gq, gk, gv, gr, glr = jnp.split(proj, offsets, axis=-1)

    aq = rope(rmsnorm(aq.reshape(B, T, N_ATTN_HEADS, ATTN_HEAD_DIM), g_q), pos, ROPE_DIM)
    ak = rope(rmsnorm(ak.reshape(B, T, N_ATTN_HEADS, ATTN_HEAD_DIM), g_k), pos, ROPE_DIM)
    av = av.reshape(B, T, N_ATTN_HEADS, ATTN_HEAD_DIM)
    iq = rope(iq.reshape(B, T, N_IDX_HEADS, IDX_DIM), pos, IDX_ROPE_DIM)
    ik = rope(ik[:, :, None, :], pos, IDX_ROPE_DIM)[:, :, 0, :]
    k_all = jnp.concatenate([past_k.astype(ak.dtype), ak], axis=1)
    v_all = jnp.concatenate([past_v.astype(av.dtype), av], axis=1)
    ik_all = jnp.concatenate([past_ik.astype(ik.dtype), ik], axis=1)
    kpos_all = jnp.concatenate([past_pos, pos])
    o_a = dsa_attend(aq, k_all, v_all, iq, ik_all, iw, pos, kpos_all).reshape(B, T, N_ATTN_HEADS * ATTN_HEAD_DIM)

    gq = gq.reshape(B, T, GLA_HEADS, GLA_DK) * GLA_DK ** -0.5
    gk = gk.reshape(B, T, GLA_HEADS, GLA_DK)
    gv = gv.reshape(B, T, GLA_HEADS, GLA_DV)
    lg = jax.nn.log_sigmoid((glr @ w_gate2 + b_gate2).astype(jnp.float32)) / GLA_GATE_TEMP
    lg = lg.reshape(B, T, GLA_HEADS, GLA_DK)
    o_g, s_new = gla_attend(gla_s0, gq, gk, gv, lg)
    o_g = rmsnorm(o_g.astype(x.dtype), g_gla) * jax.nn.silu(gr.reshape(B, T, GLA_HEADS, GLA_DV))
    o_g = o_g.reshape(B, T, GLA_HEADS * GLA_DV)

    x = x + gt_m * (jnp.concatenate([o_a, o_g], axis=-1) @ w_out)

    h2 = rmsnorm(x, g_ffn) * (1 + sc_f) + sh_f
    u = h2 @ w_up
    ext = jnp.concatenate([conv_buf.astype(u.dtype), u], axis=1)
    uc = b_conv + sum(w_conv[j] * ext[:, j:j + T] for j in range(CONV_W))
    ua, ub = jnp.split(uc, 2, axis=-1)
    x = x + gt_f * ((jax.nn.silu(ua) * ub) @ w_down)
    new_buf = ext[:, T:]
    return x, ak, av, ik, s_new, new_buf


def setup_inputs(seed: int = 0) -> dict:
    key = jax.random.key(seed)
    ks = jax.random.split(key, 24)

    def nrm(k, shape, s):
        return jax.random.normal(k, shape, jnp.float32) * s

    return {
        'x_prompt': nrm(ks[0], (BATCH, SEQ, D_MODEL), 1.0),
        'x_sample': nrm(ks[1], (DEC_BATCH, DEC_SEQ, D_MODEL), 1.0),
        'c_prompt': nrm(ks[2], (BATCH, D_MODEL), 1.0),
        'c_sample': nrm(ks[3], (DEC_BATCH, D_MODEL), 1.0),
        'cache_k': nrm(ks[4], (DEPTH, DEC_BATCH, PAST_LEN, N_ATTN_HEADS, ATTN_HEAD_DIM), 1.0),
        'cache_v': nrm(ks[5], (DEPTH, DEC_BATCH, PAST_LEN, N_ATTN_HEADS, ATTN_HEAD_DIM), 1.0),
        'cache_idx_k': nrm(ks[6], (DEPTH, DEC_BATCH, PAST_LEN, IDX_DIM), 1.0),
        'state_gla': nrm(ks[7], (DEPTH, DEC_BATCH, GLA_HEADS, GLA_DK, GLA_DV), 0.5),
        'state_ffn_conv': nrm(ks[8], (DEPTH, DEC_BATCH, CONV_W - 1, 2 * D_FF), 1.0),
        'w_ada': nrm(ks[9], (DEPTH, D_MODEL, 6 * D_MODEL), 0.5 * D_MODEL ** -0.5),
        'b_ada': nrm(ks[10], (DEPTH, 6 * D_MODEL), 0.01),
        'g_mix': 1.0 + nrm(ks[11], (DEPTH, D_MODEL), 0.02),
        'g_ffn': 1.0 + nrm(ks[12], (DEPTH, D_MODEL), 0.02),
        'w_in': nrm(ks[13], (DEPTH, D_MODEL, IN_DIM), D_MODEL ** -0.5),
        'g_q': 1.0 + nrm(ks[14], (DEPTH, ATTN_HEAD_DIM), 0.02),
        'g_k': 1.0 + nrm(ks[15], (DEPTH, ATTN_HEAD_DIM), 0.02),
        'w_gate2': nrm(ks[16], (DEPTH, GLA_GATE_RANK, GLA_HEADS * GLA_DK), GLA_GATE_RANK ** -0.5),
        'b_gate2': nrm(ks[17], (DEPTH, GLA_HEADS * GLA_DK), 0.01),
        'g_gla': 1.0 + nrm(ks[18], (DEPTH, GLA_DV), 0.02),
        'w_out': nrm(ks[19], (DEPTH, MIX_WIDTH, D_MODEL), MIX_WIDTH ** -0.5),
        'w_up': nrm(ks[20], (DEPTH, D_MODEL, 2 * D_FF), D_MODEL ** -0.5),
        'w_conv': nrm(ks[21], (DEPTH, CONV_W, 2 * D_FF), CONV_W ** -0.5),
        'b_conv': nrm(ks[22], (DEPTH, 2 * D_FF), 0.01),
        'w_down': nrm(ks[23], (DEPTH, D_FF, D_MODEL), D_FF ** -0.5),
    }


def reference(x_prompt, x_sample, c_prompt, c_sample, cache_k, cache_v, cache_idx_k, state_gla,
              state_ffn_conv, w_ada, b_ada, g_mix, g_ffn, w_in, g_q, g_k, w_gate2, b_gate2, g_gla,
              w_out, w_up, w_conv, b_conv, w_down):
    B, S, _ = x_prompt.shape
    Ts = x_sample.shape[1]
    P = cache_k.shape[2]
    dt = x_prompt.dtype
    pos_p = jnp.arange(S, dtype=jnp.int32)
    pos_s = P + jnp.arange(Ts, dtype=jnp.int32)
    past_pos_s = jnp.arange(P, dtype=jnp.int32)
    empty_pos = jnp.zeros((0,), jnp.int32)
    empty_kv = jnp.zeros((B, 0, N_ATTN_HEADS, ATTN_HEAD_DIM), dt)
    empty_ik = jnp.zeros((B, 0, IDX_DIM), dt)
    gla0 = jnp.zeros((B, GLA_HEADS, GLA_DK, GLA_DV), dt)
    conv0 = jnp.zeros((B, CONV_W - 1, 2 * D_FF), dt)

    y_p, y_s = x_prompt, x_sample
    kp, vp, ikp, sp, cp = [], [], [], [], []
    ksm, vsm, iks, ssm, csm = [], [], [], [], []
    for l in range(DEPTH):
        lw = (w_ada[l], b_ada[l], g_mix[l], g_ffn[l], w_in[l], g_q[l], g_k[l], w_gate2[l], b_gate2[l],
              g_gla[l], w_out[l], w_up[l], w_conv[l], b_conv[l], w_down[l])
        y_p, k1, v1, ik1, s1, c1 = layer_forward(y_p, c_prompt, pos_p, empty_kv, empty_kv, empty_ik,
                                                 empty_pos, gla0, conv0, lw)
        y_s, k2, v2, ik2, s2, c2 = layer_forward(y_s, c_sample, pos_s, cache_k[l], cache_v[l], cache_idx_k[l],
                                                 past_pos_s, state_gla[l], state_ffn_conv[l], lw)
        kp.append(k1); vp.append(v1); ikp.append(ik1); sp.append(s1); cp.append(c1)
        ksm.append(k2); vsm.append(v2); iks.append(ik2); ssm.append(s2); csm.append(c2)

    return (y_p, y_s,
            jnp.stack(kp), jnp.stack(vp), jnp.stack(ikp), jnp.stack(sp), jnp.stack(cp),
            jnp.stack(ksm), jnp.stack(vsm), jnp.stack(iks), jnp.stack(ssm), jnp.stack(csm))
```

```python
import functools
import math

import jax
import jax.numpy as jnp
import numpy as np
from jax import lax
from jax.experimental import pallas as pl
from jax.experimental.pallas import tpu as pltpu

F32 = jnp.float32
BF16 = jnp.bfloat16

CHUNK = 64
N_ATTN_HEADS = 8
ATTN_HEAD_DIM = 128
ROPE_DIM = 32
ROPE_THETA = 500000.0
N_IDX_HEADS = 16
IDX_DIM = 64
IDX_ROPE_DIM = 16
TOPK_MAX = 256
GLA_HEADS = 4
GLA_DK = 128
GLA_DV = 256
GLA_GATE_RANK = 16
GLA_GATE_TEMP = 16.0
CONV_W = 3
EPS = 1e-6

LANES = 128
SUBLANES = 8
VMEM_LIMIT_BYTES = 56 * 1024 * 1024

MASK_NEG = -1e30
F32_LOWEST = -3.4028235e38


def _cparams(sem):
    return pltpu.CompilerParams(dimension_semantics=sem, vmem_limit_bytes=VMEM_LIMIT_BYTES)


def _sigmoid(x):
    return 1.0 / (1.0 + jnp.exp(-x))


def _silu(x):
    return x * _sigmoid(x)


def _ada_kernel(c_ref, w_ref, b_ref, o_ref):
    s = _silu(c_ref[...]).astype(BF16)
    o_ref[...] = jnp.dot(s, w_ref[...].astype(BF16), preferred_element_type=F32) + b_ref[...]


def _ada(c_all, w_ada, b_ada):
    rows, d = c_all.shape
    n = w_ada.shape[1]
    tn = 1536 if n % 1536 == 0 else n
    return pl.pallas_call(
        _ada_kernel,
        out_shape=jax.ShapeDtypeStruct((rows, n), F32),
        grid=(n // tn,),
        in_specs=[
            pl.BlockSpec((rows, d), lambda j: (0, 0)),
            pl.BlockSpec((d, tn), lambda j: (0, j)),
            pl.BlockSpec((1, tn), lambda j: (0, j)),
        ],
        out_specs=pl.BlockSpec((rows, tn), lambda j: (0, j)),
        compiler_params=_cparams(("arbitrary",)),
        name="ada_mod",
    )(c_all, w_ada, b_ada.reshape(1, n))


def _rope_table(pos, half, period, active_lanes):
    inv = ROPE_THETA ** (-jnp.arange(half, dtype=F32) / half)
    ang = pos.astype(F32)[:, None] * inv[None, :]
    cos, sin = jnp.cos(ang), jnp.sin(ang)
    lane = np.arange(LANES)
    within = lane % period
    idx = jnp.asarray(within % half)
    first = jnp.asarray((within < half) & (lane < active_lanes))
    second = jnp.asarray((within >= half) & (within < 2 * half) & (lane < active_lanes))
    cos_l, sin_l = cos[:, idx], sin[:, idx]
    c = jnp.where(first | second, cos_l, 1.0)
    s1 = jnp.where(first, -sin_l, 0.0)
    s2 = jnp.where(second, sin_l, 0.0)
    return jnp.stack([c, s1, s2]).astype(F32)


def _rope_apply(v, tab, half):
    return (v * tab[0] + pltpu.roll(v, LANES - half, 1) * tab[1] + pltpu.roll(v, half, 1) * tab[2])


def _tile_rows(tab_ref, bb):
    t = tab_ref[...]
    if bb == 1:
        return t
    tb = t.shape[1]
    return jnp.broadcast_to(t[:, None], (3, bb, tb, LANES)).reshape(3, bb * tb, LANES)


def _inproj_kernel(x_ref, mod_ref, gmix_ref, wbig_ref, wsmall_ref, gq_ref, gk_ref,
                   ropea_ref, ropei_ref, ropes_ref, wg2_ref, bg2_ref,
                   q_out, k_out, v_out, iq_out, small_out, lg_out, gq_out, gk_out, gv_out, gr_out,
                   h_scr):
    j = pl.program_id(2)
    bb, tb, d = x_ref.shape
    rows = bb * tb

    @pl.when(j == 0)
    def _():
        x = x_ref[...]
        ms = jnp.mean(x * x, axis=-1, keepdims=True)
        y = x * lax.rsqrt(ms + EPS) * gmix_ref[...]
        mod = mod_ref[...]
        h = y * (1.0 + mod[:, :, d:2 * d]) + mod[:, :, 0:d]
        hb = h.astype(BF16).reshape(rows, d)
        h_scr[...] = hb
        small = jnp.dot(hb, wsmall_ref[...], preferred_element_type=F32)
        small = _rope_apply(small, _tile_rows(ropes_ref, bb), IDX_ROPE_DIM // 2)
        small_out[...] = small.reshape(bb, tb, LANES)
        pre = jnp.dot(small.astype(BF16), wg2_ref[...], preferred_element_type=F32) + bg2_ref[...]
        lsig = jnp.minimum(pre, 0.0) - jnp.log(1.0 + jnp.exp(-jnp.abs(pre)))
        lg_out[...] = (lsig / GLA_GATE_TEMP).reshape(bb, tb, lg_out.shape[-1])

    y = jnp.dot(h_scr[...], wbig_ref[...], preferred_element_type=F32)

    def headnorm_rope(g_ref, scale):
        tab = _tile_rows(ropea_ref, bb)
        outs = []
        for hh in range(N_ATTN_HEADS):
            yh = y[:, hh * LANES:(hh + 1) * LANES]
            n = yh * lax.rsqrt(jnp.mean(yh * yh, axis=-1, keepdims=True) + EPS) * g_ref[...]
            r = _rope_apply(n, tab, ROPE_DIM // 2)
            outs.append(r * scale if scale != 1.0 else r)
        return outs

    @pl.when(j == 0)
    def _():
        for hh, r in enumerate(headnorm_rope(gq_ref, ATTN_HEAD_DIM ** -0.5)):
            q_out[:, :, hh * LANES:(hh + 1) * LANES] = r.astype(BF16).reshape(bb, tb, LANES)

    @pl.when(j == 1)
    def _():
        for hh, r in enumerate(headnorm_rope(gk_ref, 1.0)):
            k_out[:, :, hh * LANES:(hh + 1) * LANES] = r.reshape(bb, tb, LANES)

    @pl.when(j == 2)
    def _():
        v_out[...] = y.reshape(bb, tb, y.shape[-1])

    @pl.when(j == 3)
    def _():
        tab = _tile_rows(ropei_ref, bb)
        for c in range(y.shape[-1] // LANES):
            r = _rope_apply(y[:, c * LANES:(c + 1) * LANES], tab, IDX_ROPE_DIM // 2)
            iq_out[:, :, c * LANES:(c + 1) * LANES] = r.astype(BF16).reshape(bb, tb, LANES)

    @pl.when(j == 4)
    def _():
        half = y.shape[-1] // 2
        gq_out[...] = (y[:, :half] * GLA_DK ** -0.5).reshape(bb, tb, half)
        gk_out[...] = y[:, half:].reshape(bb, tb, half)

    @pl.when(j == 5)
    def _():
        gv_out[...] = y.astype(BF16).reshape(bb, tb, y.shape[-1])

    @pl.when(j == 6)
    def _():
        gr_out[...] = y.reshape(bb, tb, y.shape[-1])


def _inproj(x, mod, g_mix, wbig, wsmall, g_q, g_k, ropea, ropei, ropes, wg2p, bg2, bb, tb):
    b, t, d = x.shape
    gw = 1024
    ngroups = wbig.shape[1] // gw
    glak = GLA_HEADS * GLA_DK
    rows = bb * tb
    tok = lambda w, dt: jax.ShapeDtypeStruct((b, t, w), dt)
    tokspec = lambda w: pl.BlockSpec((bb, tb, w), lambda i, m, j: (i, m, 0))
    const = lambda shp: pl.BlockSpec(shp, lambda i, m, j: tuple(0 for _ in shp))
    tabspec = pl.BlockSpec((3, tb, LANES), lambda i, m, j: (0, m, 0))
    out_shapes = (tok(1024, BF16), tok(1024, F32), tok(1024, F32), tok(1024, BF16), tok(LANES, F32),
                  tok(glak, F32), tok(glak, F32), tok(glak, F32), tok(1024, BF16), tok(1024, F32))
    out_specs = tuple(tokspec(s.shape[-1]) for s in out_shapes)
    return pl.pallas_call(
        _inproj_kernel,
        out_shape=out_shapes,
        grid=(b // bb, t // tb, ngroups),
        in_specs=[
            tokspec(d),
            pl.BlockSpec((bb, 1, mod.shape[-1]), lambda i, m, j: (i, 0, 0)),
            const((1, 1, d)),
            pl.BlockSpec((d, gw), lambda i, m, j: (0, j)),
            const((d, LANES)),
            const((1, LANES)), const((1, LANES)),
            tabspec, tabspec, tabspec,
            const((LANES, glak)), const((1, glak)),
        ],
        out_specs=out_specs,
        scratch_shapes=[pltpu.VMEM((rows, d), BF16)],
        compiler_params=_cparams(("arbitrary", "arbitrary", "arbitrary")),
        name="norm_inproj",
    )(x, mod, g_mix.reshape(1, 1, d), wbig, wsmall, g_q.reshape(1, LANES), g_k.reshape(1, LANES),
      ropea, ropei, ropes, wg2p, bg2.reshape(1, glak))


def _dsa_kernel(iq_ref, iw_ref, ikt_ref, q_ref, k_hbm, v_hbm, o_ref,
                u_scr, wb_scr, kbuf, vbuf, sem, m_scr, l_scr, acc_scr,
                *, past, seq_len, topk, tq, tk):
    b = pl.program_id(0)
    i = pl.program_id(1)
    nkt = u_scr.shape[0]
    lane_chunks = tk // LANES

    last_chunk = (past + (i + 1) * tq - 1) // CHUNK
    n_adm = jnp.minimum(seq_len, (last_chunk + 1) * CHUNK)
    nk = jnp.minimum((n_adm + tk - 1) // tk, nkt)

    def kv_copy(j, slot):
        return (pltpu.make_async_copy(k_hbm.at[b, pl.ds(j * tk, tk)], kbuf.at[slot], sem.at[0, slot]),
                pltpu.make_async_copy(v_hbm.at[b, pl.ds(j * tk, tk)], vbuf.at[slot], sem.at[1, slot]))

    for cp in kv_copy(0, 0):
        cp.start()

    iw = iw_ref[0]
    for hh in range(N_IDX_HEADS):
        wb_scr[hh] = jnp.broadcast_to(iw[:, hh:hh + 1], (tq, LANES))

    q_chunk = (past + i * tq + lax.broadcasted_iota(jnp.int32, (tq, LANES), 0)) // CHUNK
    lane_iota = lax.broadcasted_iota(jnp.int32, (tq, LANES), 1)
    idx_scale = (IDX_DIM * N_IDX_HEADS) ** -0.5

    def score_tile(j, carry):
        ikt = ikt_ref[0, j]
        parts = [jnp.zeros((tq, LANES), F32) for _ in range(lane_chunks)]
        for hh in range(N_IDX_HEADS):
            logits = jnp.dot(iq_ref[0, hh], ikt, preferred_element_type=F32)
            w = wb_scr[hh]
            for c in range(lane_chunks):
                parts[c] = parts[c] + w * jnp.maximum(logits[:, c * LANES:(c + 1) * LANES], 0.0)
        for c in range(lane_chunks):
            kpos = j * tk + c * LANES + lane_iota
            adm = (kpos // CHUNK <= q_chunk) & (kpos < seq_len)
            u_scr[j, :, c * LANES:(c + 1) * LANES] = jnp.where(adm, parts[c] * idx_scale, -jnp.inf)
        return carry

    lax.fori_loop(0, nk, score_tile, 0)

    int_min = jnp.int32(-2 ** 31)

    def thr_of(t_biased):
        mono = t_biased ^ int_min
        bits = jnp.where(mono >= 0, mono, mono ^ jnp.int32(0x7FFFFFFF))
        return lax.bitcast_convert_type(bits, F32)

    def search_step(it, t_biased):
        cand = t_biased | lax.shift_left(jnp.int32(1), jnp.int32(31) - it)
        thr = jnp.broadcast_to(thr_of(cand), (tq, LANES))

        def count_tile(j, cnt):
            for c in range(lane_chunks):
                cnt = cnt + jnp.where(u_scr[j, :, c * LANES:(c + 1) * LANES] >= thr, 1.0, 0.0)
            return cnt

        cnt = lax.fori_loop(0, nk, count_tile, jnp.zeros((tq, LANES), F32))
        total = jnp.sum(cnt, axis=-1, keepdims=True)
        return jnp.where(total >= float(topk), cand, t_biased)

    t_fin = lax.fori_loop(0, 32, search_step, jnp.zeros((tq, 1), jnp.int32))
    few = (t_fin >= 0) & (t_fin < jnp.int32(0x00800000))
    thr_fin = jnp.where(few, F32_LOWEST, jnp.maximum(thr_of(t_fin), F32_LOWEST))
    thr = jnp.broadcast_to(thr_fin, (tq, LANES))

    m_scr[...] = jnp.full(m_scr.shape, MASK_NEG, F32)
    l_scr[...] = jnp.zeros(l_scr.shape, F32)
    acc_scr[...] = jnp.zeros(acc_scr.shape, F32)

    def attend_tile(j, carry):
        slot = j % 2

        @pl.when(j + 1 < nk)
        def _():
            for cp in kv_copy(j + 1, 1 - slot):
                cp.start()

        for cp in kv_copy(j, slot):
            cp.wait()

        bias = jnp.concatenate(
            [jnp.where(u_scr[j, :, c * LANES:(c + 1) * LANES] >= thr, 0.0, MASK_NEG)
             for c in range(lane_chunks)], axis=1)
        for hh in range(N_ATTN_HEADS):
            hs = slice(hh * LANES, (hh + 1) * LANES)
            s = lax.dot_general(q_ref[0, :, hs], kbuf[slot, :, hs], (((1,), (1,)), ((), ())),
                                preferred_element_type=F32) + bias
            m_prev = m_scr[hh]
            m_new = jnp.maximum(m_prev, jnp.max(s, axis=-1, keepdims=True))
            alpha = jnp.exp(m_prev - m_new)
            p = jnp.exp(s - m_new)
            l_scr[hh] = alpha * l_scr[hh] + jnp.sum(p, axis=-1, keepdims=True)
            acc_scr[:, hs] = alpha * acc_scr[:, hs] + jnp.dot(p.astype(BF16), vbuf[slot, :, hs],
                                                              preferred_element_type=F32)
            m_scr[hh] = m_new
        return carry

    lax.fori_loop(0, nk, attend_tile, 0)

    for hh in range(N_ATTN_HEADS):
        hs = slice(hh * LANES, (hh + 1) * LANES)
        o_ref[0, :, hs] = (acc_scr[:, hs] / l_scr[hh]).astype(o_ref.dtype)


def _dsa(iq_h, iw, ikt, q_bf, k_bf, v_bf, *, past, seq_len, topk, tq, tk):
    b, _, t, _ = iq_h.shape
    nkt = ikt.shape[1]
    hd = N_ATTN_HEADS * ATTN_HEAD_DIM
    kern = functools.partial(_dsa_kernel, past=past, seq_len=seq_len, topk=topk, tq=tq, tk=tk)
    return pl.pallas_call(
        kern,
        out_shape=jax.ShapeDtypeStruct((b, t, hd), BF16),
        grid=(b, t // tq),
        in_specs=[
            pl.BlockSpec((1, N_IDX_HEADS, tq, IDX_DIM), lambda bi, i: (bi, 0, i, 0)),
            pl.BlockSpec((1, tq, N_IDX_HEADS), lambda bi, i: (bi, i, 0)),
            pl.BlockSpec((1, nkt, IDX_DIM, tk), lambda bi, i: (bi, 0, 0, 0)),
            pl.BlockSpec((1, tq, hd), lambda bi, i: (bi, i, 0)),
            pl.BlockSpec(memory_space=pl.ANY),
            pl.BlockSpec(memory_space=pl.ANY),
        ],
        out_specs=pl.BlockSpec((1, tq, hd), lambda bi, i: (bi, i, 0)),
        scratch_shapes=[
            pltpu.VMEM((nkt, tq, tk), F32),
            pltpu.VMEM((N_IDX_HEADS, tq, LANES), F32),
            pltpu.VMEM((2, tk, hd), BF16),
            pltpu.VMEM((2, tk, hd), BF16),
            pltpu.SemaphoreType.DMA((2, 2)),
            pltpu.VMEM((N_ATTN_HEADS, tq, 1), F32),
            pltpu.VMEM((N_ATTN_HEADS, tq, 1), F32),
            pltpu.VMEM((tq, hd), F32),
        ],
        compiler_params=_cparams(("arbitrary", "arbitrary")),
        name="dsa_attend",
    )(iq_h, iw, ikt, q_bf, k_bf, v_bf)


def _gla_kernel(q_ref, k_ref, v_ref, lg_ref, gr_ref, g_ref, s0_ref, o_ref, sT_out, st_scr, *, n_chunks):
    tstep = pl.program_id(2)

    @pl.when(tstep == 0)
    def _():
        st_scr[...] = s0_ref[0, 0]

    row = lax.broadcasted_iota(jnp.int32, (CHUNK, GLA_DK), 0)
    tri = (lax.broadcasted_iota(jnp.int32, (CHUNK, CHUNK), 0)
           >= lax.broadcasted_iota(jnp.int32, (CHUNK, CHUNK), 1))

    for c in range(n_chunks):
        rs = slice(c * CHUNK, (c + 1) * CHUNK)
        bcum = lg_ref[0, rs, :]
        shift = 1
        while shift < CHUNK:
            bcum = bcum + jnp.where(row >= shift, pltpu.roll(bcum, shift, 0), 0.0)
            shift *= 2
        bl = bcum[CHUNK - 1:CHUNK, :]
        q = q_ref[0, rs, :]
        k = k_ref[0, rs, :]
        v = v_ref[0, rs, :]
        qe = (q * jnp.exp(bcum)).astype(BF16)
        ke = (k * jnp.exp(-bcum)).astype(BF16)
        kd = (k * jnp.exp(bl - bcum)).astype(BF16)
        st = st_scr[...]
        inter = lax.dot_general(qe, st.astype(BF16), (((1,), (1,)), ((), ())), preferred_element_type=F32)
        a = lax.dot_general(qe, ke, (((1,), (1,)), ((), ())), preferred_element_type=F32)
        a = jnp.where(tri, a, 0.0)
        o = inter + jnp.dot(a.astype(BF16), v, preferred_element_type=F32)
        st_scr[...] = st * jnp.exp(bl) + lax.dot_general(v, kd, (((0,), (0,)), ((), ())),
                                                          preferred_element_type=F32)
        n = o * lax.rsqrt(jnp.mean(o * o, axis=-1, keepdims=True) + EPS) * g_ref[...]
        o_ref[0, rs, :] = (n * _silu(gr_ref[0, rs, :])).astype(o_ref.dtype)

    @pl.when(tstep == pl.num_programs(2) - 1)
    def _():
        sT_out[0, 0] = st_scr[...]


def _gla(gq, gk, gv, lg, gr, g_gla, s0t, tg):
    b, t, _ = gq.shape
    kern = functools.partial(_gla_kernel, n_chunks=tg // CHUNK)
    tok = lambda w: pl.BlockSpec((1, tg, w), lambda bi, h, m: (bi, m, h))
    st_spec = pl.BlockSpec((1, 1, GLA_DV, GLA_DK), lambda bi, h, m: (bi, h, 0, 0))
    return pl.pallas_call(
        kern,
        out_shape=(jax.ShapeDtypeStruct((b, t, GLA_HEADS * GLA_DV), BF16),
                   jax.ShapeDtypeStruct((b, GLA_HEADS, GLA_DV, GLA_DK), F32)),
        grid=(b, GLA_HEADS, t // tg),
        in_specs=[tok(GLA_DK), tok(GLA_DK), tok(GLA_DV), tok(GLA_DK), tok(GLA_DV),
                  pl.BlockSpec((1, GLA_DV), lambda bi, h, m: (0, 0)), st_spec],
        out_specs=(tok(GLA_DV), st_spec),
        scratch_shapes=[pltpu.VMEM((GLA_DV, GLA_DK), F32)],
        compiler_params=_cparams(("arbitrary", "arbitrary", "arbitrary")),
        name="gla",
    )(gq, gk, gv, lg, gr, g_gla.reshape(1, GLA_DV), s0t)


def _outproj_kernel(oa_ref, og_ref, wa_ref, wg_ref, x_ref, mod_ref, gffn_ref, x1_out, h2_out):
    bb, tb, d = x_ref.shape
    rows = bb * tb
    oa = oa_ref[...].reshape(rows, oa_ref.shape[-1])
    og = og_ref[...].reshape(rows, og_ref.shape[-1])
    mix = (jnp.dot(oa, wa_ref[...], preferred_element_type=F32)
           + jnp.dot(og, wg_ref[...], preferred_element_type=F32)).reshape(bb, tb, d)
    mod = mod_ref[...]
    x1 = x_ref[...] + mod[:, :, 2 * d:3 * d] * mix
    x1_out[...] = x1
    y = x1 * lax.rsqrt(jnp.mean(x1 * x1, axis=-1, keepdims=True) + EPS) * gffn_ref[...]
    h2_out[...] = (y * (1.0 + mod[:, :, 4 * d:5 * d]) + mod[:, :, 3 * d:4 * d]).astype(BF16)


def _outproj(o_a, o_g, wout_a, wout_g, x, mod, g_ffn, bb, tb):
    b, t, d = x.shape
    tokspec = lambda w: pl.BlockSpec((bb, tb, w), lambda i, m: (i, m, 0))
    const = lambda shp: pl.BlockSpec(shp, lambda i, m: tuple(0 for _ in shp))
    return pl.pallas_call(
        _outproj_kernel,
        out_shape=(jax.ShapeDtypeStruct((b, t, d), F32), jax.ShapeDtypeStruct((b, t, d), BF16)),
        grid=(b // bb, t // tb),
        in_specs=[tokspec(o_a.shape[-1]), tokspec(o_g.shape[-1]), const(wout_a.shape), const(wout_g.shape),
                  tokspec(d), pl.BlockSpec((bb, 1, mod.shape[-1]), lambda i, m: (i, 0, 0)), const((1, 1, d))],
        out_specs=(tokspec(d), tokspec(d)),
        compiler_params=_cparams(("arbitrary", "arbitrary")),
        name="outproj_norm",
    )(o_a, o_g, wout_a, wout_g, x, mod, g_ffn.reshape(1, 1, d))


def _ffn_kernel(h_ref, halo_ref, wa_ref, wb_ref, wca_ref, wcb_ref, bca_ref, bcb_ref, sa_ref, sb_ref,
                wd_ref, x1_ref, mod_ref, y_out, nba_out, nbb_out, acc_scr):
    m = pl.program_id(1)
    f = pl.program_id(2)
    bb, tb, d = h_ref.shape
    rows = bb * tb
    tf = wa_ref.shape[1]
    hb = h_ref[...].reshape(rows, d)
    hh = halo_ref[...].reshape(bb * SUBLANES, d)
    tpos = lax.broadcasted_iota(jnp.int32, (rows, tf), 0) % tb

    def conv_branch(w_ref, wc_ref, bc_ref, s_ref, nb_out):
        u = jnp.dot(hb, w_ref[...], preferred_element_type=F32)
        pu = jnp.dot(hh, w_ref[...], preferred_element_type=F32).reshape(bb, SUBLANES, tf)
        prev = jnp.where(m == 0, s_ref[...], pu[:, SUBLANES - (CONV_W - 1):, :])
        bcast = lambda r: jnp.broadcast_to(prev[:, r:r + 1, :], (bb, tb, tf)).reshape(rows, tf)
        p0, p1 = bcast(0), bcast(1)
        u1 = jnp.where(tpos == 0, p1, pltpu.roll(u, 1, 0))
        u2 = jnp.where(tpos == 0, p0, jnp.where(tpos == 1, p1, pltpu.roll(u, 2, 0)))
        wc = wc_ref[...]
        nb_out[:, 0] = u.reshape(bb, tb, tf)[:, tb - (CONV_W - 1):, :]
        return bc_ref[...] + wc[0:1] * u2 + wc[1:2] * u1 + wc[2:3] * u

    ua = conv_branch(wa_ref, wca_ref, bca_ref, sa_ref, nba_out)
    ub = conv_branch(wb_ref, wcb_ref, bcb_ref, sb_ref, nbb_out)
    act = (_silu(ua) * ub).astype(BF16)
    part = jnp.dot(act, wd_ref[...], preferred_element_type=F32)

    @pl.when(f == 0)
    def _():
        acc_scr[...] = part

    @pl.when(f > 0)
    def _():
        acc_scr[...] += part

    @pl.when(f == pl.num_programs(2) - 1)
    def _():
        mod = mod_ref[...]
        y_out[...] = x1_ref[...] + mod[:, :, 5 * d:6 * d] * acc_scr[...].reshape(bb, tb, d)


def _ffn(h2, w_up, w_conv, b_conv, conv_state, w_down, x1, mod, bb, tb, tf):
    b, t, d = x1.shape
    dff = w_down.shape[0]
    nf = dff // tf
    hb = tb // SUBLANES
    tokspec = lambda w: pl.BlockSpec((bb, tb, w), lambda i, m, f: (i, m, 0))
    a_col = lambda r: pl.BlockSpec((r, tf), lambda i, m, f: (0, f))
    b_col = lambda r: pl.BlockSpec((r, tf), lambda i, m, f: (0, nf + f))
    st_a = pl.BlockSpec((bb, CONV_W - 1, tf), lambda i, m, f: (i, 0, f))
    st_b = pl.BlockSpec((bb, CONV_W - 1, tf), lambda i, m, f: (i, 0, nf + f))
    nb_spec = pl.BlockSpec((bb, 1, CONV_W - 1, tf), lambda i, m, f: (i, m, 0, f))
    bc2 = b_conv.reshape(1, 2 * dff)
    return pl.pallas_call(
        _ffn_kernel,
        out_shape=(jax.ShapeDtypeStruct((b, t, d), F32),
                   jax.ShapeDtypeStruct((b, t // tb, CONV_W - 1, dff), F32),
                   jax.ShapeDtypeStruct((b, t // tb, CONV_W - 1, dff), F32)),
        grid=(b // bb, t // tb, nf),
        in_specs=[
            tokspec(d),
            pl.BlockSpec((bb, SUBLANES, d), lambda i, m, f: (i, jnp.maximum(m * hb - 1, 0), 0)),
            a_col(d), b_col(d), a_col(CONV_W), b_col(CONV_W), a_col(1), b_col(1), st_a, st_b,
            pl.BlockSpec((tf, d), lambda i, m, f: (f, 0)),
            tokspec(d),
            pl.BlockSpec((bb, 1, mod.shape[-1]), lambda i, m, f: (i, 0, 0)),
        ],
        out_specs=(tokspec(d), nb_spec, nb_spec),
        scratch_shapes=[pltpu.VMEM((bb * tb, d), F32)],
        compiler_params=_cparams(("arbitrary", "arbitrary", "arbitrary")),
        name="conv_ffn",
    )(h2, h2, w_up, w_up, w_conv, w_conv, bc2, bc2, conv_state, conv_state, w_down, x1, mod)


def _prep_weights(w_in, w_gate2, w_out, w_up, w_down):
    hd = N_ATTN_HEADS * ATTN_HEAD_DIM
    sizes = (hd, hd, hd, N_IDX_HEADS * IDX_DIM, IDX_DIM, N_IDX_HEADS,
             GLA_HEADS * GLA_DK, GLA_HEADS * GLA_DK, GLA_HEADS * GLA_DV, GLA_HEADS * GLA_DV, GLA_GATE_RANK)
    offs = np.concatenate([[0], np.cumsum(sizes)])
    col = lambda n: w_in[:, int(offs[n]):int(offs[n + 1])]
    wbig = jnp.concatenate([col(0), col(1), col(2), col(3), col(6), col(7), col(8), col(9)], axis=1).astype(BF16)
    pad = LANES - (IDX_DIM + N_IDX_HEADS + GLA_GATE_RANK)
    wsmall = jnp.concatenate([col(4), col(5), col(10), jnp.zeros((w_in.shape[0], pad), w_in.dtype)],
                             axis=1).astype(BF16)
    lo = IDX_DIM + N_IDX_HEADS
    wg2p = jnp.zeros((LANES, w_gate2.shape[1]), F32).at[lo:lo + GLA_GATE_RANK].set(w_gate2).astype(BF16)
    return dict(wbig=wbig, wsmall=wsmall, wg2p=wg2p,
                wout_a=w_out[:hd].astype(BF16), wout_g=w_out[hd:].astype(BF16),
                w_up=w_up.astype(BF16), w_down=w_down.astype(BF16))


def _layer(x, mod, past, past_k, past_v, past_ik, gla_s0, conv_state, lw, pw, *, bb, tb, tq, tk, tg, tf):
    (g_mix, g_ffn, g_q, g_k, b_gate2, g_gla, w_conv, b_conv) = lw
    b, t, d = x.shape
    hd = N_ATTN_HEADS * ATTN_HEAD_DIM
    pos = past + jnp.arange(t, dtype=jnp.int32)
    ropea = _rope_table(pos, ROPE_DIM // 2, LANES, LANES)
    ropei = _rope_table(pos, IDX_ROPE_DIM // 2, IDX_DIM, LANES)
    ropes = _rope_table(pos, IDX_ROPE_DIM // 2, LANES, IDX_DIM)

    (q_bf, k, v, iq, small, lg, gq, gk, gv, gr) = _inproj(
        x, mod, g_mix, pw["wbig"], pw["wsmall"], g_q, g_k, ropea, ropei, ropes, pw["wg2p"], b_gate2, bb, tb)
    ik = small[:, :, :IDX_DIM]
    iw = small[:, :, IDX_DIM:IDX_DIM + N_IDX_HEADS]

    k_bf, v_bf, ik_bf = k.astype(BF16), v.astype(BF16), ik.astype(BF16)
    if past_k is not None:
        k_bf = jnp.concatenate([past_k.reshape(b, past, hd).astype(BF16), k_bf], axis=1)
        v_bf = jnp.concatenate([past_v.reshape(b, past, hd).astype(BF16), v_bf], axis=1)
        ik_bf = jnp.concatenate([past_ik.astype(BF16), ik_bf], axis=1)
    seq_len = past + t
    lp = -(-seq_len // tk) * tk
    if lp != seq_len:
        padw = ((0, 0), (0, lp - seq_len), (0, 0))
        k_bf, v_bf, ik_bf = jnp.pad(k_bf, padw), jnp.pad(v_bf, padw), jnp.pad(ik_bf, padw)
    ikt = ik_bf.reshape(b, lp // tk, tk, IDX_DIM).transpose(0, 1, 3, 2)
    iq_h = iq.reshape(b, t, N_IDX_HEADS, IDX_DIM).transpose(0, 2, 1, 3)
    topk = min(TOPK_MAX, seq_len // 4)
    o_a = _dsa(iq_h, iw, ikt, q_bf, k_bf, v_bf, past=past, seq_len=seq_len, topk=topk, tq=tq, tk=tk)

    s0t = jnp.swapaxes(gla_s0, -1, -2)
    o_g, s_t = _gla(gq, gk, gv, lg, gr, g_gla, s0t, tg)
    s_new = jnp.swapaxes(s_t, -1, -2)

    x1, h2 = _outproj(o_a, o_g, pw["wout_a"], pw["wout_g"], x, mod, g_ffn, bb, tb)
    y, nb_a, nb_b = _ffn(h2, pw["w_up"], w_conv, b_conv, conv_state, pw["w_down"], x1, mod, bb, tb, tf)
    new_buf = jnp.concatenate([nb_a[:, -1], nb_b[:, -1]], axis=-1)
    return (y, k.reshape(b, t, N_ATTN_HEADS, ATTN_HEAD_DIM), v.reshape(b, t, N_ATTN_HEADS, ATTN_HEAD_DIM),
            ik, s_new, new_buf)


def _tiles(b, t, seq_len):
    if t >= 512:
        bb, tb = 1, 512
    else:
        bb, tb = min(b, 512 // t), t
    tq = min(t, 128)
    tk = 512 if seq_len % 512 == 0 else 128
    tg = min(t, 256)
    return dict(bb=bb, tb=tb, tq=tq, tk=tk, tg=tg, tf=512)


def kernel(x_prompt, x_sample, c_prompt, c_sample, cache_k, cache_v, cache_idx_k, state_gla, state_ffn_conv,
           w_ada, b_ada, g_mix, g_ffn, w_in, g_q, g_k, w_gate2, b_gate2, g_gla, w_out, w_up, w_conv, b_conv,
           w_down):
    bp, s, d = x_prompt.shape
    bs, ts, _ = x_sample.shape
    depth = w_ada.shape[0]
    past = cache_k.shape[2]
    dff = w_down.shape[1]
    dt = x_prompt.dtype

    y_p, y_s = x_prompt, x_sample
    outs = [[] for _ in range(10)]
    for l in range(depth):
        n_c = bp + bs
        rows = -(-n_c // SUBLANES) * SUBLANES
        c_all = jnp.concatenate([c_prompt, c_sample, jnp.zeros((rows - n_c, d), dt)], axis=0)
        mod = _ada(c_all, w_ada[l], b_ada[l])
        mod_p = mod[:bp].reshape(bp, 1, 6 * d)
        mod_s = mod[bp:n_c].reshape(bs, 1, 6 * d)
        pw = _prep_weights(w_in[l], w_gate2[l], w_out[l], w_up[l], w_down[l])
        lw = (g_mix[l], g_ffn[l], g_q[l], g_k[l], b_gate2[l], g_gla[l], w_conv[l], b_conv[l])

        gla0 = jnp.zeros((bp, GLA_HEADS, GLA_DK, GLA_DV), dt)
        conv0 = jnp.zeros((bp, CONV_W - 1, 2 * dff), dt)
        y_p, k1, v1, ik1, s1, c1 = _layer(y_p, mod_p, 0, None, None, None, gla0, conv0, lw, pw,
                                          **_tiles(bp, s, s))
        y_s, k2, v2, ik2, s2, c2 = _layer(y_s, mod_s, past, cache_k[l], cache_v[l], cache_idx_k[l],
                                          state_gla[l], state_ffn_conv[l], lw, pw,
                                          **_tiles(bs, ts, past + ts))
        for lst, val in zip(outs, (k1, v1, ik1, s1, c1, k2, v2, ik2, s2, c2)):
            lst.append(val)

    return (y_p, y_s) + tuple(jnp.stack(o) for o in outs)
```

```python
import functools
import math

import jax
import jax.numpy as jnp
import numpy as np
from jax import lax
from jax.experimental import pallas as pl
from jax.experimental.pallas import tpu as pltpu

F32 = jnp.float32
BF16 = jnp.bfloat16

CHUNK = 64
N_ATTN_HEADS = 8
ATTN_HEAD_DIM = 128
ROPE_DIM = 32
ROPE_THETA = 500000.0
N_IDX_HEADS = 16
IDX_DIM = 64
IDX_ROPE_DIM = 16
TOPK_MAX = 256
GLA_HEADS = 4
GLA_DK = 128
GLA_DV = 256
GLA_GATE_RANK = 16
GLA_GATE_TEMP = 16.0
CONV_W = 3
EPS = 1e-6

LANES = 128
SUBLANES = 8
VMEM_LIMIT_BYTES = 56 * 1024 * 1024

SCORE_LOOKAHEAD = 4
SCORE_SLOTS = 8

MASK_NEG = -1e30
F32_LOWEST = -3.4028235e38


def _cparams(sem):
    return pltpu.CompilerParams(dimension_semantics=sem, vmem_limit_bytes=VMEM_LIMIT_BYTES)


def _sigmoid(x):
    return 1.0 / (1.0 + jnp.exp(-x))


def _silu(x):
    return x * _sigmoid(x)


def _ada_kernel(c_ref, w_ref, b_ref, o_ref):
    s = _silu(c_ref[...]).astype(BF16)
    o_ref[...] = jnp.dot(s, w_ref[...].astype(BF16), preferred_element_type=F32) + b_ref[...]


def _ada(c_all, w_ada, b_ada):
    rows, d = c_all.shape
    n = w_ada.shape[1]
    tn = 1536 if n % 1536 == 0 else n
    return pl.pallas_call(
        _ada_kernel,
        out_shape=jax.ShapeDtypeStruct((rows, n), F32),
        grid=(n // tn,),
        in_specs=[
            pl.BlockSpec((rows, d), lambda j: (0, 0)),
            pl.BlockSpec((d, tn), lambda j: (0, j)),
            pl.BlockSpec((1, tn), lambda j: (0, j)),
        ],
        out_specs=pl.BlockSpec((rows, tn), lambda j: (0, j)),
        compiler_params=_cparams(("arbitrary",)),
        name="ada_mod",
    )(c_all, w_ada, b_ada.reshape(1, n))


def _rope_table(pos, half, period, active_lanes):
    inv = ROPE_THETA ** (-jnp.arange(half, dtype=F32) / half)
    ang = pos.astype(F32)[:, None] * inv[None, :]
    cos, sin = jnp.cos(ang), jnp.sin(ang)
    lane = np.arange(LANES)
    within = lane % period
    idx = jnp.asarray(within % half)
    first = jnp.asarray((within < half) & (lane < active_lanes))
    second = jnp.asarray((within >= half) & (within < 2 * half) & (lane < active_lanes))
    cos_l, sin_l = cos[:, idx], sin[:, idx]
    c = jnp.where(first | second, cos_l, 1.0)
    s1 = jnp.where(first, -sin_l, 0.0)
    s2 = jnp.where(second, sin_l, 0.0)
    return jnp.stack([c, s1, s2]).astype(F32)


def _rope_apply(v, tab, half):
    return (v * tab[0] + pltpu.roll(v, LANES - half, 1) * tab[1] + pltpu.roll(v, half, 1) * tab[2])


def _tile_rows(tab_ref, bb):
    t = tab_ref[...]
    if bb == 1:
        return t
    tb = t.shape[1]
    return jnp.broadcast_to(t[:, None], (3, bb, tb, LANES)).reshape(3, bb * tb, LANES)


def _inproj_kernel(x_ref, mod_ref, gmix_ref, wbig_ref, wsmall_ref, gq_ref, gk_ref,
                   ropea_ref, ropei_ref, ropes_ref, wg2_ref, bg2_ref,
                   q_out, k_out, v_out, iq_out, small_out, lg_out, gq_out, gk_out, gv_out, gr_out,
                   h_scr):
    j = pl.program_id(2)
    bb, tb, d = x_ref.shape
    rows = bb * tb

    @pl.when(j == 0)
    def _():
        x = x_ref[...]
        ms = jnp.mean(x * x, axis=-1, keepdims=True)
        y = x * lax.rsqrt(ms + EPS) * gmix_ref[...]
        mod = mod_ref[...]
        h = y * (1.0 + mod[:, :, d:2 * d]) + mod[:, :, 0:d]
        hb = h.astype(BF16).reshape(rows, d)
        h_scr[...] = hb
        small = jnp.dot(hb, wsmall_ref[...], preferred_element_type=F32)
        small = _rope_apply(small, _tile_rows(ropes_ref, bb), IDX_ROPE_DIM // 2)
        small_out[...] = small.reshape(bb, tb, LANES)
        pre = jnp.dot(small.astype(BF16), wg2_ref[...], preferred_element_type=F32) + bg2_ref[...]
        lsig = jnp.minimum(pre, 0.0) - jnp.log(1.0 + jnp.exp(-jnp.abs(pre)))
        lg_out[...] = (lsig / GLA_GATE_TEMP).reshape(bb, tb, lg_out.shape[-1])

    y = jnp.dot(h_scr[...], wbig_ref[...], preferred_element_type=F32)

    def headnorm_rope(g_ref, scale):
        tab = _tile_rows(ropea_ref, bb)
        outs = []
        for hh in range(N_ATTN_HEADS):
            yh = y[:, hh * LANES:(hh + 1) * LANES]
            n = yh * lax.rsqrt(jnp.mean(yh * yh, axis=-1, keepdims=True) + EPS) * g_ref[...]
            r = _rope_apply(n, tab, ROPE_DIM // 2)
            outs.append(r * scale if scale != 1.0 else r)
        return outs

    @pl.when(j == 0)
    def _():
        for hh, r in enumerate(headnorm_rope(gq_ref, ATTN_HEAD_DIM ** -0.5 * math.log2(math.e))):
            q_out[:, :, hh * LANES:(hh + 1) * LANES] = r.astype(BF16).reshape(bb, tb, LANES)

    @pl.when(j == 1)
    def _():
        for hh, r in enumerate(headnorm_rope(gk_ref, 1.0)):
            k_out[:, :, hh * LANES:(hh + 1) * LANES] = r.reshape(bb, tb, LANES)

    @pl.when(j == 2)
    def _():
        v_out[...] = y.reshape(bb, tb, y.shape[-1])

    @pl.when(j == 3)
    def _():
        tab = _tile_rows(ropei_ref, bb)
        for c in range(y.shape[-1] // LANES):
            r = _rope_apply(y[:, c * LANES:(c + 1) * LANES], tab, IDX_ROPE_DIM // 2)
            iq_out[:, :, c * LANES:(c + 1) * LANES] = r.astype(BF16).reshape(bb, tb, LANES)

    @pl.when(j == 4)
    def _():
        half = y.shape[-1] // 2
        gq_out[...] = (y[:, :half] * GLA_DK ** -0.5).reshape(bb, tb, half)
        gk_out[...] = y[:, half:].reshape(bb, tb, half)

    @pl.when(j == 5)
    def _():
        gv_out[...] = y.astype(BF16).reshape(bb, tb, y.shape[-1])

    @pl.when(j == 6)
    def _():
        gr_out[...] = y.reshape(bb, tb, y.shape[-1])


def _inproj(x, mod, g_mix, wbig, wsmall, g_q, g_k, ropea, ropei, ropes, wg2p, bg2, bb, tb):
    b, t, d = x.shape
    gw = 1024
    ngroups = wbig.shape[1] // gw
    glak = GLA_HEADS * GLA_DK
    rows = bb * tb
    tok = lambda w, dt: jax.ShapeDtypeStruct((b, t, w), dt)
    tokspec = lambda w: pl.BlockSpec((bb, tb, w), lambda i, m, j: (i, m, 0))
    const = lambda shp: pl.BlockSpec(shp, lambda i, m, j: tuple(0 for _ in shp))
    tabspec = pl.BlockSpec((3, tb, LANES), lambda i, m, j: (0, m, 0))
    out_shapes = (tok(1024, BF16), tok(1024, F32), tok(1024, F32), tok(1024, BF16), tok(LANES, F32),
                  tok(glak, F32), tok(glak, F32), tok(glak, F32), tok(1024, BF16), tok(1024, F32))
    out_specs = tuple(tokspec(s.shape[-1]) for s in out_shapes)
    return pl.pallas_call(
        _inproj_kernel,
        out_shape=out_shapes,
        grid=(b // bb, t // tb, ngroups),
        in_specs=[
            tokspec(d),
            pl.BlockSpec((bb, 1, mod.shape[-1]), lambda i, m, j: (i, 0, 0)),
            const((1, 1, d)),
            pl.BlockSpec((d, gw), lambda i, m, j: (0, j)),
            const((d, LANES)),
            const((1, LANES)), const((1, LANES)),
            tabspec, tabspec, tabspec,
            const((LANES, glak)), const((1, glak)),
        ],
        out_specs=out_specs,
        scratch_shapes=[pltpu.VMEM((rows, d), BF16)],
        compiler_params=_cparams(("arbitrary", "arbitrary", "arbitrary")),
        name="norm_inproj",
    )(x, mod, g_mix.reshape(1, 1, d), wbig, wsmall, g_q.reshape(1, LANES), g_k.reshape(1, LANES),
      ropea, ropei, ropes, wg2p, bg2.reshape(1, glak))


def _dsa_kernel(iqt_ref, iwt_ref, ik_ref, qt_ref, k_hbm, vt_hbm, o_ref,
                u_scr, kbuf, vbuf, sem, m_scr, l_scr, acc_scr, s_scr, bias_scr,
                *, past, seq_len, topk, tq, tk):
    b = pl.program_id(0)
    i = pl.program_id(1)
    nkt = u_scr.shape[0]

    last_chunk = (past + (i + 1) * tq - 1) // CHUNK
    n_adm = jnp.minimum(seq_len, (last_chunk + 1) * CHUNK)
    nk = jnp.minimum((n_adm + tk - 1) // tk, nkt)

    def k_copy(j):
        off = pl.multiple_of(j * tk, tk)
        return pltpu.make_async_copy(k_hbm.at[b, pl.ds(off, tk)], kbuf.at[j % 2], sem.at[0, j % 2])

    def v_copy(j):
        off = pl.multiple_of(j * tk, tk)
        return pltpu.make_async_copy(vt_hbm.at[b, :, pl.ds(off, tk)], vbuf.at[j % 2], sem.at[1, j % 2])

    k_copy(0).start()
    v_copy(0).start()

    q_chunk = (past + i * tq + lax.broadcasted_iota(jnp.int32, (tk, tq), 1)) // CHUNK
    key_iota = lax.broadcasted_iota(jnp.int32, (tk, tq), 0)
    idx_scale = (IDX_DIM * N_IDX_HEADS) ** -0.5

    def score_tile(j, carry):
        ik_t = ik_ref[0, pl.ds(pl.multiple_of(j * tk, tk), tk), :]
        acc = jnp.zeros((tk, tq), F32)
        for hh in range(N_IDX_HEADS):
            logits = jnp.dot(ik_t, iqt_ref[0, hh], preferred_element_type=F32)
            acc = acc + iwt_ref[0, hh:hh + 1, :] * jnp.maximum(logits, 0.0)
        kpos = j * tk + key_iota
        adm = (kpos // CHUNK <= q_chunk) & (kpos < seq_len)
        u_scr[j] = jnp.where(adm, acc * idx_scale, -jnp.inf)
        return carry

    lax.fori_loop(0, nk, score_tile, 0)

    int_min = jnp.int32(-2 ** 31)

    def thr_of(t_biased):
        mono = t_biased ^ int_min
        bits = jnp.where(mono >= 0, mono, mono ^ jnp.int32(0x7FFFFFFF))
        return lax.bitcast_convert_type(bits, F32)

    def search_step(it, t_biased):
        cand = t_biased | lax.shift_left(jnp.int32(1), jnp.int32(31) - it)
        thr = jnp.broadcast_to(thr_of(cand), (SUBLANES, tq))

        def count_tile(j, cnt):
            hit = jnp.where(u_scr[j].reshape(tk // SUBLANES, SUBLANES, tq) >= thr[None], 1.0, 0.0)
            return cnt + jnp.sum(hit, axis=0)

        cnt = lax.fori_loop(0, nk, count_tile, jnp.zeros((SUBLANES, tq), F32))
        total = jnp.sum(cnt, axis=0, keepdims=True)
        return jnp.where(total >= float(topk), cand, t_biased)

    t_fin = lax.fori_loop(0, 32, search_step, jnp.zeros((1, tq), jnp.int32))
    few = (t_fin >= 0) & (t_fin < jnp.int32(0x00800000))
    thr = jnp.where(few, F32_LOWEST, jnp.maximum(thr_of(t_fin), F32_LOWEST))

    m_scr[...] = jnp.full(m_scr.shape, MASK_NEG, F32)
    l_scr[...] = jnp.zeros(l_scr.shape, F32)
    acc_scr[...] = jnp.zeros(acc_scr.shape, F32)

    def select_bias(j):
        bias_scr[...] = jnp.where(u_scr[j] >= thr, 0.0, MASK_NEG)

    n_slots = s_scr.shape[0]

    def scores_head(j, hh):
        hs = slice(hh * LANES, (hh + 1) * LANES)
        s_scr[hh % n_slots] = (jnp.dot(kbuf[j % 2, :, hs], qt_ref[0, hh], preferred_element_type=F32)
                               + bias_scr[...])

    def attend_head(j, hh):
        hs = slice(hh * LANES, (hh + 1) * LANES)
        s = s_scr[hh % n_slots]
        m_prev = m_scr[hh:hh + 1, :]
        m_new = jnp.maximum(m_prev, jnp.max(s, axis=0, keepdims=True))
        alpha = jnp.exp2(m_prev - m_new)
        p = jnp.exp2(s - m_new)
        l_scr[hh:hh + 1, :] = alpha * l_scr[hh:hh + 1, :] + jnp.sum(p, axis=0, keepdims=True)
        acc_scr[hs, :] = alpha * acc_scr[hs, :] + jnp.dot(vbuf[j % 2, hs, :], p.astype(BF16),
                                                          preferred_element_type=F32)
        m_scr[hh:hh + 1, :] = m_new

    def tile_steps(j, has_next):
        v_copy(j).wait()
        for hh in range(N_ATTN_HEADS):
            ahead = hh + SCORE_LOOKAHEAD
            if ahead < N_ATTN_HEADS:
                scores_head(j, ahead)
            elif has_next:
                if ahead == N_ATTN_HEADS:
                    @pl.when(j + 2 < nk)
                    def _():
                        k_copy(j + 2).start()

                    k_copy(j + 1).wait()
                    select_bias(j + 1)
                scores_head(j + 1, ahead - N_ATTN_HEADS)
            attend_head(j, hh)

    @pl.when(nk > 1)
    def _():
        k_copy(1).start()

    k_copy(0).wait()
    select_bias(0)
    for hh in range(SCORE_LOOKAHEAD):
        scores_head(0, hh)

    def attend_tile(j, carry):
        v_copy(j + 1).start()
        tile_steps(j, True)
        return carry

    lax.fori_loop(0, nk - 1, attend_tile, 0)
    tile_steps(nk - 1, False)

    for hh in range(N_ATTN_HEADS):
        hs = slice(hh * LANES, (hh + 1) * LANES)
        o_t = acc_scr[hs, :] / l_scr[hh:hh + 1, :]
        o_ref[0, :, hs] = o_t.T.astype(o_ref.dtype)


def _dsa(iq_t, iw_t, ik_bf, q_t, k_bf, v_t, *, past, seq_len, topk, tq, tk):
    b, _, _, t = iq_t.shape
    lp = ik_bf.shape[1]
    nkt = lp // tk
    hd = N_ATTN_HEADS * ATTN_HEAD_DIM
    kern = functools.partial(_dsa_kernel, past=past, seq_len=seq_len, topk=topk, tq=tq, tk=tk)
    return pl.pallas_call(
        kern,
        out_shape=jax.ShapeDtypeStruct((b, t, hd), BF16),
        grid=(b, t // tq),
        in_specs=[
            pl.BlockSpec((1, N_IDX_HEADS, IDX_DIM, tq), lambda bi, i: (bi, 0, 0, i)),
            pl.BlockSpec((1, N_IDX_HEADS, tq), lambda bi, i: (bi, 0, i)),
            pl.BlockSpec((1, lp, IDX_DIM), lambda bi, i: (bi, 0, 0)),
            pl.BlockSpec((1, N_ATTN_HEADS, ATTN_HEAD_DIM, tq), lambda bi, i: (bi, 0, 0, i)),
            pl.BlockSpec(memory_space=pl.ANY),
            pl.BlockSpec(memory_space=pl.ANY),
        ],
        out_specs=pl.BlockSpec((1, tq, hd), lambda bi, i: (bi, i, 0)),
        scratch_shapes=[
            pltpu.VMEM((nkt, tk, tq), F32),
            pltpu.VMEM((2, tk, hd), BF16),
            pltpu.VMEM((2, hd, tk), BF16),
            pltpu.SemaphoreType.DMA((2, 2)),
            pltpu.VMEM((N_ATTN_HEADS, tq), F32),
            pltpu.VMEM((N_ATTN_HEADS, tq), F32),
            pltpu.VMEM((hd, tq), F32),
            pltpu.VMEM((SCORE_SLOTS, tk, tq), F32),
            pltpu.VMEM((tk, tq), F32),
        ],
        compiler_params=_cparams(("arbitrary", "arbitrary")),
        name="dsa_attend",
    )(iq_t, iw_t, ik_bf, q_t, k_bf, v_t)


def _gla_kernel(q_ref, k_ref, v_ref, lg_ref, gr_ref, g_ref, s0_ref, o_ref, sT_out, st_scr, *, n_chunks):
    tstep = pl.program_id(2)

    @pl.when(tstep == 0)
    def _():
        st_scr[...] = s0_ref[0, 0]

    row = lax.broadcasted_iota(jnp.int32, (CHUNK, GLA_DK), 0)
    tri = (lax.broadcasted_iota(jnp.int32, (CHUNK, CHUNK), 0)
           >= lax.broadcasted_iota(jnp.int32, (CHUNK, CHUNK), 1))

    for c in range(n_chunks):
        rs = slice(c * CHUNK, (c + 1) * CHUNK)
        bcum = lg_ref[0, rs, :]
        shift = 1
        while shift < CHUNK:
            bcum = bcum + jnp.where(row >= shift, pltpu.roll(bcum, shift, 0), 0.0)
            shift *= 2
        bl = bcum[CHUNK - 1:CHUNK, :]
        q = q_ref[0, rs, :]
        k = k_ref[0, rs, :]
        v = v_ref[0, rs, :]
        qe = (q * jnp.exp(bcum)).astype(BF16)
        ke = (k * jnp.exp(-bcum)).astype(BF16)
        kd = (k * jnp.exp(bl - bcum)).astype(BF16)
        st = st_scr[...]
        inter = lax.dot_general(qe, st.astype(BF16), (((1,), (1,)), ((), ())), preferred_element_type=F32)
        a = lax.dot_general(qe, ke, (((1,), (1,)), ((), ())), preferred_element_type=F32)
        a = jnp.where(tri, a, 0.0)
        o = inter + jnp.dot(a.astype(BF16), v, preferred_element_type=F32)
        st_scr[...] = st * jnp.exp(bl) + lax.dot_general(v, kd, (((0,), (0,)), ((), ())),
                                                          preferred_element_type=F32)
        n = o * lax.rsqrt(jnp.mean(o * o, axis=-1, keepdims=True) + EPS) * g_ref[...]
        o_ref[0, rs, :] = (n * _silu(gr_ref[0, rs, :])).astype(o_ref.dtype)

    @pl.when(tstep == pl.num_programs(2) - 1)
    def _():
        sT_out[0, 0] = st_scr[...]


def _gla(gq, gk, gv, lg, gr, g_gla, s0t, tg):
    b, t, _ = gq.shape
    kern = functools.partial(_gla_kernel, n_chunks=tg // CHUNK)
    tok = lambda w: pl.BlockSpec((1, tg, w), lambda bi, h, m: (bi, m, h))
    st_spec = pl.BlockSpec((1, 1, GLA_DV, GLA_DK), lambda bi, h, m: (bi, h, 0, 0))
    return pl.pallas_call(
        kern,
        out_shape=(jax.ShapeDtypeStruct((b, t, GLA_HEADS * GLA_DV), BF16),
                   jax.ShapeDtypeStruct((b, GLA_HEADS, GLA_DV, GLA_DK), F32)),
        grid=(b, GLA_HEADS, t // tg),
        in_specs=[tok(GLA_DK), tok(GLA_DK), tok(GLA_DV), tok(GLA_DK), tok(GLA_DV),
                  pl.BlockSpec((1, GLA_DV), lambda bi, h, m: (0, 0)), st_spec],
        out_specs=(tok(GLA_DV), st_spec),
        scratch_shapes=[pltpu.VMEM((GLA_DV, GLA_DK), F32)],
        compiler_params=_cparams(("arbitrary", "arbitrary", "arbitrary")),
        name="gla",
    )(gq, gk, gv, lg, gr, g_gla.reshape(1, GLA_DV), s0t)


def _outproj_kernel(oa_ref, og_ref, wa_ref, wg_ref, x_ref, mod_ref, gffn_ref, x1_out, h2_out):
    bb, tb, d = x_ref.shape
    rows = bb * tb
    oa = oa_ref[...].reshape(rows, oa_ref.shape[-1])
    og = og_ref[...].reshape(rows, og_ref.shape[-1])
    mix = (jnp.dot(oa, wa_ref[...], preferred_element_type=F32)
           + jnp.dot(og, wg_ref[...], preferred_element_type=F32)).reshape(bb, tb, d)
    mod = mod_ref[...]
    x1 = x_ref[...] + mod[:, :, 2 * d:3 * d] * mix
    x1_out[...] = x1
    y = x1 * lax.rsqrt(jnp.mean(x1 * x1, axis=-1, keepdims=True) + EPS) * gffn_ref[...]
    h2_out[...] = (y * (1.0 + mod[:, :, 4 * d:5 * d]) + mod[:, :, 3 * d:4 * d]).astype(BF16)


def _outproj(o_a, o_g, wout_a, wout_g, x, mod, g_ffn, bb, tb):
    b, t, d = x.shape
    tokspec = lambda w: pl.BlockSpec((bb, tb, w), lambda i, m: (i, m, 0))
    const = lambda shp: pl.BlockSpec(shp, lambda i, m: tuple(0 for _ in shp))
    return pl.pallas_call(
        _outproj_kernel,
        out_shape=(jax.ShapeDtypeStruct((b, t, d), F32), jax.ShapeDtypeStruct((b, t, d), BF16)),
        grid=(b // bb, t // tb),
        in_specs=[tokspec(o_a.shape[-1]), tokspec(o_g.shape[-1]), const(wout_a.shape), const(wout_g.shape),
                  tokspec(d), pl.BlockSpec((bb, 1, mod.shape[-1]), lambda i, m: (i, 0, 0)), const((1, 1, d))],
        out_specs=(tokspec(d), tokspec(d)),
        compiler_params=_cparams(("arbitrary", "arbitrary")),
        name="outproj_norm",
    )(o_a, o_g, wout_a, wout_g, x, mod, g_ffn.reshape(1, 1, d))


def _ffn_kernel(h_ref, halo_ref, wa_ref, wb_ref, wca_ref, wcb_ref, bca_ref, bcb_ref, sa_ref, sb_ref,
                wd_ref, x1_ref, mod_ref, y_out, nba_out, nbb_out, acc_scr):
    m = pl.program_id(1)
    f = pl.program_id(2)
    bb, tb, d = h_ref.shape
    rows = bb * tb
    tf = wa_ref.shape[1]
    hb = h_ref[...].reshape(rows, d)
    hh = halo_ref[...].reshape(bb * SUBLANES, d)
    tpos = lax.broadcasted_iota(jnp.int32, (rows, tf), 0) % tb

    def conv_branch(w_ref, wc_ref, bc_ref, s_ref, nb_out):
        u = jnp.dot(hb, w_ref[...], preferred_element_type=F32)
        pu = jnp.dot(hh, w_ref[...], preferred_element_type=F32).reshape(bb, SUBLANES, tf)
        prev = jnp.where(m == 0, s_ref[...], pu[:, SUBLANES - (CONV_W - 1):, :])
        bcast = lambda r: jnp.broadcast_to(prev[:, r:r + 1, :], (bb, tb, tf)).reshape(rows, tf)
        p0, p1 = bcast(0), bcast(1)
        u1 = jnp.where(tpos == 0, p1, pltpu.roll(u, 1, 0))
        u2 = jnp.where(tpos == 0, p0, jnp.where(tpos == 1, p1, pltpu.roll(u, 2, 0)))
        wc = wc_ref[...]
        nb_out[:, 0] = u.reshape(bb, tb, tf)[:, tb - (CONV_W - 1):, :]
        return bc_ref[...] + wc[0:1] * u2 + wc[1:2] * u1 + wc[2:3] * u

    ua = conv_branch(wa_ref, wca_ref, bca_ref, sa_ref, nba_out)
    ub = conv_branch(wb_ref, wcb_ref, bcb_ref, sb_ref, nbb_out)
    act = (_silu(ua) * ub).astype(BF16)
    part = jnp.dot(act, wd_ref[...], preferred_element_type=F32)

    @pl.when(f == 0)
    def _():
        acc_scr[...] = part

    @pl.when(f > 0)
    def _():
        acc_scr[...] += part

    @pl.when(f == pl.num_programs(2) - 1)
    def _():
        mod = mod_ref[...]
        y_out[...] = x1_ref[...] + mod[:, :, 5 * d:6 * d] * acc_scr[...].reshape(bb, tb, d)


def _ffn(h2, w_up, w_conv, b_conv, conv_state, w_down, x1, mod, bb, tb, tf):
    b, t, d = x1.shape
    dff = w_down.shape[0]
    nf = dff // tf
    hb = tb // SUBLANES
    tokspec = lambda w: pl.BlockSpec((bb, tb, w), lambda i, m, f: (i, m, 0))
    a_col = lambda r: pl.BlockSpec((r, tf), lambda i, m, f: (0, f))
    b_col = lambda r: pl.BlockSpec((r, tf), lambda i, m, f: (0, nf + f))
    st_a = pl.BlockSpec((bb, CONV_W - 1, tf), lambda i, m, f: (i, 0, f))
    st_b = pl.BlockSpec((bb, CONV_W - 1, tf), lambda i, m, f: (i, 0, nf + f))
    nb_spec = pl.BlockSpec((bb, 1, CONV_W - 1, tf), lambda i, m, f: (i, m, 0, f))
    bc2 = b_conv.reshape(1, 2 * dff)
    return pl.pallas_call(
        _ffn_kernel,
        out_shape=(jax.ShapeDtypeStruct((b, t, d), F32),
                   jax.ShapeDtypeStruct((b, t // tb, CONV_W - 1, dff), F32),
                   jax.ShapeDtypeStruct((b, t // tb, CONV_W - 1, dff), F32)),
        grid=(b // bb, t // tb, nf),
        in_specs=[
            tokspec(d),
            pl.BlockSpec((bb, SUBLANES, d), lambda i, m, f: (i, jnp.maximum(m * hb - 1, 0), 0)),
            a_col(d), b_col(d), a_col(CONV_W), b_col(CONV_W), a_col(1), b_col(1), st_a, st_b,
            pl.BlockSpec((tf, d), lambda i, m, f: (f, 0)),
            tokspec(d),
            pl.BlockSpec((bb, 1, mod.shape[-1]), lambda i, m, f: (i, 0, 0)),
        ],
        out_specs=(tokspec(d), nb_spec, nb_spec),
        scratch_shapes=[pltpu.VMEM((bb * tb, d), F32)],
        compiler_params=_cparams(("arbitrary", "arbitrary", "arbitrary")),
        name="conv_ffn",
    )(h2, h2, w_up, w_up, w_conv, w_conv, bc2, bc2, conv_state, conv_state, w_down, x1, mod)


def _prep_weights(w_in, w_gate2, w_out, w_up, w_down):
    hd = N_ATTN_HEADS * ATTN_HEAD_DIM
    sizes = (hd, hd, hd, N_IDX_HEADS * IDX_DIM, IDX_DIM, N_IDX_HEADS,
             GLA_HEADS * GLA_DK, GLA_HEADS * GLA_DK, GLA_HEADS * GLA_DV, GLA_HEADS * GLA_DV, GLA_GATE_RANK)
    offs = np.concatenate([[0], np.cumsum(sizes)])
    col = lambda n: w_in[:, int(offs[n]):int(offs[n + 1])]
    wbig = jnp.concatenate([col(0), col(1), col(2), col(3), col(6), col(7), col(8), col(9)], axis=1).astype(BF16)
    pad = LANES - (IDX_DIM + N_IDX_HEADS + GLA_GATE_RANK)
    wsmall = jnp.concatenate([col(4), col(5), col(10), jnp.zeros((w_in.shape[0], pad), w_in.dtype)],
                             axis=1).astype(BF16)
    lo = IDX_DIM + N_IDX_HEADS
    wg2p = jnp.zeros((LANES, w_gate2.shape[1]), F32).at[lo:lo + GLA_GATE_RANK].set(w_gate2).astype(BF16)
    return dict(wbig=wbig, wsmall=wsmall, wg2p=wg2p,
                wout_a=w_out[:hd].astype(BF16), wout_g=w_out[hd:].astype(BF16),
                w_up=w_up.astype(BF16), w_down=w_down.astype(BF16))


def _layer(x, mod, past, past_k, past_v, past_ik, gla_s0, conv_state, lw, pw, *, bb, tb, tq, tk, tg, tf):
    (g_mix, g_ffn, g_q, g_k, b_gate2, g_gla, w_conv, b_conv) = lw
    b, t, d = x.shape
    hd = N_ATTN_HEADS * ATTN_HEAD_DIM
    pos = past + jnp.arange(t, dtype=jnp.int32)
    ropea = _rope_table(pos, ROPE_DIM // 2, LANES, LANES)
    ropei = _rope_table(pos, IDX_ROPE_DIM // 2, IDX_DIM, LANES)
    ropes = _rope_table(pos, IDX_ROPE_DIM // 2, LANES, IDX_DIM)

    (q_bf, k, v, iq, small, lg, gq, gk, gv, gr) = _inproj(
        x, mod, g_mix, pw["wbig"], pw["wsmall"], g_q, g_k, ropea, ropei, ropes, pw["wg2p"], b_gate2, bb, tb)
    ik = small[:, :, :IDX_DIM]
    iw = small[:, :, IDX_DIM:IDX_DIM + N_IDX_HEADS]

    k_bf, v_bf, ik_bf = k.astype(BF16), v.astype(BF16), ik.astype(BF16)
    if past_k is not None:
        k_bf = jnp.concatenate([past_k.reshape(b, past, hd).astype(BF16), k_bf], axis=1)
        v_bf = jnp.concatenate([past_v.reshape(b, past, hd).astype(BF16), v_bf], axis=1)
        ik_bf = jnp.concatenate([past_ik.astype(BF16), ik_bf], axis=1)
    seq_len = past + t
    lp = -(-seq_len // tk) * tk
    if lp != seq_len:
        padw = ((0, 0), (0, lp - seq_len), (0, 0))
        k_bf, v_bf, ik_bf = jnp.pad(k_bf, padw), jnp.pad(v_bf, padw), jnp.pad(ik_bf, padw)
    tpad = -(-t // tq) * tq
    qpad = lambda a: jnp.pad(a, ((0, 0), (0, tpad - t), (0, 0))) if tpad != t else a
    iq_t = qpad(iq).reshape(b, tpad, N_IDX_HEADS, IDX_DIM).transpose(0, 2, 3, 1)
    iw_t = qpad(iw).transpose(0, 2, 1)
    q_t = qpad(q_bf).reshape(b, tpad, N_ATTN_HEADS, ATTN_HEAD_DIM).transpose(0, 2, 3, 1)
    v_t = v_bf.transpose(0, 2, 1)
    topk = min(TOPK_MAX, seq_len // 4)
    o_a = _dsa(iq_t, iw_t, ik_bf, q_t, k_bf, v_t, past=past, seq_len=seq_len, topk=topk, tq=tq, tk=tk)
    o_a = o_a[:, :t]

    s0t = jnp.swapaxes(gla_s0, -1, -2)
    o_g, s_t = _gla(gq, gk, gv, lg, gr, g_gla, s0t, tg)
    s_new = jnp.swapaxes(s_t, -1, -2)

    x1, h2 = _outproj(o_a, o_g, pw["wout_a"], pw["wout_g"], x, mod, g_ffn, bb, tb)
    y, nb_a, nb_b = _ffn(h2, pw["w_up"], w_conv, b_conv, conv_state, pw["w_down"], x1, mod, bb, tb, tf)
    new_buf = jnp.concatenate([nb_a[:, -1], nb_b[:, -1]], axis=-1)
    return (y, k.reshape(b, t, N_ATTN_HEADS, ATTN_HEAD_DIM), v.reshape(b, t, N_ATTN_HEADS, ATTN_HEAD_DIM),
            ik, s_new, new_buf)


def _tiles(b, t, seq_len):
    if t >= 512:
        bb, tb = 1, 512
    else:
        bb, tb = min(b, 512 // t), t
    tq = 256 if t % 256 == 0 else LANES
    tk = 256 if seq_len % 256 == 0 else LANES
    tg = min(t, 256)
    return dict(bb=bb, tb=tb, tq=tq, tk=tk, tg=tg, tf=512)


def kernel(x_prompt, x_sample, c_prompt, c_sample, cache_k, cache_v, cache_idx_k, state_gla, state_ffn_conv,
           w_ada, b_ada, g_mix, g_ffn, w_in, g_q, g_k, w_gate2, b_gate2, g_gla, w_out, w_up, w_conv, b_conv,
           w_down):
    bp, s, d = x_prompt.shape
    bs, ts, _ = x_sample.shape
    depth = w_ada.shape[0]
    past = cache_k.shape[2]
    dff = w_down.shape[1]
    dt = x_prompt.dtype

    y_p, y_s = x_prompt, x_sample
    outs = [[] for _ in range(10)]
    for l in range(depth):
        n_c = bp + bs
        rows = -(-n_c // SUBLANES) * SUBLANES
        c_all = jnp.concatenate([c_prompt, c_sample, jnp.zeros((rows - n_c, d), dt)], axis=0)
        mod = _ada(c_all, w_ada[l], b_ada[l])
        mod_p = mod[:bp].reshape(bp, 1, 6 * d)
        mod_s = mod[bp:n_c].reshape(bs, 1, 6 * d)
        pw = _prep_weights(w_in[l], w_gate2[l], w_out[l], w_up[l], w_down[l])
        lw = (g_mix[l], g_ffn[l], g_q[l], g_k[l], b_gate2[l], g_gla[l], w_conv[l], b_conv[l])

        gla0 = jnp.zeros((bp, GLA_HEADS, GLA_DK, GLA_DV), dt)
        conv0 = jnp.zeros((bp, CONV_W - 1, 2 * dff), dt)
        y_p, k1, v1, ik1, s1, c1 = _layer(y_p, mod_p, 0, None, None, None, gla0, conv0, lw, pw,
                                          **_tiles(bp, s, s))
        y_s, k2, v2, ik2, s2, c2 = _layer(y_s, mod_s, past, cache_k[l], cache_v[l], cache_idx_k[l],
                                          state_gla[l], state_ffn_conv[l], lw, pw,
                                          **_tiles(bs, ts, past + ts))
        for lst, val in zip(outs, (k1, v1, ik1, s1, c1, k2, v2, ik2, s2, c2)):
            lst.append(val)

    return (y_p, y_s) + tuple(jnp.stack(o) for o in outs)
```

```python
import functools
import math

import jax
import jax.numpy as jnp
import numpy as np
from jax import lax
from jax.experimental import pallas as pl
from jax.experimental.pallas import tpu as pltpu

F32 = jnp.float32
BF16 = jnp.bfloat16

CHUNK = 64
N_ATTN_HEADS = 8
ATTN_HEAD_DIM = 128
ROPE_DIM = 32
ROPE_THETA = 500000.0
N_IDX_HEADS = 16
IDX_DIM = 64
IDX_ROPE_DIM = 16
TOPK_MAX = 256
GLA_HEADS = 4
GLA_DK = 128
GLA_DV = 256
GLA_GATE_RANK = 16
GLA_GATE_TEMP = 16.0
CONV_W = 3
EPS = 1e-6

LANES = 128
SUBLANES = 8
VMEM_LIMIT_BYTES = 56 * 1024 * 1024

SCORE_LOOKAHEAD = 4
SCORE_SLOTS = 8

MASK_NEG = -1e30
F32_LOWEST = -3.4028235e38


def _cparams(sem):
    return pltpu.CompilerParams(dimension_semantics=sem, vmem_limit_bytes=VMEM_LIMIT_BYTES)


def _sigmoid(x):
    return 1.0 / (1.0 + jnp.exp(-x))


def _silu(x):
    return x * _sigmoid(x)


def _ada_kernel(c_ref, w_ref, b_ref, o_ref):
    s = _silu(c_ref[...]).astype(BF16)
    o_ref[...] = jnp.dot(s, w_ref[...].astype(BF16), preferred_element_type=F32) + b_ref[...]


def _ada(c_all, w_ada, b_ada):
    rows, d = c_all.shape
    n = w_ada.shape[1]
    tn = 1536 if n % 1536 == 0 else n
    return pl.pallas_call(
        _ada_kernel,
        out_shape=jax.ShapeDtypeStruct((rows, n), F32),
        grid=(n // tn,),
        in_specs=[
            pl.BlockSpec((rows, d), lambda j: (0, 0)),
            pl.BlockSpec((d, tn), lambda j: (0, j)),
            pl.BlockSpec((1, tn), lambda j: (0, j)),
        ],
        out_specs=pl.BlockSpec((rows, tn), lambda j: (0, j)),
        compiler_params=_cparams(("arbitrary",)),
        name="ada_mod",
    )(c_all, w_ada, b_ada.reshape(1, n))


def _rope_table(pos, half, period, active_lanes):
    inv = ROPE_THETA ** (-jnp.arange(half, dtype=F32) / half)
    ang = pos.astype(F32)[:, None] * inv[None, :]
    cos, sin = jnp.cos(ang), jnp.sin(ang)
    lane = np.arange(LANES)
    within = lane % period
    idx = jnp.asarray(within % half)
    first = jnp.asarray((within < half) & (lane < active_lanes))
    second = jnp.asarray((within >= half) & (within < 2 * half) & (lane < active_lanes))
    cos_l, sin_l = cos[:, idx], sin[:, idx]
    c = jnp.where(first | second, cos_l, 1.0)
    s1 = jnp.where(first, -sin_l, 0.0)
    s2 = jnp.where(second, sin_l, 0.0)
    return jnp.stack([c, s1, s2]).astype(F32)


def _rope_apply(v, tab, half):
    return (v * tab[0] + pltpu.roll(v, LANES - half, 1) * tab[1] + pltpu.roll(v, half, 1) * tab[2])


def _tile_rows(tab_ref, bb):
    t = tab_ref[...]
    if bb == 1:
        return t
    tb = t.shape[1]
    return jnp.broadcast_to(t[:, None], (3, bb, tb, LANES)).reshape(3, bb * tb, LANES)


def _inproj_kernel(x_ref, mod_ref, gmix_ref, wbig_ref, wsmall_ref, gq_ref, gk_ref,
                   ropea_ref, ropei_ref, ropes_ref, wg2_ref, bg2_ref,
                   q_out, k_out, v_out, iq_out, small_out, lg_out, gq_out, gk_out, gv_out, gr_out,
                   h_scr):
    j = pl.program_id(2)
    bb, tb, d = x_ref.shape
    rows = bb * tb

    @pl.when(j == 0)
    def _():
        x = x_ref[...]
        ms = jnp.mean(x * x, axis=-1, keepdims=True)
        y = x * lax.rsqrt(ms + EPS) * gmix_ref[...]
        mod = mod_ref[...]
        h = y * (1.0 + mod[:, :, d:2 * d]) + mod[:, :, 0:d]
        hb = h.astype(BF16).reshape(rows, d)
        h_scr[...] = hb
        small = jnp.dot(hb, wsmall_ref[...], preferred_element_type=F32)
        small = _rope_apply(small, _tile_rows(ropes_ref, bb), IDX_ROPE_DIM // 2)
        small_out[...] = small.reshape(bb, tb, LANES)
        pre = jnp.dot(small.astype(BF16), wg2_ref[...], preferred_element_type=F32) + bg2_ref[...]
        lsig = jnp.minimum(pre, 0.0) - jnp.log(1.0 + jnp.exp(-jnp.abs(pre)))
        lg_out[...] = (lsig / GLA_GATE_TEMP).reshape(bb, tb, lg_out.shape[-1])

    y = jnp.dot(h_scr[...], wbig_ref[...], preferred_element_type=F32)

    def headnorm_rope(g_ref, scale):
        tab = _tile_rows(ropea_ref, bb)
        outs = []
        for hh in range(N_ATTN_HEADS):
            yh = y[:, hh * LANES:(hh + 1) * LANES]
            n = yh * lax.rsqrt(jnp.mean(yh * yh, axis=-1, keepdims=True) + EPS) * g_ref[...]
            r = _rope_apply(n, tab, ROPE_DIM // 2)
            outs.append(r * scale if scale != 1.0 else r)
        return outs

    @pl.when(j == 0)
    def _():
        for hh, r in enumerate(headnorm_rope(gq_ref, ATTN_HEAD_DIM ** -0.5 * math.log2(math.e))):
            q_out[:, :, hh * LANES:(hh + 1) * LANES] = r.astype(BF16).reshape(bb, tb, LANES)

    @pl.when(j == 1)
    def _():
        for hh, r in enumerate(headnorm_rope(gk_ref, 1.0)):
            k_out[:, :, hh * LANES:(hh + 1) * LANES] = r.reshape(bb, tb, LANES)

    @pl.when(j == 2)
    def _():
        v_out[...] = y.reshape(bb, tb, y.shape[-1])

    @pl.when(j == 3)
    def _():
        tab = _tile_rows(ropei_ref, bb)
        for c in range(y.shape[-1] // LANES):
            r = _rope_apply(y[:, c * LANES:(c + 1) * LANES], tab, IDX_ROPE_DIM // 2)
            iq_out[:, :, c * LANES:(c + 1) * LANES] = r.astype(BF16).reshape(bb, tb, LANES)

    @pl.when(j == 4)
    def _():
        half = y.shape[-1] // 2
        gq_out[...] = (y[:, :half] * GLA_DK ** -0.5).reshape(bb, tb, half)
        gk_out[...] = y[:, half:].reshape(bb, tb, half)

    @pl.when(j == 5)
    def _():
        gv_out[...] = y.astype(BF16).reshape(bb, tb, y.shape[-1])

    @pl.when(j == 6)
    def _():
        gr_out[...] = y.reshape(bb, tb, y.shape[-1])


def _inproj(x, mod, g_mix, wbig, wsmall, g_q, g_k, ropea, ropei, ropes, wg2p, bg2, bb, tb):
    b, t, d = x.shape
    gw = 1024
    ngroups = wbig.shape[1] // gw
    glak = GLA_HEADS * GLA_DK
    rows = bb * tb
    tok = lambda w, dt: jax.ShapeDtypeStruct((b, t, w), dt)
    tokspec = lambda w: pl.BlockSpec((bb, tb, w), lambda i, m, j: (i, m, 0))
    const = lambda shp: pl.BlockSpec(shp, lambda i, m, j: tuple(0 for _ in shp))
    tabspec = pl.BlockSpec((3, tb, LANES), lambda i, m, j: (0, m, 0))
    out_shapes = (tok(1024, BF16), tok(1024, F32), tok(1024, F32), tok(1024, BF16), tok(LANES, F32),
                  tok(glak, F32), tok(glak, F32), tok(glak, F32), tok(1024, BF16), tok(1024, F32))
    out_specs = tuple(tokspec(s.shape[-1]) for s in out_shapes)
    return pl.pallas_call(
        _inproj_kernel,
        out_shape=out_shapes,
        grid=(b // bb, t // tb, ngroups),
        in_specs=[
            tokspec(d),
            pl.BlockSpec((bb, 1, mod.shape[-1]), lambda i, m, j: (i, 0, 0)),
            const((1, 1, d)),
            pl.BlockSpec((d, gw), lambda i, m, j: (0, j)),
            const((d, LANES)),
            const((1, LANES)), const((1, LANES)),
            tabspec, tabspec, tabspec,
            const((LANES, glak)), const((1, glak)),
        ],
        out_specs=out_specs,
        scratch_shapes=[pltpu.VMEM((rows, d), BF16)],
        compiler_params=_cparams(("arbitrary", "arbitrary", "arbitrary")),
        name="norm_inproj",
    )(x, mod, g_mix.reshape(1, 1, d), wbig, wsmall, g_q.reshape(1, LANES), g_k.reshape(1, LANES),
      ropea, ropei, ropes, wg2p, bg2.reshape(1, glak))


def _dsa_kernel(iqt_ref, iwt_ref, ik_ref, qt_ref, k_hbm, vt_hbm, o_ref,
                u_scr, ub_scr, kbuf, vbuf, sem, m_scr, l_scr, acc_scr, s_scr, bias_scr,
                *, past, seq_len, topk, tq, tk):
    b = pl.program_id(0)
    i = pl.program_id(1)
    nkt = u_scr.shape[0]

    last_chunk = (past + (i + 1) * tq - 1) // CHUNK
    n_adm = jnp.minimum(seq_len, (last_chunk + 1) * CHUNK)
    nk = jnp.minimum((n_adm + tk - 1) // tk, nkt)

    def k_copy(j):
        off = pl.multiple_of(j * tk, tk)
        return pltpu.make_async_copy(k_hbm.at[b, pl.ds(off, tk)], kbuf.at[j % 2], sem.at[0, j % 2])

    def v_copy(j):
        off = pl.multiple_of(j * tk, tk)
        return pltpu.make_async_copy(vt_hbm.at[b, :, pl.ds(off, tk)], vbuf.at[j % 2], sem.at[1, j % 2])

    k_copy(0).start()
    v_copy(0).start()

    q_chunk = (past + i * tq + lax.broadcasted_iota(jnp.int32, (tk, tq), 1)) // CHUNK
    key_iota = lax.broadcasted_iota(jnp.int32, (tk, tq), 0)
    idx_scale = (IDX_DIM * N_IDX_HEADS) ** -0.5

    def score_tile(j, carry):
        ik_t = ik_ref[0, pl.ds(pl.multiple_of(j * tk, tk), tk), :]
        acc = jnp.zeros((tk, tq), F32)
        for hh in range(N_IDX_HEADS):
            logits = jnp.dot(ik_t, iqt_ref[0, hh], preferred_element_type=F32)
            acc = acc + iwt_ref[0, hh:hh + 1, :] * jnp.maximum(logits, 0.0)
        kpos = j * tk + key_iota
        adm = (kpos // CHUNK <= q_chunk) & (kpos < seq_len)
        u = jnp.where(adm, acc * idx_scale, -jnp.inf)
        u_scr[j] = u
        ub_scr[j] = u.astype(ub_scr.dtype)
        return carry

    lax.fori_loop(0, nk, score_tile, 0)

    group = 2 if nkt % 2 == 0 else 1
    n_groups = (nk + group - 1) // group

    @pl.when(n_groups * group > nk)
    def _():
        u_scr[nk] = jnp.full((tk, tq), -jnp.inf, F32)
        ub_scr[nk] = jnp.full((tk, tq), -jnp.inf, ub_scr.dtype)

    def count_ge(scr, thr_row):
        rows = SUBLANES * (4 // jnp.dtype(scr.dtype).itemsize)
        thr_b = jnp.broadcast_to(thr_row.astype(scr.dtype), (rows, tq))
        one, zero = jnp.ones((), scr.dtype), jnp.zeros((), scr.dtype)

        def body(g, cnt):
            for r in range(group):
                x = scr[g * group + r].reshape(tk // rows, rows, tq)
                hit = jnp.where(x >= thr_b[None], one, zero)
                parts = [hit[n] for n in range(tk // rows)]
                while len(parts) > 1:
                    parts = [parts[n] + parts[n + 1] for n in range(0, len(parts), 2)]
                cnt = cnt + parts[0].astype(F32)
            return cnt

        cnt = lax.fori_loop(0, n_groups, body, jnp.zeros((rows, tq), F32))
        return jnp.sum(cnt, axis=0, keepdims=True)

    int_min = jnp.int32(-2 ** 31)
    kf = float(topk)

    def coarse_value(t_biased):
        mono = t_biased ^ int_min
        bits = jnp.where(mono >= 0, mono, mono ^ jnp.int32(0x7FFF0000))
        return lax.bitcast_convert_type(bits, F32)

    def coarse_step(it, t_biased):
        cand = t_biased | lax.shift_left(jnp.int32(1), jnp.int32(31) - it)
        return jnp.where(count_ge(ub_scr, coarse_value(cand)) >= kf, cand, t_biased)

    t16 = lax.fori_loop(0, 16, coarse_step, jnp.zeros((1, tq), jnp.int32))
    few = (t16 >= 0) & (t16 <= jnp.int32(0x007F0000))

    def to_mono(bits):
        return jnp.where(bits >= 0, bits, bits ^ jnp.int32(0x7FFFFFFF))

    c_mono = to_mono(lax.bitcast_convert_type(coarse_value(t16), jnp.int32))
    lo0 = c_mono - jnp.int32(0x8000)
    hi0 = c_mono + jnp.int32(0x10000)

    def fine_step(it, lohi):
        lo, hi = lohi
        mid = lo + lax.shift_right_arithmetic(hi - lo + 1, jnp.int32(1))
        ok = count_ge(u_scr, lax.bitcast_convert_type(to_mono(mid), F32)) >= kf
        return jnp.where(ok, mid, lo), jnp.where(ok, hi, mid - 1)

    lo_fin, _ = lax.fori_loop(0, 17, fine_step, (lo0, hi0))
    thr_fin = lax.bitcast_convert_type(to_mono(lo_fin), F32)
    thr = jnp.where(few, F32_LOWEST, jnp.maximum(thr_fin, F32_LOWEST))

    m_scr[...] = jnp.full(m_scr.shape, MASK_NEG, F32)
    l_scr[...] = jnp.zeros(l_scr.shape, F32)
    acc_scr[...] = jnp.zeros(acc_scr.shape, F32)

    def select_bias(j):
        bias_scr[...] = jnp.where(u_scr[j] >= thr, 0.0, MASK_NEG)

    n_slots = s_scr.shape[0]

    def scores_head(j, hh):
        hs = slice(hh * LANES, (hh + 1) * LANES)
        s_scr[hh % n_slots] = (jnp.dot(kbuf[j % 2, :, hs], qt_ref[0, hh], preferred_element_type=F32)
                               + bias_scr[...])

    def attend_head(j, hh):
        hs = slice(hh * LANES, (hh + 1) * LANES)
        s = s_scr[hh % n_slots]
        m_prev = m_scr[hh:hh + 1, :]
        m_new = jnp.maximum(m_prev, jnp.max(s, axis=0, keepdims=True))
        alpha = jnp.exp2(m_prev - m_new)
        p = jnp.exp2(s - m_new)
        l_scr[hh:hh + 1, :] = alpha * l_scr[hh:hh + 1, :] + jnp.sum(p, axis=0, keepdims=True)
        acc_scr[hs, :] = alpha * acc_scr[hs, :] + jnp.dot(vbuf[j % 2, hs, :], p.astype(BF16),
                                                          preferred_element_type=F32)
        m_scr[hh:hh + 1, :] = m_new

    def tile_steps(j, has_next):
        v_copy(j).wait()
        for hh in range(N_ATTN_HEADS):
            ahead = hh + SCORE_LOOKAHEAD
            if ahead < N_ATTN_HEADS:
                scores_head(j, ahead)
            elif has_next:
                if ahead == N_ATTN_HEADS:
                    @pl.when(j + 2 < nk)
                    def _():
                        k_copy(j + 2).start()

                    k_copy(j + 1).wait()
                    select_bias(j + 1)
                scores_head(j + 1, ahead - N_ATTN_HEADS)
            attend_head(j, hh)

    @pl.when(nk > 1)
    def _():
        k_copy(1).start()

    k_copy(0).wait()
    select_bias(0)
    for hh in range(SCORE_LOOKAHEAD):
        scores_head(0, hh)

    def attend_tile(j, carry):
        v_copy(j + 1).start()
        tile_steps(j, True)
        return carry

    lax.fori_loop(0, nk - 1, attend_tile, 0)
    tile_steps(nk - 1, False)

    for hh in range(N_ATTN_HEADS):
        hs = slice(hh * LANES, (hh + 1) * LANES)
        o_t = acc_scr[hs, :] / l_scr[hh:hh + 1, :]
        o_ref[0, :, hs] = o_t.T.astype(o_ref.dtype)


def _dsa(iq_t, iw_t, ik_bf, q_t, k_bf, v_t, *, past, seq_len, topk, tq, tk):
    b, _, _, t = iq_t.shape
    lp = ik_bf.shape[1]
    nkt = lp // tk
    hd = N_ATTN_HEADS * ATTN_HEAD_DIM
    kern = functools.partial(_dsa_kernel, past=past, seq_len=seq_len, topk=topk, tq=tq, tk=tk)
    return pl.pallas_call(
        kern,
        out_shape=jax.ShapeDtypeStruct((b, t, hd), BF16),
        grid=(b, t // tq),
        in_specs=[
            pl.BlockSpec((1, N_IDX_HEADS, IDX_DIM, tq), lambda bi, i: (bi, 0, 0, i)),
            pl.BlockSpec((1, N_IDX_HEADS, tq), lambda bi, i: (bi, 0, i)),
            pl.BlockSpec((1, lp, IDX_DIM), lambda bi, i: (bi, 0, 0)),
            pl.BlockSpec((1, N_ATTN_HEADS, ATTN_HEAD_DIM, tq), lambda bi, i: (bi, 0, 0, i)),
            pl.BlockSpec(memory_space=pl.ANY),
            pl.BlockSpec(memory_space=pl.ANY),
        ],
        out_specs=pl.BlockSpec((1, tq, hd), lambda bi, i: (bi, i, 0)),
        scratch_shapes=[
            pltpu.VMEM((nkt, tk, tq), F32),
            pltpu.VMEM((nkt, tk, tq), jnp.bfloat16),
            pltpu.VMEM((2, tk, hd), BF16),
            pltpu.VMEM((2, hd, tk), BF16),
            pltpu.SemaphoreType.DMA((2, 2)),
            pltpu.VMEM((N_ATTN_HEADS, tq), F32),
            pltpu.VMEM((N_ATTN_HEADS, tq), F32),
            pltpu.VMEM((hd, tq), F32),
            pltpu.VMEM((SCORE_SLOTS, tk, tq), F32),
            pltpu.VMEM((tk, tq), F32),
        ],
        compiler_params=_cparams(("arbitrary", "arbitrary")),
        name="dsa_attend",
    )(iq_t, iw_t, ik_bf, q_t, k_bf, v_t)


def _gla_kernel(q_ref, k_ref, v_ref, lg_ref, gr_ref, g_ref, s0_ref, o_ref, sT_out, st_scr, *, n_chunks):
    tstep = pl.program_id(2)

    @pl.when(tstep == 0)
    def _():
        st_scr[...] = s0_ref[0, 0]

    row = lax.broadcasted_iota(jnp.int32, (CHUNK, GLA_DK), 0)
    tri = (lax.broadcasted_iota(jnp.int32, (CHUNK, CHUNK), 0)
           >= lax.broadcasted_iota(jnp.int32, (CHUNK, CHUNK), 1))

    for c in range(n_chunks):
        rs = slice(c * CHUNK, (c + 1) * CHUNK)
        bcum = lg_ref[0, rs, :]
        shift = 1
        while shift < CHUNK:
            bcum = bcum + jnp.where(row >= shift, pltpu.roll(bcum, shift, 0), 0.0)
            shift *= 2
        bl = bcum[CHUNK - 1:CHUNK, :]
        q = q_ref[0, rs, :]
        k = k_ref[0, rs, :]
        v = v_ref[0, rs, :]
        qe = (q * jnp.exp(bcum)).astype(BF16)
        ke = (k * jnp.exp(-bcum)).astype(BF16)
        kd = (k * jnp.exp(bl - bcum)).astype(BF16)
        st = st_scr[...]
        inter = lax.dot_general(qe, st.astype(BF16), (((1,), (1,)), ((), ())), preferred_element_type=F32)
        a = lax.dot_general(qe, ke, (((1,), (1,)), ((), ())), preferred_element_type=F32)
        a = jnp.where(tri, a, 0.0)
        o = inter + jnp.dot(a.astype(BF16), v, preferred_element_type=F32)
        st_scr[...] = st * jnp.exp(bl) + lax.dot_general(v, kd, (((0,), (0,)), ((), ())),
                                                          preferred_element_type=F32)
        n = o * lax.rsqrt(jnp.mean(o * o, axis=-1, keepdims=True) + EPS) * g_ref[...]
        o_ref[0, rs, :] = (n * _silu(gr_ref[0, rs, :])).astype(o_ref.dtype)

    @pl.when(tstep == pl.num_programs(2) - 1)
    def _():
        sT_out[0, 0] = st_scr[...]


def _gla(gq, gk, gv, lg, gr, g_gla, s0t, tg):
    b, t, _ = gq.shape
    kern = functools.partial(_gla_kernel, n_chunks=tg // CHUNK)
    tok = lambda w: pl.BlockSpec((1, tg, w), lambda bi, h, m: (bi, m, h))
    st_spec = pl.BlockSpec((1, 1, GLA_DV, GLA_DK), lambda bi, h, m: (bi, h, 0, 0))
    return pl.pallas_call(
        kern,
        out_shape=(jax.ShapeDtypeStruct((b, t, GLA_HEADS * GLA_DV), BF16),
                   jax.ShapeDtypeStruct((b, GLA_HEADS, GLA_DV, GLA_DK), F32)),
        grid=(b, GLA_HEADS, t // tg),
        in_specs=[tok(GLA_DK), tok(GLA_DK), tok(GLA_DV), tok(GLA_DK), tok(GLA_DV),
                  pl.BlockSpec((1, GLA_DV), lambda bi, h, m: (0, 0)), st_spec],
        out_specs=(tok(GLA_DV), st_spec),
        scratch_shapes=[pltpu.VMEM((GLA_DV, GLA_DK), F32)],
        compiler_params=_cparams(("arbitrary", "arbitrary", "arbitrary")),
        name="gla",
    )(gq, gk, gv, lg, gr, g_gla.reshape(1, GLA_DV), s0t)


def _outproj_kernel(oa_ref, og_ref, wa_ref, wg_ref, x_ref, mod_ref, gffn_ref, x1_out, h2_out):
    bb, tb, d = x_ref.shape
    rows = bb * tb
    oa = oa_ref[...].reshape(rows, oa_ref.shape[-1])
    og = og_ref[...].reshape(rows, og_ref.shape[-1])
    mix = (jnp.dot(oa, wa_ref[...], preferred_element_type=F32)
           + jnp.dot(og, wg_ref[...], preferred_element_type=F32)).reshape(bb, tb, d)
    mod = mod_ref[...]
    x1 = x_ref[...] + mod[:, :, 2 * d:3 * d] * mix
    x1_out[...] = x1
    y = x1 * lax.rsqrt(jnp.mean(x1 * x1, axis=-1, keepdims=True) + EPS) * gffn_ref[...]
    h2_out[...] = (y * (1.0 + mod[:, :, 4 * d:5 * d]) + mod[:, :, 3 * d:4 * d]).astype(BF16)


def _outproj(o_a, o_g, wout_a, wout_g, x, mod, g_ffn, bb, tb):
    b, t, d = x.shape
    tokspec = lambda w: pl.BlockSpec((bb, tb, w), lambda i, m: (i, m, 0))
    const = lambda shp: pl.BlockSpec(shp, lambda i, m: tuple(0 for _ in shp))
    return pl.pallas_call(
        _outproj_kernel,
        out_shape=(jax.ShapeDtypeStruct((b, t, d), F32), jax.ShapeDtypeStruct((b, t, d), BF16)),
        grid=(b // bb, t // tb),
        in_specs=[tokspec(o_a.shape[-1]), tokspec(o_g.shape[-1]), const(wout_a.shape), const(wout_g.shape),
                  tokspec(d), pl.BlockSpec((bb, 1, mod.shape[-1]), lambda i, m: (i, 0, 0)), const((1, 1, d))],
        out_specs=(tokspec(d), tokspec(d)),
        compiler_params=_cparams(("arbitrary", "arbitrary")),
        name="outproj_norm",
    )(o_a, o_g, wout_a, wout_g, x, mod, g_ffn.reshape(1, 1, d))


def _ffn_kernel(h_ref, halo_ref, wa_ref, wb_ref, wca_ref, wcb_ref, bca_ref, bcb_ref, sa_ref, sb_ref,
                wd_ref, x1_ref, mod_ref, y_out, nba_out, nbb_out, acc_scr):
    m = pl.program_id(1)
    f = pl.program_id(2)
    bb, tb, d = h_ref.shape
    rows = bb * tb
    tf = wa_ref.shape[1]
    hb = h_ref[...].reshape(rows, d)
    hh = halo_ref[...].reshape(bb * SUBLANES, d)
    tpos = lax.broadcasted_iota(jnp.int32, (rows, tf), 0) % tb

    def conv_branch(w_ref, wc_ref, bc_ref, s_ref, nb_out):
        u = jnp.dot(hb, w_ref[...], preferred_element_type=F32)
        pu = jnp.dot(hh, w_ref[...], preferred_element_type=F32).reshape(bb, SUBLANES, tf)
        prev = jnp.where(m == 0, s_ref[...], pu[:, SUBLANES - (CONV_W - 1):, :])
        bcast = lambda r: jnp.broadcast_to(prev[:, r:r + 1, :], (bb, tb, tf)).reshape(rows, tf)
        p0, p1 = bcast(0), bcast(1)
        u1 = jnp.where(tpos == 0, p1, pltpu.roll(u, 1, 0))
        u2 = jnp.where(tpos == 0, p0, jnp.where(tpos == 1, p1, pltpu.roll(u, 2, 0)))
        wc = wc_ref[...]
        nb_out[:, 0] = u.reshape(bb, tb, tf)[:, tb - (CONV_W - 1):, :]
        return bc_ref[...] + wc[0:1] * u2 + wc[1:2] * u1 + wc[2:3] * u

    ua = conv_branch(wa_ref, wca_ref, bca_ref, sa_ref, nba_out)
    ub = conv_branch(wb_ref, wcb_ref, bcb_ref, sb_ref, nbb_out)
    act = (_silu(ua) * ub).astype(BF16)
    part = jnp.dot(act, wd_ref[...], preferred_element_type=F32)

    @pl.when(f == 0)
    def _():
        acc_scr[...] = part

    @pl.when(f > 0)
    def _():
        acc_scr[...] += part

    @pl.when(f == pl.num_programs(2) - 1)
    def _():
        mod = mod_ref[...]
        y_out[...] = x1_ref[...] + mod[:, :, 5 * d:6 * d] * acc_scr[...].reshape(bb, tb, d)


def _ffn(h2, w_up, w_conv, b_conv, conv_state, w_down, x1, mod, bb, tb, tf):
    b, t, d = x1.shape
    dff = w_down.shape[0]
    nf = dff // tf
    hb = tb // SUBLANES
    tokspec = lambda w: pl.BlockSpec((bb, tb, w), lambda i, m, f: (i, m, 0))
    a_col = lambda r: pl.BlockSpec((r, tf), lambda i, m, f: (0, f))
    b_col = lambda r: pl.BlockSpec((r, tf), lambda i, m, f: (0, nf + f))
    st_a = pl.BlockSpec((bb, CONV_W - 1, tf), lambda i, m, f: (i, 0, f))
    st_b = pl.BlockSpec((bb, CONV_W - 1, tf), lambda i, m, f: (i, 0, nf + f))
    nb_spec = pl.BlockSpec((bb, 1, CONV_W - 1, tf), lambda i, m, f: (i, m, 0, f))
    bc2 = b_conv.reshape(1, 2 * dff)
    return pl.pallas_call(
        _ffn_kernel,
        out_shape=(jax.ShapeDtypeStruct((b, t, d), F32),
                   jax.ShapeDtypeStruct((b, t // tb, CONV_W - 1, dff), F32),
                   jax.ShapeDtypeStruct((b, t // tb, CONV_W - 1, dff), F32)),
        grid=(b // bb, t // tb, nf),
        in_specs=[
            tokspec(d),
            pl.BlockSpec((bb, SUBLANES, d), lambda i, m, f: (i, jnp.maximum(m * hb - 1, 0), 0)),
            a_col(d), b_col(d), a_col(CONV_W), b_col(CONV_W), a_col(1), b_col(1), st_a, st_b,
            pl.BlockSpec((tf, d), lambda i, m, f: (f, 0)),
            tokspec(d),
            pl.BlockSpec((bb, 1, mod.shape[-1]), lambda i, m, f: (i, 0, 0)),
        ],
        out_specs=(tokspec(d), nb_spec, nb_spec),
        scratch_shapes=[pltpu.VMEM((bb * tb, d), F32)],
        compiler_params=_cparams(("arbitrary", "arbitrary", "arbitrary")),
        name="conv_ffn",
    )(h2, h2, w_up, w_up, w_conv, w_conv, bc2, bc2, conv_state, conv_state, w_down, x1, mod)


def _prep_weights(w_in, w_gate2, w_out, w_up, w_down):
    hd = N_ATTN_HEADS * ATTN_HEAD_DIM
    sizes = (hd, hd, hd, N_IDX_HEADS * IDX_DIM, IDX_DIM, N_IDX_HEADS,
             GLA_HEADS * GLA_DK, GLA_HEADS * GLA_DK, GLA_HEADS * GLA_DV, GLA_HEADS * GLA_DV, GLA_GATE_RANK)
    offs = np.concatenate([[0], np.cumsum(sizes)])
    col = lambda n: w_in[:, int(offs[n]):int(offs[n + 1])]
    wbig = jnp.concatenate([col(0), col(1), col(2), col(3), col(6), col(7), col(8), col(9)], axis=1).astype(BF16)
    pad = LANES - (IDX_DIM + N_IDX_HEADS + GLA_GATE_RANK)
    wsmall = jnp.concatenate([col(4), col(5), col(10), jnp.zeros((w_in.shape[0], pad), w_in.dtype)],
                             axis=1).astype(BF16)
    lo = IDX_DIM + N_IDX_HEADS
    wg2p = jnp.zeros((LANES, w_gate2.shape[1]), F32).at[lo:lo + GLA_GATE_RANK].set(w_gate2).astype(BF16)
    return dict(wbig=wbig, wsmall=wsmall, wg2p=wg2p,
                wout_a=w_out[:hd].astype(BF16), wout_g=w_out[hd:].astype(BF16),
                w_up=w_up.astype(BF16), w_down=w_down.astype(BF16))


def _layer(x, mod, past, past_k, past_v, past_ik, gla_s0, conv_state, lw, pw, *, bb, tb, tq, tk, tg, tf):
    (g_mix, g_ffn, g_q, g_k, b_gate2, g_gla, w_conv, b_conv) = lw
    b, t, d = x.shape
    hd = N_ATTN_HEADS * ATTN_HEAD_DIM
    pos = past + jnp.arange(t, dtype=jnp.int32)
    ropea = _rope_table(pos, ROPE_DIM // 2, LANES, LANES)
    ropei = _rope_table(pos, IDX_ROPE_DIM // 2, IDX_DIM, LANES)
    ropes = _rope_table(pos, IDX_ROPE_DIM // 2, LANES, IDX_DIM)

    (q_bf, k, v, iq, small, lg, gq, gk, gv, gr) = _inproj(
        x, mod, g_mix, pw["wbig"], pw["wsmall"], g_q, g_k, ropea, ropei, ropes, pw["wg2p"], b_gate2, bb, tb)
    ik = small[:, :, :IDX_DIM]
    iw = small[:, :, IDX_DIM:IDX_DIM + N_IDX_HEADS]

    k_bf, v_bf, ik_bf = k.astype(BF16), v.astype(BF16), ik.astype(BF16)
    if past_k is not None:
        k_bf = jnp.concatenate([past_k.reshape(b, past, hd).astype(BF16), k_bf], axis=1)
        v_bf = jnp.concatenate([past_v.reshape(b, past, hd).astype(BF16), v_bf], axis=1)
        ik_bf = jnp.concatenate([past_ik.astype(BF16), ik_bf], axis=1)
    seq_len = past + t
    lp = -(-seq_len // tk) * tk
    if lp != seq_len:
        padw = ((0, 0), (0, lp - seq_len), (0, 0))
        k_bf, v_bf, ik_bf = jnp.pad(k_bf, padw), jnp.pad(v_bf, padw), jnp.pad(ik_bf, padw)
    tpad = -(-t // tq) * tq
    qpad = lambda a: jnp.pad(a, ((0, 0), (0, tpad - t), (0, 0))) if tpad != t else a
    iq_t = qpad(iq).reshape(b, tpad, N_IDX_HEADS, IDX_DIM).transpose(0, 2, 3, 1)
    iw_t = qpad(iw).transpose(0, 2, 1)
    q_t = qpad(q_bf).reshape(b, tpad, N_ATTN_HEADS, ATTN_HEAD_DIM).transpose(0, 2, 3, 1)
    v_t = v_bf.transpose(0, 2, 1)
    topk = min(TOPK_MAX, seq_len // 4)
    o_a = _dsa(iq_t, iw_t, ik_bf, q_t, k_bf, v_t, past=past, seq_len=seq_len, topk=topk, tq=tq, tk=tk)
    o_a = o_a[:, :t]

    s0t = jnp.swapaxes(gla_s0, -1, -2)
    o_g, s_t = _gla(gq, gk, gv, lg, gr, g_gla, s0t, tg)
    s_new = jnp.swapaxes(s_t, -1, -2)

    x1, h2 = _outproj(o_a, o_g, pw["wout_a"], pw["wout_g"], x, mod, g_ffn, bb, tb)
    y, nb_a, nb_b = _ffn(h2, pw["w_up"], w_conv, b_conv, conv_state, pw["w_down"], x1, mod, bb, tb, tf)
    new_buf = jnp.concatenate([nb_a[:, -1], nb_b[:, -1]], axis=-1)
    return (y, k.reshape(b, t, N_ATTN_HEADS, ATTN_HEAD_DIM), v.reshape(b, t, N_ATTN_HEADS, ATTN_HEAD_DIM),
            ik, s_new, new_buf)


def _tiles(b, t, seq_len):
    if t >= 512:
        bb, tb = 1, 512
    else:
        bb, tb = min(b, 512 // t), t
    tq = 256 if t % 256 == 0 else LANES
    tk = 256 if seq_len % 256 == 0 else LANES
    tg = min(t, 256)
    return dict(bb=bb, tb=tb, tq=tq, tk=tk, tg=tg, tf=512)


def kernel(x_prompt, x_sample, c_prompt, c_sample, cache_k, cache_v, cache_idx_k, state_gla, state_ffn_conv,
           w_ada, b_ada, g_mix, g_ffn, w_in, g_q, g_k, w_gate2, b_gate2, g_gla, w_out, w_up, w_conv, b_conv,
           w_down):
    bp, s, d = x_prompt.shape
    bs, ts, _ = x_sample.shape
    depth = w_ada.shape[0]
    past = cache_k.shape[2]
    dff = w_down.shape[1]
    dt = x_prompt.dtype

    y_p, y_s = x_prompt, x_sample
    outs = [[] for _ in range(10)]
    for l in range(depth):
        n_c = bp + bs
        rows = -(-n_c // SUBLANES) * SUBLANES
        c_all = jnp.concatenate([c_prompt, c_sample, jnp.zeros((rows - n_c, d), dt)], axis=0)
        mod = _ada(c_all, w_ada[l], b_ada[l])
        mod_p = mod[:bp].reshape(bp, 1, 6 * d)
        mod_s = mod[bp:n_c].reshape(bs, 1, 6 * d)
        pw = _prep_weights(w_in[l], w_gate2[l], w_out[l], w_up[l], w_down[l])
        lw = (g_mix[l], g_ffn[l], g_q[l], g_k[l], b_gate2[l], g_gla[l], w_conv[l], b_conv[l])

        gla0 = jnp.zeros((bp, GLA_HEADS, GLA_DK, GLA_DV), dt)
        conv0 = jnp.zeros((bp, CONV_W - 1, 2 * dff), dt)
        y_p, k1, v1, ik1, s1, c1 = _layer(y_p, mod_p, 0, None, None, None, gla0, conv0, lw, pw,
                                          **_tiles(bp, s, s))
        y_s, k2, v2, ik2, s2, c2 = _layer(y_s, mod_s, past, cache_k[l], cache_v[l], cache_idx_k[l],
                                          state_gla[l], state_ffn_conv[l], lw, pw,
                                          **_tiles(bs, ts, past + ts))
        for lst, val in zip(outs, (k1, v1, ik1, s1, c1, k2, v2, ik2, s2, c2)):
            lst.append(val)

    return (y_p, y_s) + tuple(jnp.stack(o) for o in outs)
```

```python
import functools
import math

import jax
import jax.numpy as jnp
import numpy as np
from jax import lax
from jax.experimental import pallas as pl
from jax.experimental.pallas import tpu as pltpu

F32 = jnp.float32
BF16 = jnp.bfloat16

CHUNK = 64
N_ATTN_HEADS = 8
ATTN_HEAD_DIM = 128
ROPE_DIM = 32
ROPE_THETA = 500000.0
N_IDX_HEADS = 16
IDX_DIM = 64
IDX_ROPE_DIM = 16
TOPK_MAX = 256
GLA_HEADS = 4
GLA_DK = 128
GLA_DV = 256
GLA_GATE_RANK = 16
GLA_GATE_TEMP = 16.0
CONV_W = 3
EPS = 1e-6

LANES = 128
SUBLANES = 8
VMEM_LIMIT_BYTES = 56 * 1024 * 1024

SCORE_LOOKAHEAD = 4
SCORE_SLOTS = 8
FINE_STEPS = 17

MASK_NEG = -1e30
F32_LOWEST = -3.4028235e38


def _cparams(sem):
    return pltpu.CompilerParams(dimension_semantics=sem, vmem_limit_bytes=VMEM_LIMIT_BYTES)


def _sigmoid(x):
    return 1.0 / (1.0 + jnp.exp(-x))


def _silu(x):
    return x * _sigmoid(x)


def _ada_kernel(c_ref, w_ref, b_ref, o_ref):
    s = _silu(c_ref[...]).astype(BF16)
    o_ref[...] = jnp.dot(s, w_ref[...].astype(BF16), preferred_element_type=F32) + b_ref[...]


def _ada(c_all, w_ada, b_ada):
    rows, d = c_all.shape
    n = w_ada.shape[1]
    tn = 1536 if n % 1536 == 0 else n
    return pl.pallas_call(
        _ada_kernel,
        out_shape=jax.ShapeDtypeStruct((rows, n), F32),
        grid=(n // tn,),
        in_specs=[
            pl.BlockSpec((rows, d), lambda j: (0, 0)),
            pl.BlockSpec((d, tn), lambda j: (0, j)),
            pl.BlockSpec((1, tn), lambda j: (0, j)),
        ],
        out_specs=pl.BlockSpec((rows, tn), lambda j: (0, j)),
        compiler_params=_cparams(("arbitrary",)),
        name="ada_mod",
    )(c_all, w_ada, b_ada.reshape(1, n))


def _rope_table(pos, half, period, active_lanes):
    inv = ROPE_THETA ** (-jnp.arange(half, dtype=F32) / half)
    ang = pos.astype(F32)[:, None] * inv[None, :]
    cos, sin = jnp.cos(ang), jnp.sin(ang)
    lane = np.arange(LANES)
    within = lane % period
    idx = jnp.asarray(within % half)
    first = jnp.asarray((within < half) & (lane < active_lanes))
    second = jnp.asarray((within >= half) & (within < 2 * half) & (lane < active_lanes))
    cos_l, sin_l = cos[:, idx], sin[:, idx]
    c = jnp.where(first | second, cos_l, 1.0)
    s1 = jnp.where(first, -sin_l, 0.0)
    s2 = jnp.where(second, sin_l, 0.0)
    return jnp.stack([c, s1, s2]).astype(F32)


def _rope_apply(v, tab, half):
    return (v * tab[0] + pltpu.roll(v, LANES - half, 1) * tab[1] + pltpu.roll(v, half, 1) * tab[2])


def _tile_rows(tab_ref, bb):
    t = tab_ref[...]
    if bb == 1:
        return t
    tb = t.shape[1]
    return jnp.broadcast_to(t[:, None], (3, bb, tb, LANES)).reshape(3, bb * tb, LANES)


def _inproj_kernel(x_ref, mod_ref, gmix_ref, wbig_ref, wsmall_ref, gq_ref, gk_ref,
                   ropea_ref, ropei_ref, ropes_ref, wg2_ref, bg2_ref,
                   q_out, k_out, v_out, iq_out, small_out, lg_out, gq_out, gk_out, gv_out, gr_out,
                   h_scr):
    j = pl.program_id(2)
    bb, tb, d = x_ref.shape
    rows = bb * tb

    @pl.when(j == 0)
    def _():
        x = x_ref[...]
        ms = jnp.mean(x * x, axis=-1, keepdims=True)
        y = x * lax.rsqrt(ms + EPS) * gmix_ref[...]
        mod = mod_ref[...]
        h = y * (1.0 + mod[:, :, d:2 * d]) + mod[:, :, 0:d]
        hb = h.astype(BF16).reshape(rows, d)
        h_scr[...] = hb
        small = jnp.dot(hb, wsmall_ref[...], preferred_element_type=F32)
        small = _rope_apply(small, _tile_rows(ropes_ref, bb), IDX_ROPE_DIM // 2)
        small_out[...] = small.reshape(bb, tb, LANES)
        pre = jnp.dot(small.astype(BF16), wg2_ref[...], preferred_element_type=F32) + bg2_ref[...]
        lsig = jnp.minimum(pre, 0.0) - jnp.log(1.0 + jnp.exp(-jnp.abs(pre)))
        lg_out[...] = (lsig / GLA_GATE_TEMP).reshape(bb, tb, lg_out.shape[-1])

    y = jnp.dot(h_scr[...], wbig_ref[...], preferred_element_type=F32)

    def headnorm_rope(g_ref, scale):
        tab = _tile_rows(ropea_ref, bb)
        outs = []
        for hh in range(N_ATTN_HEADS):
            yh = y[:, hh * LANES:(hh + 1) * LANES]
            n = yh * lax.rsqrt(jnp.mean(yh * yh, axis=-1, keepdims=True) + EPS) * g_ref[...]
            r = _rope_apply(n, tab, ROPE_DIM // 2)
            outs.append(r * scale if scale != 1.0 else r)
        return outs

    @pl.when(j == 0)
    def _():
        for hh, r in enumerate(headnorm_rope(gq_ref, ATTN_HEAD_DIM ** -0.5 * math.log2(math.e))):
            q_out[:, :, hh * LANES:(hh + 1) * LANES] = r.astype(BF16).reshape(bb, tb, LANES)

    @pl.when(j == 1)
    def _():
        for hh, r in enumerate(headnorm_rope(gk_ref, 1.0)):
            k_out[:, :, hh * LANES:(hh + 1) * LANES] = r.reshape(bb, tb, LANES)

    @pl.when(j == 2)
    def _():
        v_out[...] = y.reshape(bb, tb, y.shape[-1])

    @pl.when(j == 3)
    def _():
        tab = _tile_rows(ropei_ref, bb)
        for c in range(y.shape[-1] // LANES):
            r = _rope_apply(y[:, c * LANES:(c + 1) * LANES], tab, IDX_ROPE_DIM // 2)
            iq_out[:, :, c * LANES:(c + 1) * LANES] = r.astype(BF16).reshape(bb, tb, LANES)

    @pl.when(j == 4)
    def _():
        half = y.shape[-1] // 2
        gq_out[...] = (y[:, :half] * GLA_DK ** -0.5).reshape(bb, tb, half)
        gk_out[...] = y[:, half:].reshape(bb, tb, half)

    @pl.when(j == 5)
    def _():
        gv_out[...] = y.astype(BF16).reshape(bb, tb, y.shape[-1])

    @pl.when(j == 6)
    def _():
        gr_out[...] = y.reshape(bb, tb, y.shape[-1])


def _inproj(x, mod, g_mix, wbig, wsmall, g_q, g_k, ropea, ropei, ropes, wg2p, bg2, bb, tb):
    b, t, d = x.shape
    gw = 1024
    ngroups = wbig.shape[1] // gw
    glak = GLA_HEADS * GLA_DK
    rows = bb * tb
    tok = lambda w, dt: jax.ShapeDtypeStruct((b, t, w), dt)
    tokspec = lambda w: pl.BlockSpec((bb, tb, w), lambda i, m, j: (i, m, 0))
    const = lambda shp: pl.BlockSpec(shp, lambda i, m, j: tuple(0 for _ in shp))
    tabspec = pl.BlockSpec((3, tb, LANES), lambda i, m, j: (0, m, 0))
    out_shapes = (tok(1024, BF16), tok(1024, F32), tok(1024, F32), tok(1024, BF16), tok(LANES, F32),
                  tok(glak, F32), tok(glak, F32), tok(glak, F32), tok(1024, BF16), tok(1024, F32))
    out_specs = tuple(tokspec(s.shape[-1]) for s in out_shapes)
    return pl.pallas_call(
        _inproj_kernel,
        out_shape=out_shapes,
        grid=(b // bb, t // tb, ngroups),
        in_specs=[
            tokspec(d),
            pl.BlockSpec((bb, 1, mod.shape[-1]), lambda i, m, j: (i, 0, 0)),
            const((1, 1, d)),
            pl.BlockSpec((d, gw), lambda i, m, j: (0, j)),
            const((d, LANES)),
            const((1, LANES)), const((1, LANES)),
            tabspec, tabspec, tabspec,
            const((LANES, glak)), const((1, glak)),
        ],
        out_specs=out_specs,
        scratch_shapes=[pltpu.VMEM((rows, d), BF16)],
        compiler_params=_cparams(("arbitrary", "arbitrary", "arbitrary")),
        name="norm_inproj",
    )(x, mod, g_mix.reshape(1, 1, d), wbig, wsmall, g_q.reshape(1, LANES), g_k.reshape(1, LANES),
      ropea, ropei, ropes, wg2p, bg2.reshape(1, glak))


def _dsa_kernel(iqt_ref, iwt_ref, ik_ref, qt_ref, k_hbm, vt_hbm, o_ref,
                u_scr, ub_scr, kbuf, vbuf, sem, m_scr, l_scr, acc_scr, s_scr, bias_scr,
                *, past, seq_len, topk, tq, tk):
    b = pl.program_id(0)
    i = pl.program_id(1)
    nkt = u_scr.shape[0]

    last_chunk = (past + (i + 1) * tq - 1) // CHUNK
    n_adm = jnp.minimum(seq_len, (last_chunk + 1) * CHUNK)
    nk = jnp.minimum((n_adm + tk - 1) // tk, nkt)

    def k_copy(j):
        off = pl.multiple_of(j * tk, tk)
        return pltpu.make_async_copy(k_hbm.at[b, pl.ds(off, tk)], kbuf.at[j % 2], sem.at[0, j % 2])

    def v_copy(j):
        off = pl.multiple_of(j * tk, tk)
        return pltpu.make_async_copy(vt_hbm.at[b, :, pl.ds(off, tk)], vbuf.at[j % 2], sem.at[1, j % 2])

    k_copy(0).start()
    v_copy(0).start()

    q_chunk = (past + i * tq + lax.broadcasted_iota(jnp.int32, (tk, tq), 1)) // CHUNK
    key_iota = lax.broadcasted_iota(jnp.int32, (tk, tq), 0)
    idx_scale = (IDX_DIM * N_IDX_HEADS) ** -0.5

    def score_tile(j, carry):
        ik_t = ik_ref[0, pl.ds(pl.multiple_of(j * tk, tk), tk), :]
        acc = jnp.zeros((tk, tq), F32)
        for hh in range(N_IDX_HEADS):
            logits = jnp.dot(ik_t, iqt_ref[0, hh], preferred_element_type=F32)
            acc = acc + iwt_ref[0, hh:hh + 1, :] * jnp.maximum(logits, 0.0)
        kpos = j * tk + key_iota
        adm = (kpos // CHUNK <= q_chunk) & (kpos < seq_len)
        u = jnp.where(adm, acc * idx_scale, -jnp.inf)
        u_scr[j] = u
        ub_scr[j] = u.astype(ub_scr.dtype)
        return carry

    lax.fori_loop(0, nk, score_tile, 0)

    group = 2 if nkt % 2 == 0 else 1
    n_groups = (nk + group - 1) // group

    @pl.when(n_groups * group > nk)
    def _():
        u_scr[nk] = jnp.full((tk, tq), -jnp.inf, F32)
        ub_scr[nk] = jnp.full((tk, tq), -jnp.inf, ub_scr.dtype)

    def count_ge(scr, thr_row):
        rows = SUBLANES * (4 // jnp.dtype(scr.dtype).itemsize)
        thr_b = jnp.broadcast_to(thr_row.astype(scr.dtype), (rows, tq))
        one, zero = jnp.ones((), scr.dtype), jnp.zeros((), scr.dtype)

        def body(g, cnt):
            for r in range(group):
                x = scr[g * group + r].reshape(tk // rows, rows, tq)
                hit = jnp.where(x >= thr_b[None], one, zero)
                parts = [hit[n] for n in range(tk // rows)]
                while len(parts) > 1:
                    parts = [parts[n] + parts[n + 1] for n in range(0, len(parts), 2)]
                cnt = cnt + parts[0].astype(F32)
            return cnt

        cnt = lax.fori_loop(0, n_groups, body, jnp.zeros((rows, tq), F32))
        return jnp.sum(cnt, axis=0, keepdims=True)

    int_min = jnp.int32(-2 ** 31)
    kf = float(topk)

    def coarse_value(t_biased):
        mono = t_biased ^ int_min
        bits = jnp.where(mono >= 0, mono, mono ^ jnp.int32(0x7FFF0000))
        return lax.bitcast_convert_type(bits, F32)

    def coarse_step(it, t_biased):
        cand = t_biased | lax.shift_left(jnp.int32(1), jnp.int32(31) - it)
        return jnp.where(count_ge(ub_scr, coarse_value(cand)) >= kf, cand, t_biased)

    t16 = lax.fori_loop(0, 16, coarse_step, jnp.zeros((1, tq), jnp.int32))
    few = (t16 >= 0) & (t16 <= jnp.int32(0x007F0000))

    def to_mono(bits):
        return jnp.where(bits >= 0, bits, bits ^ jnp.int32(0x7FFFFFFF))

    c_mono = to_mono(lax.bitcast_convert_type(coarse_value(t16), jnp.int32))
    lo0 = c_mono - jnp.int32(0x8000)
    hi0 = c_mono + jnp.int32(0x10000)
    hi0 = jnp.where(few, lo0, hi0)

    def fine_cond(state):
        it, _, _, pending = state
        return (it < FINE_STEPS) & (pending > 0)

    def fine_step(state):
        it, lo, hi, _ = state
        mid = lo + lax.shift_right_arithmetic(hi - lo + 1, jnp.int32(1))
        total = count_ge(u_scr, lax.bitcast_convert_type(to_mono(mid), F32))
        ok = total >= kf
        lo = jnp.where(ok, mid, lo)
        hi = jnp.where(total == kf, mid, jnp.where(ok, hi, mid - 1))
        return it + 1, lo, hi, jnp.max(jnp.where(hi > lo, 1, 0))

    _, lo_fin, _, _ = lax.while_loop(fine_cond, fine_step,
                                     (jnp.int32(0), lo0, hi0, jnp.max(jnp.where(hi0 > lo0, 1, 0))))
    thr_fin = lax.bitcast_convert_type(to_mono(lo_fin), F32)
    thr = jnp.where(few, F32_LOWEST, jnp.maximum(thr_fin, F32_LOWEST))

    m_scr[...] = jnp.full(m_scr.shape, MASK_NEG, F32)
    l_scr[...] = jnp.zeros(l_scr.shape, F32)
    acc_scr[...] = jnp.zeros(acc_scr.shape, F32)

    def select_bias(j):
        bias_scr[...] = jnp.where(u_scr[j] >= thr, 0.0, MASK_NEG)

    n_slots = s_scr.shape[0]

    def scores_head(j, hh):
        hs = slice(hh * LANES, (hh + 1) * LANES)
        s_scr[hh % n_slots] = (jnp.dot(kbuf[j % 2, :, hs], qt_ref[0, hh], preferred_element_type=F32)
                               + bias_scr[...])

    def attend_head(j, hh):
        hs = slice(hh * LANES, (hh + 1) * LANES)
        s = s_scr[hh % n_slots]
        m_prev = m_scr[hh:hh + 1, :]
        m_new = jnp.maximum(m_prev, jnp.max(s, axis=0, keepdims=True))
        alpha = jnp.exp2(m_prev - m_new)
        p = jnp.exp2(s - m_new)
        l_scr[hh:hh + 1, :] = alpha * l_scr[hh:hh + 1, :] + jnp.sum(p, axis=0, keepdims=True)
        acc_scr[hs, :] = alpha * acc_scr[hs, :] + jnp.dot(vbuf[j % 2, hs, :], p.astype(BF16),
                                                          preferred_element_type=F32)
        m_scr[hh:hh + 1, :] = m_new

    def tile_steps(j, has_next):
        v_copy(j).wait()
        for hh in range(N_ATTN_HEADS):
            ahead = hh + SCORE_LOOKAHEAD
            if ahead < N_ATTN_HEADS:
                scores_head(j, ahead)
            elif has_next:
                if ahead == N_ATTN_HEADS:
                    @pl.when(j + 2 < nk)
                    def _():
                        k_copy(j + 2).start()

                    k_copy(j + 1).wait()
                    select_bias(j + 1)
                scores_head(j + 1, ahead - N_ATTN_HEADS)
            attend_head(j, hh)

    @pl.when(nk > 1)
    def _():
        k_copy(1).start()

    k_copy(0).wait()
    select_bias(0)
    for hh in range(SCORE_LOOKAHEAD):
        scores_head(0, hh)

    def attend_tile(j, carry):
        v_copy(j + 1).start()
        tile_steps(j, True)
        return carry

    lax.fori_loop(0, nk - 1, attend_tile, 0)
    tile_steps(nk - 1, False)

    for hh in range(N_ATTN_HEADS):
        hs = slice(hh * LANES, (hh + 1) * LANES)
        o_t = acc_scr[hs, :] / l_scr[hh:hh + 1, :]
        o_ref[0, :, hs] = o_t.T.astype(o_ref.dtype)


def _dsa(iq_t, iw_t, ik_bf, q_t, k_bf, v_t, *, past, seq_len, topk, tq, tk):
    b, _, _, t = iq_t.shape
    lp = ik_bf.shape[1]
    nkt = lp // tk
    hd = N_ATTN_HEADS * ATTN_HEAD_DIM
    kern = functools.partial(_dsa_kernel, past=past, seq_len=seq_len, topk=topk, tq=tq, tk=tk)
    return pl.pallas_call(
        kern,
        out_shape=jax.ShapeDtypeStruct((b, t, hd), BF16),
        grid=(b, t // tq),
        in_specs=[
            pl.BlockSpec((1, N_IDX_HEADS, IDX_DIM, tq), lambda bi, i: (bi, 0, 0, i)),
            pl.BlockSpec((1, N_IDX_HEADS, tq), lambda bi, i: (bi, 0, i)),
            pl.BlockSpec((1, lp, IDX_DIM), lambda bi, i: (bi, 0, 0)),
            pl.BlockSpec((1, N_ATTN_HEADS, ATTN_HEAD_DIM, tq), lambda bi, i: (bi, 0, 0, i)),
            pl.BlockSpec(memory_space=pl.ANY),
            pl.BlockSpec(memory_space=pl.ANY),
        ],
        out_specs=pl.BlockSpec((1, tq, hd), lambda bi, i: (bi, i, 0)),
        scratch_shapes=[
            pltpu.VMEM((nkt, tk, tq), F32),
            pltpu.VMEM((nkt, tk, tq), jnp.bfloat16),
            pltpu.VMEM((2, tk, hd), BF16),
            pltpu.VMEM((2, hd, tk), BF16),
            pltpu.SemaphoreType.DMA((2, 2)),
            pltpu.VMEM((N_ATTN_HEADS, tq), F32),
            pltpu.VMEM((N_ATTN_HEADS, tq), F32),
            pltpu.VMEM((hd, tq), F32),
            pltpu.VMEM((SCORE_SLOTS, tk, tq), F32),
            pltpu.VMEM((tk, tq), F32),
        ],
        compiler_params=_cparams(("arbitrary", "arbitrary")),
        name="dsa_attend",
    )(iq_t, iw_t, ik_bf, q_t, k_bf, v_t)


def _gla_kernel(q_ref, k_ref, v_ref, lg_ref, gr_ref, g_ref, s0_ref, o_ref, sT_out, st_scr, *, n_chunks):
    tstep = pl.program_id(2)

    @pl.when(tstep == 0)
    def _():
        st_scr[...] = s0_ref[0, 0]

    row = lax.broadcasted_iota(jnp.int32, (CHUNK, GLA_DK), 0)
    tri = (lax.broadcasted_iota(jnp.int32, (CHUNK, CHUNK), 0)
           >= lax.broadcasted_iota(jnp.int32, (CHUNK, CHUNK), 1))

    for c in range(n_chunks):
        rs = slice(c * CHUNK, (c + 1) * CHUNK)
        bcum = lg_ref[0, rs, :]
        shift = 1
        while shift < CHUNK:
            bcum = bcum + jnp.where(row >= shift, pltpu.roll(bcum, shift, 0), 0.0)
            shift *= 2
        bl = bcum[CHUNK - 1:CHUNK, :]
        q = q_ref[0, rs, :]
        k = k_ref[0, rs, :]
        v = v_ref[0, rs, :]
        qe = (q * jnp.exp(bcum)).astype(BF16)
        ke = (k * jnp.exp(-bcum)).astype(BF16)
        kd = (k * jnp.exp(bl - bcum)).astype(BF16)
        st = st_scr[...]
        inter = lax.dot_general(qe, st.astype(BF16), (((1,), (1,)), ((), ())), preferred_element_type=F32)
        a = lax.dot_general(qe, ke, (((1,), (1,)), ((), ())), preferred_element_type=F32)
        a = jnp.where(tri, a, 0.0)
        o = inter + jnp.dot(a.astype(BF16), v, preferred_element_type=F32)
        st_scr[...] = st * jnp.exp(bl) + lax.dot_general(v, kd, (((0,), (0,)), ((), ())),
                                                          preferred_element_type=F32)
        n = o * lax.rsqrt(jnp.mean(o * o, axis=-1, keepdims=True) + EPS) * g_ref[...]
        o_ref[0, rs, :] = (n * _silu(gr_ref[0, rs, :])).astype(o_ref.dtype)

    @pl.when(tstep == pl.num_programs(2) - 1)
    def _():
        sT_out[0, 0] = st_scr[...]


def _gla(gq, gk, gv, lg, gr, g_gla, s0t, tg):
    b, t, _ = gq.shape
    kern = functools.partial(_gla_kernel, n_chunks=tg // CHUNK)
    tok = lambda w: pl.BlockSpec((1, tg, w), lambda bi, h, m: (bi, m, h))
    st_spec = pl.BlockSpec((1, 1, GLA_DV, GLA_DK), lambda bi, h, m: (bi, h, 0, 0))
    return pl.pallas_call(
        kern,
        out_shape=(jax.ShapeDtypeStruct((b, t, GLA_HEADS * GLA_DV), BF16),
                   jax.ShapeDtypeStruct((b, GLA_HEADS, GLA_DV, GLA_DK), F32)),
        grid=(b, GLA_HEADS, t // tg),
        in_specs=[tok(GLA_DK), tok(GLA_DK), tok(GLA_DV), tok(GLA_DK), tok(GLA_DV),
                  pl.BlockSpec((1, GLA_DV), lambda bi, h, m: (0, 0)), st_spec],
        out_specs=(tok(GLA_DV), st_spec),
        scratch_shapes=[pltpu.VMEM((GLA_DV, GLA_DK), F32)],
        compiler_params=_cparams(("arbitrary", "arbitrary", "arbitrary")),
        name="gla",
    )(gq, gk, gv, lg, gr, g_gla.reshape(1, GLA_DV), s0t)


def _outproj_kernel(oa_ref, og_ref, wa_ref, wg_ref, x_ref, mod_ref, gffn_ref, x1_out, h2_out):
    bb, tb, d = x_ref.shape
    rows = bb * tb
    oa = oa_ref[...].reshape(rows, oa_ref.shape[-1])
    og = og_ref[...].reshape(rows, og_ref.shape[-1])
    mix = (jnp.dot(oa, wa_ref[...], preferred_element_type=F32)
           + jnp.dot(og, wg_ref[...], preferred_element_type=F32)).reshape(bb, tb, d)
    mod = mod_ref[...]
    x1 = x_ref[...] + mod[:, :, 2 * d:3 * d] * mix
    x1_out[...] = x1
    y = x1 * lax.rsqrt(jnp.mean(x1 * x1, axis=-1, keepdims=True) + EPS) * gffn_ref[...]
    h2_out[...] = (y * (1.0 + mod[:, :, 4 * d:5 * d]) + mod[:, :, 3 * d:4 * d]).astype(BF16)


def _outproj(o_a, o_g, wout_a, wout_g, x, mod, g_ffn, bb, tb):
    b, t, d = x.shape
    tokspec = lambda w: pl.BlockSpec((bb, tb, w), lambda i, m: (i, m, 0))
    const = lambda shp: pl.BlockSpec(shp, lambda i, m: tuple(0 for _ in shp))
    return pl.pallas_call(
        _outproj_kernel,
        out_shape=(jax.ShapeDtypeStruct((b, t, d), F32), jax.ShapeDtypeStruct((b, t, d), BF16)),
        grid=(b // bb, t // tb),
        in_specs=[tokspec(o_a.shape[-1]), tokspec(o_g.shape[-1]), const(wout_a.shape), const(wout_g.shape),
                  tokspec(d), pl.BlockSpec((bb, 1, mod.shape[-1]), lambda i, m: (i, 0, 0)), const((1, 1, d))],
        out_specs=(tokspec(d), tokspec(d)),
        compiler_params=_cparams(("arbitrary", "arbitrary")),
        name="outproj_norm",
    )(o_a, o_g, wout_a, wout_g, x, mod, g_ffn.reshape(1, 1, d))


def _ffn_kernel(h_ref, halo_ref, wa_ref, wb_ref, wca_ref, wcb_ref, bca_ref, bcb_ref, sa_ref, sb_ref,
                wd_ref, x1_ref, mod_ref, y_out, nba_out, nbb_out, acc_scr, hext_scr):
    m = pl.program_id(1)
    f = pl.program_id(2)
    bb, tb, d = h_ref.shape
    rows = bb * tb
    tf = wa_ref.shape[1]
    halo = halo_ref.shape[1]
    tpos = lax.broadcasted_iota(jnp.int32, (rows, tf), 0) % tb

    @pl.when(f == 0)
    def _():
        hext_scr[:, 0:halo, :] = halo_ref[...]
        hext_scr[:, halo:, :] = h_ref[...]

    hext = hext_scr[...].reshape(bb * (tb + halo), d)

    def conv_branch(w_ref, wc_ref, bc_ref, s_ref, nb_out):
        u_ext = jnp.dot(hext, w_ref[...], preferred_element_type=F32).reshape(bb, tb + halo, tf)
        u = u_ext[:, halo:, :].reshape(rows, tf)
        prev = jnp.where(m == 0, s_ref[...], u_ext[:, halo - (CONV_W - 1):halo, :])
        bcast = lambda r: jnp.broadcast_to(prev[:, r:r + 1, :], (bb, tb, tf)).reshape(rows, tf)
        p0, p1 = bcast(0), bcast(1)
        u1 = jnp.where(tpos == 0, p1, pltpu.roll(u, 1, 0))
        u2 = jnp.where(tpos == 0, p0, jnp.where(tpos == 1, p1, pltpu.roll(u, 2, 0)))
        wc = wc_ref[...]
        nb_out[:, 0] = u.reshape(bb, tb, tf)[:, tb - (CONV_W - 1):, :]
        return bc_ref[...] + wc[0:1] * u2 + wc[1:2] * u1 + wc[2:3] * u

    ua = conv_branch(wa_ref, wca_ref, bca_ref, sa_ref, nba_out)
    ub = conv_branch(wb_ref, wcb_ref, bcb_ref, sb_ref, nbb_out)
    act = (_silu(ua) * ub).astype(BF16)
    part = jnp.dot(act, wd_ref[...], preferred_element_type=F32)

    @pl.when(f == 0)
    def _():
        acc_scr[...] = part

    @pl.when(f > 0)
    def _():
        acc_scr[...] += part

    @pl.when(f == pl.num_programs(2) - 1)
    def _():
        mod = mod_ref[...]
        y_out[...] = x1_ref[...] + mod[:, :, 5 * d:6 * d] * acc_scr[...].reshape(bb, tb, d)


def _ffn(h2, w_up, w_conv, b_conv, conv_state, w_down, x1, mod, bb, tb, tf):
    b, t, d = x1.shape
    dff = w_down.shape[0]
    nf = dff // tf
    halo = 2 * SUBLANES
    hb = tb // halo
    tokspec = lambda w: pl.BlockSpec((bb, tb, w), lambda i, m, f: (i, m, 0))
    a_col = lambda r: pl.BlockSpec((r, tf), lambda i, m, f: (0, f))
    b_col = lambda r: pl.BlockSpec((r, tf), lambda i, m, f: (0, nf + f))
    st_a = pl.BlockSpec((bb, CONV_W - 1, tf), lambda i, m, f: (i, 0, f))
    st_b = pl.BlockSpec((bb, CONV_W - 1, tf), lambda i, m, f: (i, 0, nf + f))
    nb_spec = pl.BlockSpec((bb, 1, CONV_W - 1, tf), lambda i, m, f: (i, m, 0, f))
    bc2 = b_conv.reshape(1, 2 * dff)
    return pl.pallas_call(
        _ffn_kernel,
        out_shape=(jax.ShapeDtypeStruct((b, t, d), F32),
                   jax.ShapeDtypeStruct((b, t // tb, CONV_W - 1, dff), F32),
                   jax.ShapeDtypeStruct((b, t // tb, CONV_W - 1, dff), F32)),
        grid=(b // bb, t // tb, nf),
        in_specs=[
            tokspec(d),
            pl.BlockSpec((bb, halo, d), lambda i, m, f: (i, jnp.maximum(m * hb - 1, 0), 0)),
            a_col(d), b_col(d), a_col(CONV_W), b_col(CONV_W), a_col(1), b_col(1), st_a, st_b,
            pl.BlockSpec((tf, d), lambda i, m, f: (f, 0)),
            tokspec(d),
            pl.BlockSpec((bb, 1, mod.shape[-1]), lambda i, m, f: (i, 0, 0)),
        ],
        out_specs=(tokspec(d), nb_spec, nb_spec),
        scratch_shapes=[pltpu.VMEM((bb * tb, d), F32), pltpu.VMEM((bb, tb + halo, d), BF16)],
        compiler_params=_cparams(("arbitrary", "arbitrary", "arbitrary")),
        name="conv_ffn",
    )(h2, h2, w_up, w_up, w_conv, w_conv, bc2, bc2, conv_state, conv_state, w_down, x1, mod)


def _prep_weights(w_in, w_gate2, w_out, w_up, w_down):
    hd = N_ATTN_HEADS * ATTN_HEAD_DIM
    sizes = (hd, hd, hd, N_IDX_HEADS * IDX_DIM, IDX_DIM, N_IDX_HEADS,
             GLA_HEADS * GLA_DK, GLA_HEADS * GLA_DK, GLA_HEADS * GLA_DV, GLA_HEADS * GLA_DV, GLA_GATE_RANK)
    offs = np.concatenate([[0], np.cumsum(sizes)])
    col = lambda n: w_in[:, int(offs[n]):int(offs[n + 1])]
    wbig = jnp.concatenate([col(0), col(1), col(2), col(3), col(6), col(7), col(8), col(9)], axis=1).astype(BF16)
    pad = LANES - (IDX_DIM + N_IDX_HEADS + GLA_GATE_RANK)
    wsmall = jnp.concatenate([col(4), col(5), col(10), jnp.zeros((w_in.shape[0], pad), w_in.dtype)],
                             axis=1).astype(BF16)
    lo = IDX_DIM + N_IDX_HEADS
    wg2p = jnp.zeros((LANES, w_gate2.shape[1]), F32).at[lo:lo + GLA_GATE_RANK].set(w_gate2).astype(BF16)
    return dict(wbig=wbig, wsmall=wsmall, wg2p=wg2p,
                wout_a=w_out[:hd].astype(BF16), wout_g=w_out[hd:].astype(BF16),
                w_up=w_up.astype(BF16), w_down=w_down.astype(BF16))


def _layer(x, mod, past, past_k, past_v, past_ik, gla_s0, conv_state, lw, pw, *, bb, tb, tq, tk, tg, tf):
    (g_mix, g_ffn, g_q, g_k, b_gate2, g_gla, w_conv, b_conv) = lw
    b, t, d = x.shape
    hd = N_ATTN_HEADS * ATTN_HEAD_DIM
    pos = past + jnp.arange(t, dtype=jnp.int32)
    ropea = _rope_table(pos, ROPE_DIM // 2, LANES, LANES)
    ropei = _rope_table(pos, IDX_ROPE_DIM // 2, IDX_DIM, LANES)
    ropes = _rope_table(pos, IDX_ROPE_DIM // 2, LANES, IDX_DIM)

    (q_bf, k, v, iq, small, lg, gq, gk, gv, gr) = _inproj(
        x, mod, g_mix, pw["wbig"], pw["wsmall"], g_q, g_k, ropea, ropei, ropes, pw["wg2p"], b_gate2, bb, tb)
    ik = small[:, :, :IDX_DIM]
    iw = small[:, :, IDX_DIM:IDX_DIM + N_IDX_HEADS]

    k_bf, v_bf, ik_bf = k.astype(BF16), v.astype(BF16), ik.astype(BF16)
    if past_k is not None:
        k_bf = jnp.concatenate([past_k.reshape(b, past, hd).astype(BF16), k_bf], axis=1)
        v_bf = jnp.concatenate([past_v.reshape(b, past, hd).astype(BF16), v_bf], axis=1)
        ik_bf = jnp.concatenate([past_ik.astype(BF16), ik_bf], axis=1)
    seq_len = past + t
    lp = -(-seq_len // tk) * tk
    if lp != seq_len:
        padw = ((0, 0), (0, lp - seq_len), (0, 0))
        k_bf, v_bf, ik_bf = jnp.pad(k_bf, padw), jnp.pad(v_bf, padw), jnp.pad(ik_bf, padw)
    tpad = -(-t // tq) * tq
    qpad = lambda a: jnp.pad(a, ((0, 0), (0, tpad - t), (0, 0))) if tpad != t else a
    iq_t = qpad(iq).reshape(b, tpad, N_IDX_HEADS, IDX_DIM).transpose(0, 2, 3, 1)
    iw_t = qpad(iw).transpose(0, 2, 1)
    q_t = qpad(q_bf).reshape(b, tpad, N_ATTN_HEADS, ATTN_HEAD_DIM).transpose(0, 2, 3, 1)
    v_t = v_bf.transpose(0, 2, 1)
    topk = min(TOPK_MAX, seq_len // 4)
    o_a = _dsa(iq_t, iw_t, ik_bf, q_t, k_bf, v_t, past=past, seq_len=seq_len, topk=topk, tq=tq, tk=tk)
    o_a = o_a[:, :t]

    s0t = jnp.swapaxes(gla_s0, -1, -2)
    o_g, s_t = _gla(gq, gk, gv, lg, gr, g_gla, s0t, tg)
    s_new = jnp.swapaxes(s_t, -1, -2)

    x1, h2 = _outproj(o_a, o_g, pw["wout_a"], pw["wout_g"], x, mod, g_ffn, bb, tb)
    y, nb_a, nb_b = _ffn(h2, pw["w_up"], w_conv, b_conv, conv_state, pw["w_down"], x1, mod, bb, tb, tf)
    new_buf = jnp.concatenate([nb_a[:, -1], nb_b[:, -1]], axis=-1)
    return (y, k.reshape(b, t, N_ATTN_HEADS, ATTN_HEAD_DIM), v.reshape(b, t, N_ATTN_HEADS, ATTN_HEAD_DIM),
            ik, s_new, new_buf)


def _tiles(b, t, seq_len):
    if t >= 512:
        bb, tb = 1, 512
    else:
        bb, tb = min(b, 512 // t), t
    tq = 256 if t % 256 == 0 else LANES
    tk = 256 if seq_len % 256 == 0 else LANES
    tg = min(t, 256)
    return dict(bb=bb, tb=tb, tq=tq, tk=tk, tg=tg, tf=512)


def kernel(x_prompt, x_sample, c_prompt, c_sample, cache_k, cache_v, cache_idx_k, state_gla, state_ffn_conv,
           w_ada, b_ada, g_mix, g_ffn, w_in, g_q, g_k, w_gate2, b_gate2, g_gla, w_out, w_up, w_conv, b_conv,
           w_down):
    bp, s, d = x_prompt.shape
    bs, ts, _ = x_sample.shape
    depth = w_ada.shape[0]
    past = cache_k.shape[2]
    dff = w_down.shape[1]
    dt = x_prompt.dtype

    y_p, y_s = x_prompt, x_sample
    outs = [[] for _ in range(10)]
    for l in range(depth):
        n_c = bp + bs
        rows = -(-n_c // SUBLANES) * SUBLANES
        c_all = jnp.concatenate([c_prompt, c_sample, jnp.zeros((rows - n_c, d), dt)], axis=0)
        mod = _ada(c_all, w_ada[l], b_ada[l])
        mod_p = mod[:bp].reshape(bp, 1, 6 * d)
        mod_s = mod[bp:n_c].reshape(bs, 1, 6 * d)
        pw = _prep_weights(w_in[l], w_gate2[l], w_out[l], w_up[l], w_down[l])
        lw = (g_mix[l], g_ffn[l], g_q[l], g_k[l], b_gate2[l], g_gla[l], w_conv[l], b_conv[l])

        gla0 = jnp.zeros((bp, GLA_HEADS, GLA_DK, GLA_DV), dt)
        conv0 = jnp.zeros((bp, CONV_W - 1, 2 * dff), dt)
        y_p, k1, v1, ik1, s1, c1 = _layer(y_p, mod_p, 0, None, None, None, gla0, conv0, lw, pw,
                                          **_tiles(bp, s, s))
        y_s, k2, v2, ik2, s2, c2 = _layer(y_s, mod_s, past, cache_k[l], cache_v[l], cache_idx_k[l],
                                          state_gla[l], state_ffn_conv[l], lw, pw,
                                          **_tiles(bs, ts, past + ts))
        for lst, val in zip(outs, (k1, v1, ik1, s1, c1, k2, v2, ik2, s2, c2)):
            lst.append(val)

    return (y_p, y_s) + tuple(jnp.stack(o) for o in outs)
```

```python
import functools
import math

import jax
import jax.numpy as jnp
import numpy as np
from jax import lax
from jax.experimental import pallas as pl
from jax.experimental.pallas import tpu as pltpu

F32 = jnp.float32
BF16 = jnp.bfloat16

CHUNK = 64
N_ATTN_HEADS = 8
ATTN_HEAD_DIM = 128
ROPE_DIM = 32
ROPE_THETA = 500000.0
N_IDX_HEADS = 16
IDX_DIM = 64
IDX_ROPE_DIM = 16
TOPK_MAX = 256
GLA_HEADS = 4
GLA_DK = 128
GLA_DV = 256
GLA_GATE_RANK = 16
GLA_GATE_TEMP = 16.0
CONV_W = 3
EPS = 1e-6

LANES = 128
SUBLANES = 8
VMEM_LIMIT_BYTES = 56 * 1024 * 1024

SCORE_LOOKAHEAD = 4
SCORE_SLOTS = 8
FINE_STEPS = 17

MASK_NEG = -1e30
F32_LOWEST = -3.4028235e38


def _cparams(sem):
    return pltpu.CompilerParams(dimension_semantics=sem, vmem_limit_bytes=VMEM_LIMIT_BYTES)


def _sigmoid(x):
    return 1.0 / (1.0 + jnp.exp(-x))


def _silu(x):
    return x * _sigmoid(x)


def _ada_kernel(c_ref, w_ref, b_ref, o_ref):
    s = _silu(c_ref[...]).astype(BF16)
    o_ref[...] = jnp.dot(s, w_ref[...].astype(BF16), preferred_element_type=F32) + b_ref[...]


def _ada(c_all, w_ada, b_ada):
    rows, d = c_all.shape
    n = w_ada.shape[1]
    tn = 1536 if n % 1536 == 0 else n
    return pl.pallas_call(
        _ada_kernel,
        out_shape=jax.ShapeDtypeStruct((rows, n), F32),
        grid=(n // tn,),
        in_specs=[
            pl.BlockSpec((rows, d), lambda j: (0, 0)),
            pl.BlockSpec((d, tn), lambda j: (0, j)),
            pl.BlockSpec((1, tn), lambda j: (0, j)),
        ],
        out_specs=pl.BlockSpec((rows, tn), lambda j: (0, j)),
        compiler_params=_cparams(("arbitrary",)),
        name="ada_mod",
    )(c_all, w_ada, b_ada.reshape(1, n))


def _rope_table(pos, half, period, active_lanes):
    inv = ROPE_THETA ** (-jnp.arange(half, dtype=F32) / half)
    lane = np.arange(LANES)
    within = lane % period
    first = jnp.asarray((within < half) & (lane < active_lanes))
    second = jnp.asarray((within >= half) & (within < 2 * half) & (lane < active_lanes))
    ang = pos.astype(F32)[:, None] * inv[jnp.asarray(within % half)][None, :]
    cos, sin = jnp.cos(ang), jnp.sin(ang)
    c = jnp.where(first | second, cos, 1.0)
    s1 = jnp.where(first, -sin, 0.0)
    s2 = jnp.where(second, sin, 0.0)
    return jnp.stack([c, s1, s2]).astype(F32)


def _rope_apply(v, tab, half):
    return (v * tab[0] + pltpu.roll(v, LANES - half, 1) * tab[1] + pltpu.roll(v, half, 1) * tab[2])


def _tile_rows(tab_ref, bb):
    t = tab_ref[...]
    if bb == 1:
        return t
    tb = t.shape[1]
    return jnp.broadcast_to(t[:, None], (3, bb, tb, LANES)).reshape(3, bb * tb, LANES)


def _inproj_kernel(x_ref, mod_ref, gmix_ref, wbig_ref, wsmall_ref, gq_ref, gk_ref,
                   ropea_ref, ropei_ref, ropes_ref, wg2_ref, bg2_ref,
                   q_out, k_out, v_out, iq_out, small_out, lg_out, gq_out, gk_out, gv_out, gr_out,
                   h_scr):
    j = pl.program_id(2)
    bb, tb, d = x_ref.shape
    rows = bb * tb

    @pl.when(j == 0)
    def _():
        x = x_ref[...]
        ms = jnp.mean(x * x, axis=-1, keepdims=True)
        y = x * lax.rsqrt(ms + EPS) * gmix_ref[...]
        mod = mod_ref[...]
        h = y * (1.0 + mod[:, :, d:2 * d]) + mod[:, :, 0:d]
        hb = h.astype(BF16).reshape(rows, d)
        h_scr[...] = hb
        small = jnp.dot(hb, wsmall_ref[...], preferred_element_type=F32)
        small = _rope_apply(small, _tile_rows(ropes_ref, bb), IDX_ROPE_DIM // 2)
        small_out[...] = small.reshape(bb, tb, LANES)
        pre = jnp.dot(small.astype(BF16), wg2_ref[...], preferred_element_type=F32) + bg2_ref[...]
        lsig = jnp.minimum(pre, 0.0) - jnp.log(1.0 + jnp.exp(-jnp.abs(pre)))
        lg_out[...] = (lsig / GLA_GATE_TEMP).reshape(bb, tb, lg_out.shape[-1])

    y = jnp.dot(h_scr[...], wbig_ref[...], preferred_element_type=F32)

    def headnorm_rope(g_ref, scale):
        tab = _tile_rows(ropea_ref, bb)
        outs = []
        for hh in range(N_ATTN_HEADS):
            yh = y[:, hh * LANES:(hh + 1) * LANES]
            n = yh * lax.rsqrt(jnp.mean(yh * yh, axis=-1, keepdims=True) + EPS) * g_ref[...]
            r = _rope_apply(n, tab, ROPE_DIM // 2)
            outs.append(r * scale if scale != 1.0 else r)
        return outs

    @pl.when(j == 0)
    def _():
        for hh, r in enumerate(headnorm_rope(gq_ref, ATTN_HEAD_DIM ** -0.5 * math.log2(math.e))):
            q_out[:, :, hh * LANES:(hh + 1) * LANES] = r.astype(BF16).reshape(bb, tb, LANES)

    @pl.when(j == 1)
    def _():
        for hh, r in enumerate(headnorm_rope(gk_ref, 1.0)):
            k_out[:, :, hh * LANES:(hh + 1) * LANES] = r.reshape(bb, tb, LANES)

    @pl.when(j == 2)
    def _():
        v_out[...] = y.reshape(bb, tb, y.shape[-1])

    @pl.when(j == 3)
    def _():
        tab = _tile_rows(ropei_ref, bb)
        for c in range(y.shape[-1] // LANES):
            r = _rope_apply(y[:, c * LANES:(c + 1) * LANES], tab, IDX_ROPE_DIM // 2)
            iq_out[:, :, c * LANES:(c + 1) * LANES] = r.astype(BF16).reshape(bb, tb, LANES)

    @pl.when(j == 4)
    def _():
        half = y.shape[-1] // 2
        gq_out[...] = (y[:, :half] * GLA_DK ** -0.5).reshape(bb, tb, half)
        gk_out[...] = y[:, half:].reshape(bb, tb, half)

    @pl.when(j == 5)
    def _():
        gv_out[...] = y.astype(BF16).reshape(bb, tb, y.shape[-1])

    @pl.when(j == 6)
    def _():
        gr_out[...] = y.reshape(bb, tb, y.shape[-1])


def _inproj(x, mod, g_mix, wbig, wsmall, g_q, g_k, ropea, ropei, ropes, wg2p, bg2, bb, tb):
    b, t, d = x.shape
    gw = 1024
    ngroups = wbig.shape[1] // gw
    glak = GLA_HEADS * GLA_DK
    rows = bb * tb
    tok = lambda w, dt: jax.ShapeDtypeStruct((b, t, w), dt)
    tokspec = lambda w: pl.BlockSpec((bb, tb, w), lambda i, m, j: (i, m, 0))
    const = lambda shp: pl.BlockSpec(shp, lambda i, m, j: tuple(0 for _ in shp))
    tabspec = pl.BlockSpec((3, tb, LANES), lambda i, m, j: (0, m, 0))
    out_shapes = (tok(1024, BF16), tok(1024, F32), tok(1024, F32), tok(1024, BF16), tok(LANES, F32),
                  tok(glak, F32), tok(glak, F32), tok(glak, F32), tok(1024, BF16), tok(1024, F32))
    out_specs = tuple(tokspec(s.shape[-1]) for s in out_shapes)
    return pl.pallas_call(
        _inproj_kernel,
        out_shape=out_shapes,
        grid=(b // bb, t // tb, ngroups),
        in_specs=[
            tokspec(d),
            pl.BlockSpec((bb, 1, mod.shape[-1]), lambda i, m, j: (i, 0, 0)),
            const((1, 1, d)),
            pl.BlockSpec((d, gw), lambda i, m, j: (0, j)),
            const((d, LANES)),
            const((1, LANES)), const((1, LANES)),
            tabspec, tabspec, tabspec,
            const((LANES, glak)), const((1, glak)),
        ],
        out_specs=out_specs,
        scratch_shapes=[pltpu.VMEM((rows, d), BF16)],
        compiler_params=_cparams(("arbitrary", "arbitrary", "arbitrary")),
        name="norm_inproj",
    )(x, mod, g_mix.reshape(1, 1, d), wbig, wsmall, g_q.reshape(1, LANES), g_k.reshape(1, LANES),
      ropea, ropei, ropes, wg2p, bg2.reshape(1, glak))


def _dsa_kernel(iqt_ref, iwt_ref, ik_ref, qt_ref, k_hbm, vt_hbm, o_ref,
                u_scr, ub_scr, kbuf, vbuf, sem, m_scr, l_scr, acc_scr, s_scr, bias_scr,
                *, past, seq_len, topk, tq, tk):
    b = pl.program_id(0)
    i = pl.program_id(1)
    nkt = u_scr.shape[0]

    last_chunk = (past + (i + 1) * tq - 1) // CHUNK
    n_adm = jnp.minimum(seq_len, (last_chunk + 1) * CHUNK)
    nk = jnp.minimum((n_adm + tk - 1) // tk, nkt)

    def k_copy(j):
        off = pl.multiple_of(j * tk, tk)
        return pltpu.make_async_copy(k_hbm.at[b, pl.ds(off, tk)], kbuf.at[j % 2], sem.at[0, j % 2])

    def v_copy(j):
        off = pl.multiple_of(j * tk, tk)
        return pltpu.make_async_copy(vt_hbm.at[b, :, pl.ds(off, tk)], vbuf.at[j % 2], sem.at[1, j % 2])

    k_copy(0).start()
    v_copy(0).start()

    q_chunk = (past + i * tq + lax.broadcasted_iota(jnp.int32, (tk, tq), 1)) // CHUNK
    key_iota = lax.broadcasted_iota(jnp.int32, (tk, tq), 0)
    idx_scale = (IDX_DIM * N_IDX_HEADS) ** -0.5

    def score_tile(j, carry):
        ik_t = ik_ref[0, pl.ds(pl.multiple_of(j * tk, tk), tk), :]
        acc = jnp.zeros((tk, tq), F32)
        for hh in range(N_IDX_HEADS):
            logits = jnp.dot(ik_t, iqt_ref[0, hh], preferred_element_type=F32)
            acc = acc + iwt_ref[0, hh:hh + 1, :] * jnp.maximum(logits, 0.0)
        kpos = j * tk + key_iota
        adm = (kpos // CHUNK <= q_chunk) & (kpos < seq_len)
        u = jnp.where(adm, acc * idx_scale, -jnp.inf)
        u_scr[j] = u
        ub_scr[j] = u.astype(ub_scr.dtype)
        return carry

    lax.fori_loop(0, nk, score_tile, 0)

    group = 2 if nkt % 2 == 0 else 1
    n_groups = (nk + group - 1) // group

    @pl.when(n_groups * group > nk)
    def _():
        u_scr[nk] = jnp.full((tk, tq), -jnp.inf, F32)
        ub_scr[nk] = jnp.full((tk, tq), -jnp.inf, ub_scr.dtype)

    def count_ge(scr, thr_row):
        rows = SUBLANES * (4 // jnp.dtype(scr.dtype).itemsize)
        thr_b = jnp.broadcast_to(thr_row.astype(scr.dtype), (rows, tq))
        one, zero = jnp.ones((), scr.dtype), jnp.zeros((), scr.dtype)

        def body(g, cnt):
            for r in range(group):
                x = scr[g * group + r].reshape(tk // rows, rows, tq)
                hit = jnp.where(x >= thr_b[None], one, zero)
                parts = [hit[n] for n in range(tk // rows)]
                while len(parts) > 1:
                    parts = [parts[n] + parts[n + 1] for n in range(0, len(parts), 2)]
                cnt = cnt + parts[0].astype(F32)
            return cnt

        cnt = lax.fori_loop(0, n_groups, body, jnp.zeros((rows, tq), F32))
        return jnp.sum(cnt, axis=0, keepdims=True)

    int_min = jnp.int32(-2 ** 31)
    kf = float(topk)

    def coarse_value(t_biased):
        mono = t_biased ^ int_min
        bits = jnp.where(mono >= 0, mono, mono ^ jnp.int32(0x7FFF0000))
        return lax.bitcast_convert_type(bits, F32)

    def coarse_step(it, t_biased):
        cand = t_biased | lax.shift_left(jnp.int32(1), jnp.int32(31) - it)
        return jnp.where(count_ge(ub_scr, coarse_value(cand)) >= kf, cand, t_biased)

    t16 = lax.fori_loop(0, 16, coarse_step, jnp.zeros((1, tq), jnp.int32))
    few = (t16 >= 0) & (t16 <= jnp.int32(0x007F0000))

    def to_mono(bits):
        return jnp.where(bits >= 0, bits, bits ^ jnp.int32(0x7FFFFFFF))

    c_mono = to_mono(lax.bitcast_convert_type(coarse_value(t16), jnp.int32))
    lo0 = c_mono - jnp.int32(0x8000)
    hi0 = c_mono + jnp.int32(0x10000)
    hi0 = jnp.where(few, lo0, hi0)

    def fine_cond(state):
        it, _, _, pending = state
        return (it < FINE_STEPS) & (pending > 0)

    def fine_step(state):
        it, lo, hi, _ = state
        mid = lo + lax.shift_right_arithmetic(hi - lo + 1, jnp.int32(1))
        total = count_ge(u_scr, lax.bitcast_convert_type(to_mono(mid), F32))
        ok = total >= kf
        lo = jnp.where(ok, mid, lo)
        hi = jnp.where(total == kf, mid, jnp.where(ok, hi, mid - 1))
        return it + 1, lo, hi, jnp.max(jnp.where(hi > lo, 1, 0))

    _, lo_fin, _, _ = lax.while_loop(fine_cond, fine_step,
                                     (jnp.int32(0), lo0, hi0, jnp.max(jnp.where(hi0 > lo0, 1, 0))))
    thr_fin = lax.bitcast_convert_type(to_mono(lo_fin), F32)
    thr = jnp.where(few, F32_LOWEST, jnp.maximum(thr_fin, F32_LOWEST))

    m_scr[...] = jnp.full(m_scr.shape, MASK_NEG, F32)
    l_scr[...] = jnp.zeros(l_scr.shape, F32)
    acc_scr[...] = jnp.zeros(acc_scr.shape, F32)

    def select_bias(j):
        bias_scr[...] = jnp.where(u_scr[j] >= thr, 0.0, MASK_NEG)

    n_slots = s_scr.shape[0]

    def scores_head(j, hh):
        hs = slice(hh * LANES, (hh + 1) * LANES)
        s_scr[hh % n_slots] = (jnp.dot(kbuf[j % 2, :, hs], qt_ref[0, hh], preferred_element_type=F32)
                               + bias_scr[...])

    def attend_head(j, hh):
        hs = slice(hh * LANES, (hh + 1) * LANES)
        s = s_scr[hh % n_slots]
        m_prev = m_scr[hh:hh + 1, :]
        m_new = jnp.maximum(m_prev, jnp.max(s, axis=0, keepdims=True))
        alpha = jnp.exp2(m_prev - m_new)
        p = jnp.exp2(s - m_new)
        l_scr[hh:hh + 1, :] = alpha * l_scr[hh:hh + 1, :] + jnp.sum(p, axis=0, keepdims=True)
        acc_scr[hs, :] = alpha * acc_scr[hs, :] + jnp.dot(vbuf[j % 2, hs, :], p.astype(BF16),
                                                          preferred_element_type=F32)
        m_scr[hh:hh + 1, :] = m_new

    def tile_steps(j, has_next):
        v_copy(j).wait()
        for hh in range(N_ATTN_HEADS):
            ahead = hh + SCORE_LOOKAHEAD
            if ahead < N_ATTN_HEADS:
                scores_head(j, ahead)
            elif has_next:
                if ahead == N_ATTN_HEADS:
                    @pl.when(j + 2 < nk)
                    def _():
                        k_copy(j + 2).start()

                    k_copy(j + 1).wait()
                    select_bias(j + 1)
                scores_head(j + 1, ahead - N_ATTN_HEADS)
            attend_head(j, hh)

    @pl.when(nk > 1)
    def _():
        k_copy(1).start()

    k_copy(0).wait()
    select_bias(0)
    for hh in range(SCORE_LOOKAHEAD):
        scores_head(0, hh)

    def attend_tile(j, carry):
        v_copy(j + 1).start()
        tile_steps(j, True)
        return carry

    lax.fori_loop(0, nk - 1, attend_tile, 0)
    tile_steps(nk - 1, False)

    for hh in range(N_ATTN_HEADS):
        hs = slice(hh * LANES, (hh + 1) * LANES)
        o_t = acc_scr[hs, :] / l_scr[hh:hh + 1, :]
        o_ref[0, :, hs] = o_t.T.astype(o_ref.dtype)


def _dsa(iq_t, iw_t, ik_bf, q_t, k_bf, v_t, *, past, seq_len, topk, tq, tk):
    b, _, _, t = iq_t.shape
    lp = ik_bf.shape[1]
    nkt = lp // tk
    hd = N_ATTN_HEADS * ATTN_HEAD_DIM
    kern = functools.partial(_dsa_kernel, past=past, seq_len=seq_len, topk=topk, tq=tq, tk=tk)
    return pl.pallas_call(
        kern,
        out_shape=jax.ShapeDtypeStruct((b, t, hd), BF16),
        grid=(b, t // tq),
        in_specs=[
            pl.BlockSpec((1, N_IDX_HEADS, IDX_DIM, tq), lambda bi, i: (bi, 0, 0, i)),
            pl.BlockSpec((1, N_IDX_HEADS, tq), lambda bi, i: (bi, 0, i)),
            pl.BlockSpec((1, lp, IDX_DIM), lambda bi, i: (bi, 0, 0)),
            pl.BlockSpec((1, N_ATTN_HEADS, ATTN_HEAD_DIM, tq), lambda bi, i: (bi, 0, 0, i)),
            pl.BlockSpec(memory_space=pl.ANY),
            pl.BlockSpec(memory_space=pl.ANY),
        ],
        out_specs=pl.BlockSpec((1, tq, hd), lambda bi, i: (bi, i, 0)),
        scratch_shapes=[
            pltpu.VMEM((nkt, tk, tq), F32),
            pltpu.VMEM((nkt, tk, tq), jnp.bfloat16),
            pltpu.VMEM((2, tk, hd), BF16),
            pltpu.VMEM((2, hd, tk), BF16),
            pltpu.SemaphoreType.DMA((2, 2)),
            pltpu.VMEM((N_ATTN_HEADS, tq), F32),
            pltpu.VMEM((N_ATTN_HEADS, tq), F32),
            pltpu.VMEM((hd, tq), F32),
            pltpu.VMEM((SCORE_SLOTS, tk, tq), F32),
            pltpu.VMEM((tk, tq), F32),
        ],
        compiler_params=_cparams(("arbitrary", "arbitrary")),
        name="dsa_attend",
    )(iq_t, iw_t, ik_bf, q_t, k_bf, v_t)


def _gla_kernel(q_ref, k_ref, v_ref, lg_ref, gr_ref, g_ref, s0_ref, o_ref, sT_out, st_scr, *, n_chunks):
    tstep = pl.program_id(2)

    @pl.when(tstep == 0)
    def _():
        st_scr[...] = s0_ref[0, 0]

    row = lax.broadcasted_iota(jnp.int32, (CHUNK, GLA_DK), 0)
    tri = (lax.broadcasted_iota(jnp.int32, (CHUNK, CHUNK), 0)
           >= lax.broadcasted_iota(jnp.int32, (CHUNK, CHUNK), 1))

    for c in range(n_chunks):
        rs = slice(c * CHUNK, (c + 1) * CHUNK)
        bcum = lg_ref[0, rs, :]
        shift = 1
        while shift < CHUNK:
            bcum = bcum + jnp.where(row >= shift, pltpu.roll(bcum, shift, 0), 0.0)
            shift *= 2
        bl = bcum[CHUNK - 1:CHUNK, :]
        q = q_ref[0, rs, :]
        k = k_ref[0, rs, :]
        v = v_ref[0, rs, :]
        qe = (q * jnp.exp(bcum)).astype(BF16)
        ke = (k * jnp.exp(-bcum)).astype(BF16)
        kd = (k * jnp.exp(bl - bcum)).astype(BF16)
        st = st_scr[...]
        inter = lax.dot_general(qe, st.astype(BF16), (((1,), (1,)), ((), ())), preferred_element_type=F32)
        a = lax.dot_general(qe, ke, (((1,), (1,)), ((), ())), preferred_element_type=F32)
        a = jnp.where(tri, a, 0.0)
        o = inter + jnp.dot(a.astype(BF16), v, preferred_element_type=F32)
        st_scr[...] = st * jnp.exp(bl) + lax.dot_general(v, kd, (((0,), (0,)), ((), ())),
                                                          preferred_element_type=F32)
        n = o * lax.rsqrt(jnp.mean(o * o, axis=-1, keepdims=True) + EPS) * g_ref[...]
        o_ref[0, rs, :] = (n * _silu(gr_ref[0, rs, :])).astype(o_ref.dtype)

    @pl.when(tstep == pl.num_programs(2) - 1)
    def _():
        sT_out[0, 0] = st_scr[...]


def _gla(gq, gk, gv, lg, gr, g_gla, s0t, tg):
    b, t, _ = gq.shape
    kern = functools.partial(_gla_kernel, n_chunks=tg // CHUNK)
    tok = lambda w: pl.BlockSpec((1, tg, w), lambda bi, h, m: (bi, m, h))
    st_spec = pl.BlockSpec((1, 1, GLA_DV, GLA_DK), lambda bi, h, m: (bi, h, 0, 0))
    return pl.pallas_call(
        kern,
        out_shape=(jax.ShapeDtypeStruct((b, t, GLA_HEADS * GLA_DV), BF16),
                   jax.ShapeDtypeStruct((b, GLA_HEADS, GLA_DV, GLA_DK), F32)),
        grid=(b, GLA_HEADS, t // tg),
        in_specs=[tok(GLA_DK), tok(GLA_DK), tok(GLA_DV), tok(GLA_DK), tok(GLA_DV),
                  pl.BlockSpec((1, GLA_DV), lambda bi, h, m: (0, 0)), st_spec],
        out_specs=(tok(GLA_DV), st_spec),
        scratch_shapes=[pltpu.VMEM((GLA_DV, GLA_DK), F32)],
        compiler_params=_cparams(("arbitrary", "arbitrary", "arbitrary")),
        name="gla",
    )(gq, gk, gv, lg, gr, g_gla.reshape(1, GLA_DV), s0t)


def _outproj_kernel(oa_ref, og_ref, wa_ref, wg_ref, x_ref, mod_ref, gffn_ref, x1_out, h2_out):
    bb, tb, d = x_ref.shape
    rows = bb * tb
    oa = oa_ref[...].reshape(rows, oa_ref.shape[-1])
    og = og_ref[...].reshape(rows, og_ref.shape[-1])
    mix = (jnp.dot(oa, wa_ref[...], preferred_element_type=F32)
           + jnp.dot(og, wg_ref[...], preferred_element_type=F32)).reshape(bb, tb, d)
    mod = mod_ref[...]
    x1 = x_ref[...] + mod[:, :, 2 * d:3 * d] * mix
    x1_out[...] = x1
    y = x1 * lax.rsqrt(jnp.mean(x1 * x1, axis=-1, keepdims=True) + EPS) * gffn_ref[...]
    h2_out[...] = (y * (1.0 + mod[:, :, 4 * d:5 * d]) + mod[:, :, 3 * d:4 * d]).astype(BF16)


def _outproj(o_a, o_g, wout_a, wout_g, x, mod, g_ffn, bb, tb):
    b, t, d = x.shape
    tokspec = lambda w: pl.BlockSpec((bb, tb, w), lambda i, m: (i, m, 0))
    const = lambda shp: pl.BlockSpec(shp, lambda i, m: tuple(0 for _ in shp))
    return pl.pallas_call(
        _outproj_kernel,
        out_shape=(jax.ShapeDtypeStruct((b, t, d), F32), jax.ShapeDtypeStruct((b, t, d), BF16)),
        grid=(b // bb, t // tb),
        in_specs=[tokspec(o_a.shape[-1]), tokspec(o_g.shape[-1]), const(wout_a.shape), const(wout_g.shape),
                  tokspec(d), pl.BlockSpec((bb, 1, mod.shape[-1]), lambda i, m: (i, 0, 0)), const((1, 1, d))],
        out_specs=(tokspec(d), tokspec(d)),
        compiler_params=_cparams(("arbitrary", "arbitrary")),
        name="outproj_norm",
    )(o_a, o_g, wout_a, wout_g, x, mod, g_ffn.reshape(1, 1, d))


def _ffn_kernel(h_ref, halo_ref, wa_ref, wb_ref, wca_ref, wcb_ref, bca_ref, bcb_ref, sa_ref, sb_ref,
                wd_ref, x1_ref, mod_ref, y_out, nba_out, nbb_out, acc_scr, hext_scr, act_scr):
    m = pl.program_id(1)
    f = pl.program_id(2)
    nf = pl.num_programs(2) - 1
    bb, tb, d = h_ref.shape
    rows = bb * tb
    tf = wa_ref.shape[1]
    halo = halo_ref.shape[1]

    @pl.when(f == 0)
    def _():
        hext_scr[:, 0:halo, :] = halo_ref[...]
        hext_scr[:, halo:, :] = h_ref[...]
        acc_scr[...] = jnp.zeros(acc_scr.shape, F32)
        act_scr[1] = jnp.zeros(act_scr.shape[1:], act_scr.dtype)

    def conv_branch(hext, tpos, w_ref, wc_ref, bc_ref, s_ref, nb_out):
        u_ext = jnp.dot(hext, w_ref[...], preferred_element_type=F32).reshape(bb, tb + halo, tf)
        u = u_ext[:, halo:, :].reshape(rows, tf)
        prev = jnp.where(m == 0, s_ref[...], u_ext[:, halo - (CONV_W - 1):halo, :])
        bcast = lambda r: jnp.broadcast_to(prev[:, r:r + 1, :], (bb, tb, tf)).reshape(rows, tf)
        p0, p1 = bcast(0), bcast(1)
        u1 = jnp.where(tpos == 0, p1, pltpu.roll(u, 1, 0))
        u2 = jnp.where(tpos == 0, p0, jnp.where(tpos == 1, p1, pltpu.roll(u, 2, 0)))
        wc = wc_ref[...]
        nb_out[:, 0] = u.reshape(bb, tb, tf)[:, tb - (CONV_W - 1):, :]
        return bc_ref[...] + wc[0:1] * u2 + wc[1:2] * u1 + wc[2:3] * u

    @pl.when(f < nf)
    def _():
        act_prev = act_scr[(f + 1) % 2]
        hext = hext_scr[...].reshape(bb * (tb + halo), d)
        tpos = lax.broadcasted_iota(jnp.int32, (rows, tf), 0) % tb
        ua = conv_branch(hext, tpos, wa_ref, wca_ref, bca_ref, sa_ref, nba_out)
        ub = conv_branch(hext, tpos, wb_ref, wcb_ref, bcb_ref, sb_ref, nbb_out)
        acc_scr[...] += jnp.dot(act_prev, wd_ref[...], preferred_element_type=F32)
        act_scr[f % 2] = (_silu(ua) * ub).astype(act_scr.dtype)

    @pl.when(f == nf)
    def _():
        acc = acc_scr[...] + jnp.dot(act_scr[(f + 1) % 2], wd_ref[...], preferred_element_type=F32)
        mod = mod_ref[...]
        y_out[...] = x1_ref[...] + mod[:, :, 5 * d:6 * d] * acc.reshape(bb, tb, d)


def _ffn(h2, w_up, w_conv, b_conv, conv_state, w_down, x1, mod, bb, tb, tf):
    b, t, d = x1.shape
    dff = w_down.shape[0]
    nf = dff // tf
    halo = 2 * SUBLANES
    hb = tb // halo
    tokspec = lambda w: pl.BlockSpec((bb, tb, w), lambda i, m, f: (i, m, 0))
    up = lambda f: jnp.minimum(f, nf - 1)
    a_col = lambda r: pl.BlockSpec((r, tf), lambda i, m, f: (0, up(f)))
    b_col = lambda r: pl.BlockSpec((r, tf), lambda i, m, f: (0, nf + up(f)))
    st_a = pl.BlockSpec((bb, CONV_W - 1, tf), lambda i, m, f: (i, 0, up(f)))
    st_b = pl.BlockSpec((bb, CONV_W - 1, tf), lambda i, m, f: (i, 0, nf + up(f)))
    nb_spec = pl.BlockSpec((bb, 1, CONV_W - 1, tf), lambda i, m, f: (i, m, 0, up(f)))
    bc2 = b_conv.reshape(1, 2 * dff)
    return pl.pallas_call(
        _ffn_kernel,
        out_shape=(jax.ShapeDtypeStruct((b, t, d), F32),
                   jax.ShapeDtypeStruct((b, t // tb, CONV_W - 1, dff), F32),
                   jax.ShapeDtypeStruct((b, t // tb, CONV_W - 1, dff), F32)),
        grid=(b // bb, t // tb, nf + 1),
        in_specs=[
            tokspec(d),
            pl.BlockSpec((bb, halo, d), lambda i, m, f: (i, jnp.maximum(m * hb - 1, 0), 0)),
            a_col(d), b_col(d), a_col(CONV_W), b_col(CONV_W), a_col(1), b_col(1), st_a, st_b,
            pl.BlockSpec((tf, d), lambda i, m, f: (jnp.maximum(f - 1, 0), 0)),
            tokspec(d),
            pl.BlockSpec((bb, 1, mod.shape[-1]), lambda i, m, f: (i, 0, 0)),
        ],
        out_specs=(tokspec(d), nb_spec, nb_spec),
        scratch_shapes=[pltpu.VMEM((bb * tb, d), F32), pltpu.VMEM((bb, tb + halo, d), BF16),
                        pltpu.VMEM((2, bb * tb, tf), BF16)],
        compiler_params=_cparams(("arbitrary", "arbitrary", "arbitrary")),
        name="conv_ffn",
    )(h2, h2, w_up, w_up, w_conv, w_conv, bc2, bc2, conv_state, conv_state, w_down, x1, mod)


def _prep_weights(w_in, w_gate2, w_out, w_up, w_down):
    hd = N_ATTN_HEADS * ATTN_HEAD_DIM
    sizes = (hd, hd, hd, N_IDX_HEADS * IDX_DIM, IDX_DIM, N_IDX_HEADS,
             GLA_HEADS * GLA_DK, GLA_HEADS * GLA_DK, GLA_HEADS * GLA_DV, GLA_HEADS * GLA_DV, GLA_GATE_RANK)
    offs = np.concatenate([[0], np.cumsum(sizes)])
    col = lambda n: w_in[:, int(offs[n]):int(offs[n + 1])]
    wbig = jnp.concatenate([col(0), col(1), col(2), col(3), col(6), col(7), col(8), col(9)], axis=1).astype(BF16)
    pad = LANES - (IDX_DIM + N_IDX_HEADS + GLA_GATE_RANK)
    wsmall = jnp.concatenate([col(4), col(5), col(10), jnp.zeros((w_in.shape[0], pad), w_in.dtype)],
                             axis=1).astype(BF16)
    lo = IDX_DIM + N_IDX_HEADS
    wg2p = jnp.zeros((LANES, w_gate2.shape[1]), F32).at[lo:lo + GLA_GATE_RANK].set(w_gate2).astype(BF16)
    return dict(wbig=wbig, wsmall=wsmall, wg2p=wg2p,
                wout_a=w_out[:hd].astype(BF16), wout_g=w_out[hd:].astype(BF16),
                w_up=w_up.astype(BF16), w_down=w_down.astype(BF16))


def _layer(x, mod, past, past_k, past_v, past_ik, gla_s0, conv_state, lw, pw, *, bb, tb, tq, tk, tg, tf):
    (g_mix, g_ffn, g_q, g_k, b_gate2, g_gla, w_conv, b_conv) = lw
    b, t, d = x.shape
    hd = N_ATTN_HEADS * ATTN_HEAD_DIM
    pos = past + jnp.arange(t, dtype=jnp.int32)
    ropea = _rope_table(pos, ROPE_DIM // 2, LANES, LANES)
    ropei = _rope_table(pos, IDX_ROPE_DIM // 2, IDX_DIM, LANES)
    ropes = _rope_table(pos, IDX_ROPE_DIM // 2, LANES, IDX_DIM)

    (q_bf, k, v, iq, small, lg, gq, gk, gv, gr) = _inproj(
        x, mod, g_mix, pw["wbig"], pw["wsmall"], g_q, g_k, ropea, ropei, ropes, pw["wg2p"], b_gate2, bb, tb)
    ik = small[:, :, :IDX_DIM]
    iw = small[:, :, IDX_DIM:IDX_DIM + N_IDX_HEADS]

    k_bf, v_bf, ik_bf = k.astype(BF16), v.astype(BF16), ik.astype(BF16)
    if past_k is not None:
        k_bf = jnp.concatenate([past_k.reshape(b, past, hd).astype(BF16), k_bf], axis=1)
        v_bf = jnp.concatenate([past_v.reshape(b, past, hd).astype(BF16), v_bf], axis=1)
        ik_bf = jnp.concatenate([past_ik.astype(BF16), ik_bf], axis=1)
    seq_len = past + t
    lp = -(-seq_len // tk) * tk
    if lp != seq_len:
        padw = ((0, 0), (0, lp - seq_len), (0, 0))
        k_bf, v_bf, ik_bf = jnp.pad(k_bf, padw), jnp.pad(v_bf, padw), jnp.pad(ik_bf, padw)
    tpad = -(-t // tq) * tq
    qpad = lambda a: jnp.pad(a, ((0, 0), (0, tpad - t), (0, 0))) if tpad != t else a
    iq_t = qpad(iq).reshape(b, tpad, N_IDX_HEADS, IDX_DIM).transpose(0, 2, 3, 1)
    iw_t = qpad(iw).transpose(0, 2, 1)
    q_t = qpad(q_bf).reshape(b, tpad, N_ATTN_HEADS, ATTN_HEAD_DIM).transpose(0, 2, 3, 1)
    v_t = v_bf.transpose(0, 2, 1)
    topk = min(TOPK_MAX, seq_len // 4)
    o_a = _dsa(iq_t, iw_t, ik_bf, q_t, k_bf, v_t, past=past, seq_len=seq_len, topk=topk, tq=tq, tk=tk)
    o_a = o_a[:, :t]

    s0t = jnp.swapaxes(gla_s0, -1, -2)
    o_g, s_t = _gla(gq, gk, gv, lg, gr, g_gla, s0t, tg)
    s_new = jnp.swapaxes(s_t, -1, -2)

    x1, h2 = _outproj(o_a, o_g, pw["wout_a"], pw["wout_g"], x, mod, g_ffn, bb, tb)
    y, nb_a, nb_b = _ffn(h2, pw["w_up"], w_conv, b_conv, conv_state, pw["w_down"], x1, mod, bb, tb, tf)
    new_buf = jnp.concatenate([nb_a[:, -1], nb_b[:, -1]], axis=-1)
    return (y, k.reshape(b, t, N_ATTN_HEADS, ATTN_HEAD_DIM), v.reshape(b, t, N_ATTN_HEADS, ATTN_HEAD_DIM),
            ik, s_new, new_buf)


def _tiles(b, t, seq_len):
    if t >= 512:
        bb, tb = 1, 512
    else:
        bb, tb = min(b, 512 // t), t
    tq = 256 if t % 256 == 0 else LANES
    tk = 256 if seq_len % 256 == 0 else LANES
    tg = min(t, 256)
    return dict(bb=bb, tb=tb, tq=tq, tk=tk, tg=tg, tf=512)


def kernel(x_prompt, x_sample, c_prompt, c_sample, cache_k, cache_v, cache_idx_k, state_gla, state_ffn_conv,
           w_ada, b_ada, g_mix, g_ffn, w_in, g_q, g_k, w_gate2, b_gate2, g_gla, w_out, w_up, w_conv, b_conv,
           w_down):
    bp, s, d = x_prompt.shape
    bs, ts, _ = x_sample.shape
    depth = w_ada.shape[0]
    past = cache_k.shape[2]
    dff = w_down.shape[1]
    dt = x_prompt.dtype

    y_p, y_s = x_prompt, x_sample
    outs = [[] for _ in range(10)]
    for l in range(depth):
        n_c = bp + bs
        rows = -(-n_c // SUBLANES) * SUBLANES
        c_all = jnp.concatenate([c_prompt, c_sample, jnp.zeros((rows - n_c, d), dt)], axis=0)
        mod = _ada(c_all, w_ada[l], b_ada[l])
        mod_p = mod[:bp].reshape(bp, 1, 6 * d)
        mod_s = mod[bp:n_c].reshape(bs, 1, 6 * d)
        pw = _prep_weights(w_in[l], w_gate2[l], w_out[l], w_up[l], w_down[l])
        lw = (g_mix[l], g_ffn[l], g_q[l], g_k[l], b_gate2[l], g_gla[l], w_conv[l], b_conv[l])

        gla0 = jnp.zeros((bp, GLA_HEADS, GLA_DK, GLA_DV), dt)
        conv0 = jnp.zeros((bp, CONV_W - 1, 2 * dff), dt)
        y_p, k1, v1, ik1, s1, c1 = _layer(y_p, mod_p, 0, None, None, None, gla0, conv0, lw, pw,
                                          **_tiles(bp, s, s))
        y_s, k2, v2, ik2, s2, c2 = _layer(y_s, mod_s, past, cache_k[l], cache_v[l], cache_idx_k[l],
                                          state_gla[l], state_ffn_conv[l], lw, pw,
                                          **_tiles(bs, ts, past + ts))
        for lst, val in zip(outs, (k1, v1, ik1, s1, c1, k2, v2, ik2, s2, c2)):
            lst.append(val)

    return (y_p, y_s) + tuple(jnp.stack(o) for o in outs)
```

```python
import functools
import math

import jax
import jax.numpy as jnp
import numpy as np
from jax import lax
from jax.experimental import pallas as pl
from jax.experimental.pallas import tpu as pltpu

F32 = jnp.float32
BF16 = jnp.bfloat16

CHUNK = 64
N_ATTN_HEADS = 8
ATTN_HEAD_DIM = 128
ROPE_DIM = 32
ROPE_THETA = 500000.0
N_IDX_HEADS = 16
IDX_DIM = 64
IDX_ROPE_DIM = 16
TOPK_MAX = 256
GLA_HEADS = 4
GLA_DK = 128
GLA_DV = 256
GLA_GATE_RANK = 16
GLA_GATE_TEMP = 16.0
CONV_W = 3
EPS = 1e-6

LANES = 128
SUBLANES = 8
VMEM_LIMIT_BYTES = 56 * 1024 * 1024

SCORE_LOOKAHEAD = 4
SCORE_SLOTS = 8
FINE_STEPS = 17
PROJ_GROUP = 1024
FFN_TILE = 512

MASK_NEG = -1e30
F32_LOWEST = -3.4028235e38


def _cparams(sem):
    return pltpu.CompilerParams(dimension_semantics=sem, vmem_limit_bytes=VMEM_LIMIT_BYTES)


def _sigmoid(x):
    return 1.0 / (1.0 + jnp.exp(-x))


def _silu(x):
    return x * _sigmoid(x)


def _ada_kernel(c_ref, w_ref, b_ref, o_ref):
    s = _silu(c_ref[...]).astype(BF16)
    o_ref[...] = jnp.dot(s, w_ref[...].astype(BF16), preferred_element_type=F32) + b_ref[...]


def _ada(c_all, w_ada, b_ada):
    rows, d = c_all.shape
    n = w_ada.shape[1]
    tn = 1536 if n % 1536 == 0 else n
    return pl.pallas_call(
        _ada_kernel,
        out_shape=jax.ShapeDtypeStruct((rows, n), F32),
        grid=(n // tn,),
        in_specs=[
            pl.BlockSpec((rows, d), lambda j: (0, 0)),
            pl.BlockSpec((d, tn), lambda j: (0, j)),
            pl.BlockSpec((1, tn), lambda j: (0, j)),
        ],
        out_specs=pl.BlockSpec((rows, tn), lambda j: (0, j)),
        compiler_params=_cparams(("arbitrary",)),
        name="ada_mod",
    )(c_all, w_ada, b_ada.reshape(1, n))


def _rope_table(pos, half, period, active_lanes):
    inv = ROPE_THETA ** (-jnp.arange(half, dtype=F32) / half)
    lane = np.arange(LANES)
    within = lane % period
    first = jnp.asarray((within < half) & (lane < active_lanes))
    second = jnp.asarray((within >= half) & (within < 2 * half) & (lane < active_lanes))
    ang = pos.astype(F32)[:, None] * inv[jnp.asarray(within % half)][None, :]
    cos, sin = jnp.cos(ang), jnp.sin(ang)
    c = jnp.where(first | second, cos, 1.0)
    s1 = jnp.where(first, -sin, 0.0)
    s2 = jnp.where(second, sin, 0.0)
    return jnp.stack([c, s1, s2]).astype(F32)


def _rope_apply(v, tab, half):
    return (v * tab[0] + pltpu.roll(v, LANES - half, 1) * tab[1] + pltpu.roll(v, half, 1) * tab[2])


def _tile_rows(tab_ref, bb):
    t = tab_ref[...]
    if bb == 1:
        return t
    tb = t.shape[1]
    return jnp.broadcast_to(t[:, None], (3, bb, tb, LANES)).reshape(3, bb * tb, LANES)


def _inproj_kernel(x_ref, mod_ref, gmix_ref, wbig_ref, wsmall_ref, gq_ref, gk_ref,
                   ropea_ref, ropei_ref, ropes_ref, wg2_ref, bg2_ref,
                   q_out, k_out, v_out, iq_out, small_out, lg_out, gq_out, gk_out, gv_out, gr_out,
                   h_scr):
    j = pl.program_id(2)
    bb, tb, d = x_ref.shape
    rows = bb * tb

    @pl.when(j == 0)
    def _():
        x = x_ref[...]
        ms = jnp.mean(x * x, axis=-1, keepdims=True)
        y = x * lax.rsqrt(ms + EPS) * gmix_ref[...]
        mod = mod_ref[...]
        h = y * (1.0 + mod[:, :, d:2 * d]) + mod[:, :, 0:d]
        hb = h.astype(BF16).reshape(rows, d)
        h_scr[...] = hb
        small = jnp.dot(hb, wsmall_ref[...], preferred_element_type=F32)
        small = _rope_apply(small, _tile_rows(ropes_ref, bb), IDX_ROPE_DIM // 2)
        small_out[...] = small.reshape(bb, tb, LANES)
        pre = jnp.dot(small.astype(BF16), wg2_ref[...], preferred_element_type=F32) + bg2_ref[...]
        lsig = jnp.minimum(pre, 0.0) - jnp.log(1.0 + jnp.exp(-jnp.abs(pre)))
        lg_out[...] = (lsig / GLA_GATE_TEMP).reshape(bb, tb, lg_out.shape[-1])

    y = jnp.dot(h_scr[...], wbig_ref[0], preferred_element_type=F32)

    def headnorm_rope(g_ref, scale):
        tab = _tile_rows(ropea_ref, bb)
        outs = []
        for hh in range(N_ATTN_HEADS):
            yh = y[:, hh * LANES:(hh + 1) * LANES]
            n = yh * lax.rsqrt(jnp.mean(yh * yh, axis=-1, keepdims=True) + EPS) * g_ref[...]
            r = _rope_apply(n, tab, ROPE_DIM // 2)
            outs.append(r * scale if scale != 1.0 else r)
        return outs

    @pl.when(j == 0)
    def _():
        for hh, r in enumerate(headnorm_rope(gq_ref, ATTN_HEAD_DIM ** -0.5 * math.log2(math.e))):
            q_out[:, :, hh * LANES:(hh + 1) * LANES] = r.astype(BF16).reshape(bb, tb, LANES)

    @pl.when(j == 1)
    def _():
        for hh, r in enumerate(headnorm_rope(gk_ref, 1.0)):
            k_out[:, :, hh * LANES:(hh + 1) * LANES] = r.reshape(bb, tb, LANES)

    @pl.when(j == 2)
    def _():
        v_out[...] = y.reshape(bb, tb, y.shape[-1])

    @pl.when(j == 3)
    def _():
        tab = _tile_rows(ropei_ref, bb)
        for c in range(y.shape[-1] // LANES):
            r = _rope_apply(y[:, c * LANES:(c + 1) * LANES], tab, IDX_ROPE_DIM // 2)
            iq_out[:, :, c * LANES:(c + 1) * LANES] = r.astype(BF16).reshape(bb, tb, LANES)

    @pl.when(j == 4)
    def _():
        half = y.shape[-1] // 2
        gq_out[...] = (y[:, :half] * GLA_DK ** -0.5).reshape(bb, tb, half)
        gk_out[...] = y[:, half:].reshape(bb, tb, half)

    @pl.when(j == 5)
    def _():
        gv_out[...] = y.astype(BF16).reshape(bb, tb, y.shape[-1])

    @pl.when(j == 6)
    def _():
        gr_out[...] = y.reshape(bb, tb, y.shape[-1])


def _inproj(x, mod, g_mix, wbig, wsmall, g_q, g_k, ropea, ropei, ropes, wg2p, bg2, bb, tb):
    b, t, d = x.shape
    ngroups, _, gw = wbig.shape
    glak = GLA_HEADS * GLA_DK
    rows = bb * tb
    tok = lambda w, dt: jax.ShapeDtypeStruct((b, t, w), dt)
    tokspec = lambda w: pl.BlockSpec((bb, tb, w), lambda i, m, j: (i, m, 0))
    const = lambda shp: pl.BlockSpec(shp, lambda i, m, j: tuple(0 for _ in shp))
    tabspec = pl.BlockSpec((3, tb, LANES), lambda i, m, j: (0, m, 0))
    out_shapes = (tok(1024, BF16), tok(1024, F32), tok(1024, F32), tok(1024, BF16), tok(LANES, F32),
                  tok(glak, F32), tok(glak, F32), tok(glak, F32), tok(1024, BF16), tok(1024, F32))
    out_specs = tuple(tokspec(s.shape[-1]) for s in out_shapes)
    return pl.pallas_call(
        _inproj_kernel,
        out_shape=out_shapes,
        grid=(b // bb, t // tb, ngroups),
        in_specs=[
            tokspec(d),
            pl.BlockSpec((bb, 1, mod.shape[-1]), lambda i, m, j: (i, 0, 0)),
            const((1, 1, d)),
            pl.BlockSpec((1, d, gw), lambda i, m, j: (j, 0, 0)),
            const((d, LANES)),
            const((1, LANES)), const((1, LANES)),
            tabspec, tabspec, tabspec,
            const((LANES, glak)), const((1, glak)),
        ],
        out_specs=out_specs,
        scratch_shapes=[pltpu.VMEM((rows, d), BF16)],
        compiler_params=_cparams(("arbitrary", "arbitrary", "arbitrary")),
        name="norm_inproj",
    )(x, mod, g_mix.reshape(1, 1, d), wbig, wsmall, g_q.reshape(1, LANES), g_k.reshape(1, LANES),
      ropea, ropei, ropes, wg2p, bg2.reshape(1, glak))


def _dsa_kernel(iqt_ref, iwt_ref, ik_ref, qt_ref, k_hbm, vt_hbm, o_ref,
                u_scr, ub_scr, kbuf, vbuf, sem, m_scr, l_scr, acc_scr, s_scr, bias_scr,
                *, past, seq_len, topk, tq, tk):
    b = pl.program_id(0)
    i = pl.program_id(1)
    nkt = u_scr.shape[0]

    last_chunk = (past + (i + 1) * tq - 1) // CHUNK
    n_adm = jnp.minimum(seq_len, (last_chunk + 1) * CHUNK)
    nk = jnp.minimum((n_adm + tk - 1) // tk, nkt)

    def k_copy(j):
        off = pl.multiple_of(j * tk, tk)
        return pltpu.make_async_copy(k_hbm.at[b, pl.ds(off, tk)], kbuf.at[j % 2], sem.at[0, j % 2])

    def v_copy(j):
        return pltpu.make_async_copy(vt_hbm.at[b, j], vbuf.at[j % 2], sem.at[1, j % 2])

    k_copy(0).start()
    v_copy(0).start()

    q_chunk = (past + i * tq + lax.broadcasted_iota(jnp.int32, (tk, tq), 1)) // CHUNK
    key_iota = lax.broadcasted_iota(jnp.int32, (tk, tq), 0)
    idx_scale = (IDX_DIM * N_IDX_HEADS) ** -0.5

    def score_tile(j, carry):
        ik_t = ik_ref[0, pl.ds(pl.multiple_of(j * tk, tk), tk), :]
        acc = jnp.zeros((tk, tq), F32)
        for hh in range(N_IDX_HEADS):
            logits = jnp.dot(ik_t, iqt_ref[0, hh], preferred_element_type=F32)
            acc = acc + iwt_ref[0, hh:hh + 1, :] * jnp.maximum(logits, 0.0)
        kpos = j * tk + key_iota
        adm = (kpos // CHUNK <= q_chunk) & (kpos < seq_len)
        u = jnp.where(adm, acc * idx_scale, -jnp.inf)
        u_scr[j] = u
        ub_scr[j] = u.astype(ub_scr.dtype)
        return carry

    lax.fori_loop(0, nk, score_tile, 0)

    group = 2 if nkt % 2 == 0 else 1
    n_groups = (nk + group - 1) // group

    @pl.when(n_groups * group > nk)
    def _():
        u_scr[nk] = jnp.full((tk, tq), -jnp.inf, F32)
        ub_scr[nk] = jnp.full((tk, tq), -jnp.inf, ub_scr.dtype)

    def count_ge(scr, thr_row):
        rows = SUBLANES * (4 // jnp.dtype(scr.dtype).itemsize)
        thr_b = jnp.broadcast_to(thr_row.astype(scr.dtype), (rows, tq))
        one, zero = jnp.ones((), scr.dtype), jnp.zeros((), scr.dtype)

        def body(g, cnt):
            for r in range(group):
                x = scr[g * group + r].reshape(tk // rows, rows, tq)
                hit = jnp.where(x >= thr_b[None], one, zero)
                parts = [hit[n] for n in range(tk // rows)]
                while len(parts) > 1:
                    parts = [parts[n] + parts[n + 1] for n in range(0, len(parts), 2)]
                cnt = cnt + parts[0].astype(F32)
            return cnt

        cnt = lax.fori_loop(0, n_groups, body, jnp.zeros((rows, tq), F32))
        return jnp.sum(cnt, axis=0, keepdims=True)

    int_min = jnp.int32(-2 ** 31)
    kf = float(topk)

    def coarse_value(t_biased):
        mono = t_biased ^ int_min
        bits = jnp.where(mono >= 0, mono, mono ^ jnp.int32(0x7FFF0000))
        return lax.bitcast_convert_type(bits, F32)

    def coarse_step(it, t_biased):
        cand = t_biased | lax.shift_left(jnp.int32(1), jnp.int32(31) - it)
        return jnp.where(count_ge(ub_scr, coarse_value(cand)) >= kf, cand, t_biased)

    t16 = lax.fori_loop(0, 16, coarse_step, jnp.zeros((1, tq), jnp.int32))
    few = (t16 >= 0) & (t16 <= jnp.int32(0x007F0000))

    def to_mono(bits):
        return jnp.where(bits >= 0, bits, bits ^ jnp.int32(0x7FFFFFFF))

    c_mono = to_mono(lax.bitcast_convert_type(coarse_value(t16), jnp.int32))
    lo0 = c_mono - jnp.int32(0x8000)
    hi0 = c_mono + jnp.int32(0x10000)
    hi0 = jnp.where(few, lo0, hi0)

    def fine_cond(state):
        it, _, _, pending = state
        return (it < FINE_STEPS) & (pending > 0)

    def fine_step(state):
        it, lo, hi, _ = state
        mid = lo + lax.shift_right_arithmetic(hi - lo + 1, jnp.int32(1))
        total = count_ge(u_scr, lax.bitcast_convert_type(to_mono(mid), F32))
        ok = total >= kf
        lo = jnp.where(ok, mid, lo)
        hi = jnp.where(total == kf, mid, jnp.where(ok, hi, mid - 1))
        return it + 1, lo, hi, jnp.max(jnp.where(hi > lo, 1, 0))

    _, lo_fin, _, _ = lax.while_loop(fine_cond, fine_step,
                                     (jnp.int32(0), lo0, hi0, jnp.max(jnp.where(hi0 > lo0, 1, 0))))
    thr_fin = lax.bitcast_convert_type(to_mono(lo_fin), F32)
    thr = jnp.where(few, F32_LOWEST, jnp.maximum(thr_fin, F32_LOWEST))

    m_scr[...] = jnp.full(m_scr.shape, MASK_NEG, F32)
    l_scr[...] = jnp.zeros(l_scr.shape, F32)
    acc_scr[...] = jnp.zeros(acc_scr.shape, F32)

    def select_bias(j):
        bias_scr[...] = jnp.where(u_scr[j] >= thr, 0.0, MASK_NEG)

    n_slots = s_scr.shape[0]

    def scores_head(j, hh):
        hs = slice(hh * LANES, (hh + 1) * LANES)
        s_scr[hh % n_slots] = (jnp.dot(kbuf[j % 2, :, hs], qt_ref[0, hh], preferred_element_type=F32)
                               + bias_scr[...])

    def attend_head(j, hh):
        hs = slice(hh * LANES, (hh + 1) * LANES)
        s = s_scr[hh % n_slots]
        m_prev = m_scr[hh:hh + 1, :]
        m_new = jnp.maximum(m_prev, jnp.max(s, axis=0, keepdims=True))
        alpha = jnp.exp2(m_prev - m_new)
        p = jnp.exp2(s - m_new)
        l_scr[hh:hh + 1, :] = alpha * l_scr[hh:hh + 1, :] + jnp.sum(p, axis=0, keepdims=True)
        acc_scr[hs, :] = alpha * acc_scr[hs, :] + jnp.dot(vbuf[j % 2, hs, :], p.astype(BF16),
                                                          preferred_element_type=F32)
        m_scr[hh:hh + 1, :] = m_new

    def tile_steps(j, has_next):
        v_copy(j).wait()
        for hh in range(N_ATTN_HEADS):
            ahead = hh + SCORE_LOOKAHEAD
            if ahead < N_ATTN_HEADS:
                scores_head(j, ahead)
            elif has_next:
                if ahead == N_ATTN_HEADS:
                    @pl.when(j + 2 < nk)
                    def _():
                        k_copy(j + 2).start()

                    k_copy(j + 1).wait()
                    select_bias(j + 1)
                scores_head(j + 1, ahead - N_ATTN_HEADS)
            attend_head(j, hh)

    @pl.when(nk > 1)
    def _():
        k_copy(1).start()

    k_copy(0).wait()
    select_bias(0)
    for hh in range(SCORE_LOOKAHEAD):
        scores_head(0, hh)

    def attend_tile(j, carry):
        v_copy(j + 1).start()
        tile_steps(j, True)
        return carry

    lax.fori_loop(0, nk - 1, attend_tile, 0)
    tile_steps(nk - 1, False)

    for hh in range(N_ATTN_HEADS):
        hs = slice(hh * LANES, (hh + 1) * LANES)
        o_t = acc_scr[hs, :] / l_scr[hh:hh + 1, :]
        o_ref[0, :, hs] = o_t.T.astype(o_ref.dtype)


def _dsa(iq_t, iw_t, ik_bf, q_t, k_bf, v_t, *, past, seq_len, topk, tq, tk):
    b, _, _, t = iq_t.shape
    lp = ik_bf.shape[1]
    nkt = lp // tk
    hd = N_ATTN_HEADS * ATTN_HEAD_DIM
    kern = functools.partial(_dsa_kernel, past=past, seq_len=seq_len, topk=topk, tq=tq, tk=tk)
    return pl.pallas_call(
        kern,
        out_shape=jax.ShapeDtypeStruct((b, t, hd), BF16),
        grid=(b, t // tq),
        in_specs=[
            pl.BlockSpec((1, N_IDX_HEADS, IDX_DIM, tq), lambda bi, i: (bi, 0, 0, i)),
            pl.BlockSpec((1, N_IDX_HEADS, tq), lambda bi, i: (bi, 0, i)),
            pl.BlockSpec((1, lp, IDX_DIM), lambda bi, i: (bi, 0, 0)),
            pl.BlockSpec((1, N_ATTN_HEADS, ATTN_HEAD_DIM, tq), lambda bi, i: (bi, 0, 0, i)),
            pl.BlockSpec(memory_space=pl.ANY),
            pl.BlockSpec(memory_space=pl.ANY),
        ],
        out_specs=pl.BlockSpec((1, tq, hd), lambda bi, i: (bi, i, 0)),
        scratch_shapes=[
            pltpu.VMEM((nkt, tk, tq), F32),
            pltpu.VMEM((nkt, tk, tq), jnp.bfloat16),
            pltpu.VMEM((2, tk, hd), BF16),
            pltpu.VMEM((2, hd, tk), BF16),
            pltpu.SemaphoreType.DMA((2, 2)),
            pltpu.VMEM((N_ATTN_HEADS, tq), F32),
            pltpu.VMEM((N_ATTN_HEADS, tq), F32),
            pltpu.VMEM((hd, tq), F32),
            pltpu.VMEM((SCORE_SLOTS, tk, tq), F32),
            pltpu.VMEM((tk, tq), F32),
        ],
        compiler_params=_cparams(("arbitrary", "arbitrary")),
        name="dsa_attend",
    )(iq_t, iw_t, ik_bf, q_t, k_bf, v_t)


def _gla_kernel(q_ref, k_ref, v_ref, lg_ref, gr_ref, g_ref, s0_ref, o_ref, sT_out, st_scr, *, n_chunks):
    tstep = pl.program_id(2)

    @pl.when(tstep == 0)
    def _():
        st_scr[...] = s0_ref[0, 0]

    row = lax.broadcasted_iota(jnp.int32, (CHUNK, GLA_DK), 0)
    tri = (lax.broadcasted_iota(jnp.int32, (CHUNK, CHUNK), 0)
           >= lax.broadcasted_iota(jnp.int32, (CHUNK, CHUNK), 1))

    for c in range(n_chunks):
        rs = slice(c * CHUNK, (c + 1) * CHUNK)
        bcum = lg_ref[0, rs, :]
        shift = 1
        while shift < CHUNK:
            bcum = bcum + jnp.where(row >= shift, pltpu.roll(bcum, shift, 0), 0.0)
            shift *= 2
        bl = bcum[CHUNK - 1:CHUNK, :]
        q = q_ref[0, rs, :]
        k = k_ref[0, rs, :]
        v = v_ref[0, rs, :]
        qe = (q * jnp.exp(bcum)).astype(BF16)
        ke = (k * jnp.exp(-bcum)).astype(BF16)
        kd = (k * jnp.exp(bl - bcum)).astype(BF16)
        st = st_scr[...]
        inter = lax.dot_general(qe, st.astype(BF16), (((1,), (1,)), ((), ())), preferred_element_type=F32)
        a = lax.dot_general(qe, ke, (((1,), (1,)), ((), ())), preferred_element_type=F32)
        a = jnp.where(tri, a, 0.0)
        o = inter + jnp.dot(a.astype(BF16), v, preferred_element_type=F32)
        st_scr[...] = st * jnp.exp(bl) + lax.dot_general(v, kd, (((0,), (0,)), ((), ())),
                                                          preferred_element_type=F32)
        n = o * lax.rsqrt(jnp.mean(o * o, axis=-1, keepdims=True) + EPS) * g_ref[...]
        o_ref[0, rs, :] = (n * _silu(gr_ref[0, rs, :])).astype(o_ref.dtype)

    @pl.when(tstep == pl.num_programs(2) - 1)
    def _():
        sT_out[0, 0] = st_scr[...]


def _gla(gq, gk, gv, lg, gr, g_gla, s0t, tg):
    b, t, _ = gq.shape
    kern = functools.partial(_gla_kernel, n_chunks=tg // CHUNK)
    tok = lambda w: pl.BlockSpec((1, tg, w), lambda bi, h, m: (bi, m, h))
    st_spec = pl.BlockSpec((1, 1, GLA_DV, GLA_DK), lambda bi, h, m: (bi, h, 0, 0))
    return pl.pallas_call(
        kern,
        out_shape=(jax.ShapeDtypeStruct((b, t, GLA_HEADS * GLA_DV), BF16),
                   jax.ShapeDtypeStruct((b, GLA_HEADS, GLA_DV, GLA_DK), F32)),
        grid=(b, GLA_HEADS, t // tg),
        in_specs=[tok(GLA_DK), tok(GLA_DK), tok(GLA_DV), tok(GLA_DK), tok(GLA_DV),
                  pl.BlockSpec((1, GLA_DV), lambda bi, h, m: (0, 0)), st_spec],
        out_specs=(tok(GLA_DV), st_spec),
        scratch_shapes=[pltpu.VMEM((GLA_DV, GLA_DK), F32)],
        compiler_params=_cparams(("arbitrary", "arbitrary", "arbitrary")),
        name="gla",
    )(gq, gk, gv, lg, gr, g_gla.reshape(1, GLA_DV), s0t)


def _outproj_kernel(oa_ref, og_ref, wa_ref, wg_ref, x_ref, mod_ref, gffn_ref, x1_out, h2_out):
    bb, tb, d = x_ref.shape
    rows = bb * tb
    oa = oa_ref[...].reshape(rows, oa_ref.shape[-1])
    og = og_ref[...].reshape(rows, og_ref.shape[-1])
    mix = (jnp.dot(oa, wa_ref[...], preferred_element_type=F32)
           + jnp.dot(og, wg_ref[...], preferred_element_type=F32)).reshape(bb, tb, d)
    mod = mod_ref[...]
    x1 = x_ref[...] + mod[:, :, 2 * d:3 * d] * mix
    x1_out[...] = x1
    y = x1 * lax.rsqrt(jnp.mean(x1 * x1, axis=-1, keepdims=True) + EPS) * gffn_ref[...]
    h2_out[...] = (y * (1.0 + mod[:, :, 4 * d:5 * d]) + mod[:, :, 3 * d:4 * d]).astype(BF16)


def _outproj(o_a, o_g, wout_a, wout_g, x, mod, g_ffn, bb, tb):
    b, t, d = x.shape
    tokspec = lambda w: pl.BlockSpec((bb, tb, w), lambda i, m: (i, m, 0))
    const = lambda shp: pl.BlockSpec(shp, lambda i, m: tuple(0 for _ in shp))
    return pl.pallas_call(
        _outproj_kernel,
        out_shape=(jax.ShapeDtypeStruct((b, t, d), F32), jax.ShapeDtypeStruct((b, t, d), BF16)),
        grid=(b // bb, t // tb),
        in_specs=[tokspec(o_a.shape[-1]), tokspec(o_g.shape[-1]), const(wout_a.shape), const(wout_g.shape),
                  tokspec(d), pl.BlockSpec((bb, 1, mod.shape[-1]), lambda i, m: (i, 0, 0)), const((1, 1, d))],
        out_specs=(tokspec(d), tokspec(d)),
        compiler_params=_cparams(("arbitrary", "arbitrary")),
        name="outproj_norm",
    )(o_a, o_g, wout_a, wout_g, x, mod, g_ffn.reshape(1, 1, d))


def _ffn_kernel(h_ref, halo_ref, wa_ref, wb_ref, wca_ref, wcb_ref, bca_ref, bcb_ref, sa_ref, sb_ref,
                wd_ref, x1_ref, mod_ref, y_out, nba_out, nbb_out, acc_scr, hext_scr, act_scr):
    m = pl.program_id(1)
    f = pl.program_id(2)
    nf = pl.num_programs(2) - 1
    bb, tb, d = h_ref.shape
    rows = bb * tb
    tf = wa_ref.shape[-1]
    halo = halo_ref.shape[1]

    @pl.when(f == 0)
    def _():
        hext_scr[:, 0:halo, :] = halo_ref[...]
        hext_scr[:, halo:, :] = h_ref[...]
        acc_scr[...] = jnp.zeros(acc_scr.shape, F32)
        act_scr[1] = jnp.zeros(act_scr.shape[1:], act_scr.dtype)

    def conv_branch(hext, tpos, w_ref, wc_ref, bc_ref, s_ref, nb_out):
        u_ext = jnp.dot(hext, w_ref[0], preferred_element_type=F32).reshape(bb, tb + halo, tf)
        u = u_ext[:, halo:, :].reshape(rows, tf)
        prev = jnp.where(m == 0, s_ref[...], u_ext[:, halo - (CONV_W - 1):halo, :])
        bcast = lambda r: jnp.broadcast_to(prev[:, r:r + 1, :], (bb, tb, tf)).reshape(rows, tf)
        p0, p1 = bcast(0), bcast(1)
        u1 = jnp.where(tpos == 0, p1, pltpu.roll(u, 1, 0))
        u2 = jnp.where(tpos == 0, p0, jnp.where(tpos == 1, p1, pltpu.roll(u, 2, 0)))
        wc = wc_ref[...]
        nb_out[:, 0] = u.reshape(bb, tb, tf)[:, tb - (CONV_W - 1):, :]
        return bc_ref[...] + wc[0:1] * u2 + wc[1:2] * u1 + wc[2:3] * u

    @pl.when(f < nf)
    def _():
        act_prev = act_scr[(f + 1) % 2]
        hext = hext_scr[...].reshape(bb * (tb + halo), d)
        tpos = lax.broadcasted_iota(jnp.int32, (rows, tf), 0) % tb
        ua = conv_branch(hext, tpos, wa_ref, wca_ref, bca_ref, sa_ref, nba_out)
        ub = conv_branch(hext, tpos, wb_ref, wcb_ref, bcb_ref, sb_ref, nbb_out)
        acc_scr[...] += jnp.dot(act_prev, wd_ref[...], preferred_element_type=F32)
        act_scr[f % 2] = (_silu(ua) * ub).astype(act_scr.dtype)

    @pl.when(f == nf)
    def _():
        acc = acc_scr[...] + jnp.dot(act_scr[(f + 1) % 2], wd_ref[...], preferred_element_type=F32)
        mod = mod_ref[...]
        y_out[...] = x1_ref[...] + mod[:, :, 5 * d:6 * d] * acc.reshape(bb, tb, d)


def _ffn(h2, w_up, w_conv, b_conv, conv_state, w_down, x1, mod, bb, tb):
    b, t, d = x1.shape
    dff = w_down.shape[0]
    tf = w_up.shape[-1]
    nf = dff // tf
    halo = 2 * SUBLANES
    hb = tb // halo
    tokspec = lambda w: pl.BlockSpec((bb, tb, w), lambda i, m, f: (i, m, 0))
    up = lambda f: jnp.minimum(f, nf - 1)
    a_col = lambda r: pl.BlockSpec((r, tf), lambda i, m, f: (0, up(f)))
    b_col = lambda r: pl.BlockSpec((r, tf), lambda i, m, f: (0, nf + up(f)))
    st_a = pl.BlockSpec((bb, CONV_W - 1, tf), lambda i, m, f: (i, 0, up(f)))
    st_b = pl.BlockSpec((bb, CONV_W - 1, tf), lambda i, m, f: (i, 0, nf + up(f)))
    nb_spec = pl.BlockSpec((bb, 1, CONV_W - 1, tf), lambda i, m, f: (i, m, 0, up(f)))
    bc2 = b_conv.reshape(1, 2 * dff)
    return pl.pallas_call(
        _ffn_kernel,
        out_shape=(jax.ShapeDtypeStruct((b, t, d), F32),
                   jax.ShapeDtypeStruct((b, t // tb, CONV_W - 1, dff), F32),
                   jax.ShapeDtypeStruct((b, t // tb, CONV_W - 1, dff), F32)),
        grid=(b // bb, t // tb, nf + 1),
        in_specs=[
            tokspec(d),
            pl.BlockSpec((bb, halo, d), lambda i, m, f: (i, jnp.maximum(m * hb - 1, 0), 0)),
            pl.BlockSpec((1, d, tf), lambda i, m, f: (up(f), 0, 0)),
            pl.BlockSpec((1, d, tf), lambda i, m, f: (nf + up(f), 0, 0)),
            a_col(CONV_W), b_col(CONV_W), a_col(1), b_col(1), st_a, st_b,
            pl.BlockSpec((tf, d), lambda i, m, f: (jnp.maximum(f - 1, 0), 0)),
            tokspec(d),
            pl.BlockSpec((bb, 1, mod.shape[-1]), lambda i, m, f: (i, 0, 0)),
        ],
        out_specs=(tokspec(d), nb_spec, nb_spec),
        scratch_shapes=[pltpu.VMEM((bb * tb, d), F32), pltpu.VMEM((bb, tb + halo, d), BF16),
                        pltpu.VMEM((2, bb * tb, tf), BF16)],
        compiler_params=_cparams(("arbitrary", "arbitrary", "arbitrary")),
        name="conv_ffn",
    )(h2, h2, w_up, w_up, w_conv, w_conv, bc2, bc2, conv_state, conv_state, w_down, x1, mod)


def _prep_weights(w_in, w_gate2, w_out, w_up, w_down):
    hd = N_ATTN_HEADS * ATTN_HEAD_DIM
    sizes = (hd, hd, hd, N_IDX_HEADS * IDX_DIM, IDX_DIM, N_IDX_HEADS,
             GLA_HEADS * GLA_DK, GLA_HEADS * GLA_DK, GLA_HEADS * GLA_DV, GLA_HEADS * GLA_DV, GLA_GATE_RANK)
    offs = np.concatenate([[0], np.cumsum(sizes)])
    col = lambda n: w_in[:, int(offs[n]):int(offs[n + 1])]
    d = w_in.shape[0]
    tile_major = lambda w, width: w.reshape(d, w.shape[1] // width, width).transpose(1, 0, 2).astype(BF16)
    wbig = tile_major(jnp.concatenate([col(0), col(1), col(2), col(3), col(6), col(7), col(8), col(9)], axis=1),
                      PROJ_GROUP)
    pad = LANES - (IDX_DIM + N_IDX_HEADS + GLA_GATE_RANK)
    wsmall = jnp.concatenate([col(4), col(5), col(10), jnp.zeros((w_in.shape[0], pad), w_in.dtype)],
                             axis=1).astype(BF16)
    lo = IDX_DIM + N_IDX_HEADS
    wg2p = jnp.zeros((LANES, w_gate2.shape[1]), F32).at[lo:lo + GLA_GATE_RANK].set(w_gate2).astype(BF16)
    return dict(wbig=wbig, wsmall=wsmall, wg2p=wg2p,
                wout_a=w_out[:hd].astype(BF16), wout_g=w_out[hd:].astype(BF16),
                w_up=tile_major(w_up, FFN_TILE), w_down=w_down.astype(BF16))


def _layer(x, mod, past, past_k, past_v, past_ik, gla_s0, conv_state, lw, pw, *, bb, tb, tq, tk, tg):
    (g_mix, g_ffn, g_q, g_k, b_gate2, g_gla, w_conv, b_conv) = lw
    b, t, d = x.shape
    hd = N_ATTN_HEADS * ATTN_HEAD_DIM
    pos = past + jnp.arange(t, dtype=jnp.int32)
    ropea = _rope_table(pos, ROPE_DIM // 2, LANES, LANES)
    ropei = _rope_table(pos, IDX_ROPE_DIM // 2, IDX_DIM, LANES)
    ropes = _rope_table(pos, IDX_ROPE_DIM // 2, LANES, IDX_DIM)

    (q_bf, k, v, iq, small, lg, gq, gk, gv, gr) = _inproj(
        x, mod, g_mix, pw["wbig"], pw["wsmall"], g_q, g_k, ropea, ropei, ropes, pw["wg2p"], b_gate2, bb, tb)
    ik = small[:, :, :IDX_DIM]
    iw = small[:, :, IDX_DIM:IDX_DIM + N_IDX_HEADS]

    k_bf, v_bf, ik_bf = k.astype(BF16), v.astype(BF16), ik.astype(BF16)
    if past_k is not None:
        k_bf = jnp.concatenate([past_k.reshape(b, past, hd).astype(BF16), k_bf], axis=1)
        v_bf = jnp.concatenate([past_v.reshape(b, past, hd).astype(BF16), v_bf], axis=1)
        ik_bf = jnp.concatenate([past_ik.astype(BF16), ik_bf], axis=1)
    seq_len = past + t
    lp = -(-seq_len // tk) * tk
    if lp != seq_len:
        padw = ((0, 0), (0, lp - seq_len), (0, 0))
        k_bf, v_bf, ik_bf = jnp.pad(k_bf, padw), jnp.pad(v_bf, padw), jnp.pad(ik_bf, padw)
    tpad = -(-t // tq) * tq
    qpad = lambda a: jnp.pad(a, ((0, 0), (0, tpad - t), (0, 0))) if tpad != t else a
    iq_t = qpad(iq).reshape(b, tpad, N_IDX_HEADS, IDX_DIM).transpose(0, 2, 3, 1)
    iw_t = qpad(iw).transpose(0, 2, 1)
    q_t = qpad(q_bf).reshape(b, tpad, N_ATTN_HEADS, ATTN_HEAD_DIM).transpose(0, 2, 3, 1)
    v_t = v_bf.reshape(b, lp // tk, tk, hd).transpose(0, 1, 3, 2)
    topk = min(TOPK_MAX, seq_len // 4)
    o_a = _dsa(iq_t, iw_t, ik_bf, q_t, k_bf, v_t, past=past, seq_len=seq_len, topk=topk, tq=tq, tk=tk)
    o_a = o_a[:, :t]

    s0t = jnp.swapaxes(gla_s0, -1, -2)
    o_g, s_t = _gla(gq, gk, gv, lg, gr, g_gla, s0t, tg)
    s_new = jnp.swapaxes(s_t, -1, -2)

    x1, h2 = _outproj(o_a, o_g, pw["wout_a"], pw["wout_g"], x, mod, g_ffn, bb, tb)
    y, nb_a, nb_b = _ffn(h2, pw["w_up"], w_conv, b_conv, conv_state, pw["w_down"], x1, mod, bb, tb)
    new_buf = jnp.concatenate([nb_a[:, -1], nb_b[:, -1]], axis=-1)
    return (y, k.reshape(b, t, N_ATTN_HEADS, ATTN_HEAD_DIM), v.reshape(b, t, N_ATTN_HEADS, ATTN_HEAD_DIM),
            ik, s_new, new_buf)


def _tiles(b, t, seq_len):
    if t >= 512:
        bb, tb = 1, 512
    else:
        bb, tb = min(b, 512 // t), t
    tq = 256 if t % 256 == 0 else LANES
    tk = 256 if seq_len % 256 == 0 else LANES
    tg = min(t, 256)
    return dict(bb=bb, tb=tb, tq=tq, tk=tk, tg=tg)


def kernel(x_prompt, x_sample, c_prompt, c_sample, cache_k, cache_v, cache_idx_k, state_gla, state_ffn_conv,
           w_ada, b_ada, g_mix, g_ffn, w_in, g_q, g_k, w_gate2, b_gate2, g_gla, w_out, w_up, w_conv, b_conv,
           w_down):
    bp, s, d = x_prompt.shape
    bs, ts, _ = x_sample.shape
    depth = w_ada.shape[0]
    past = cache_k.shape[2]
    dff = w_down.shape[1]
    dt = x_prompt.dtype

    y_p, y_s = x_prompt, x_sample
    outs = [[] for _ in range(10)]
    for l in range(depth):
        n_c = bp + bs
        rows = -(-n_c // SUBLANES) * SUBLANES
        c_all = jnp.concatenate([c_prompt, c_sample, jnp.zeros((rows - n_c, d), dt)], axis=0)
        mod = _ada(c_all, w_ada[l], b_ada[l])
        mod_p = mod[:bp].reshape(bp, 1, 6 * d)
        mod_s = mod[bp:n_c].reshape(bs, 1, 6 * d)
        pw = _prep_weights(w_in[l], w_gate2[l], w_out[l], w_up[l], w_down[l])
        lw = (g_mix[l], g_ffn[l], g_q[l], g_k[l], b_gate2[l], g_gla[l], w_conv[l], b_conv[l])

        gla0 = jnp.zeros((bp, GLA_HEADS, GLA_DK, GLA_DV), dt)
        conv0 = jnp.zeros((bp, CONV_W - 1, 2 * dff), dt)
        y_p, k1, v1, ik1, s1, c1 = _layer(y_p, mod_p, 0, None, None, None, gla0, conv0, lw, pw,
                                          **_tiles(bp, s, s))
        y_s, k2, v2, ik2, s2, c2 = _layer(y_s, mod_s, past, cache_k[l], cache_v[l], cache_idx_k[l],
                                          state_gla[l], state_ffn_conv[l], lw, pw,
                                          **_tiles(bs, ts, past + ts))
        for lst, val in zip(outs, (k1, v1, ik1, s1, c1, k2, v2, ik2, s2, c2)):
            lst.append(val)

    return (y_p, y_s) + tuple(jnp.stack(o) for o in outs)
```

```python
import functools
import math

import jax
import jax.numpy as jnp
import numpy as np
from jax import lax
from jax.experimental import pallas as pl
from jax.experimental.pallas import tpu as pltpu

F32 = jnp.float32
BF16 = jnp.bfloat16

CHUNK = 64
N_ATTN_HEADS = 8
ATTN_HEAD_DIM = 128
ROPE_DIM = 32
ROPE_THETA = 500000.0
N_IDX_HEADS = 16
IDX_DIM = 64
IDX_ROPE_DIM = 16
TOPK_MAX = 256
GLA_HEADS = 4
GLA_DK = 128
GLA_DV = 256
GLA_GATE_RANK = 16
GLA_GATE_TEMP = 16.0
CONV_W = 3
EPS = 1e-6

LANES = 128
SUBLANES = 8
VMEM_LIMIT_BYTES = 56 * 1024 * 1024

SCORE_LOOKAHEAD = 4
SCORE_SLOTS = 8
FINE_STEPS = 17
PROJ_GROUP = 1024
FFN_TILE = 512

MASK_NEG = -1e30
F32_LOWEST = -3.4028235e38


def _cparams(sem):
    return pltpu.CompilerParams(dimension_semantics=sem, vmem_limit_bytes=VMEM_LIMIT_BYTES)


def _sigmoid(x):
    return 1.0 / (1.0 + jnp.exp(-x))


def _silu(x):
    return x * _sigmoid(x)


def _ada_kernel(c_ref, w_ref, b_ref, o_ref):
    s = _silu(c_ref[...]).astype(BF16)
    o_ref[...] = jnp.dot(s, w_ref[...].astype(BF16), preferred_element_type=F32) + b_ref[...]


def _ada(c_all, w_ada, b_ada):
    rows, d = c_all.shape
    n = w_ada.shape[1]
    tn = 1536 if n % 1536 == 0 else n
    return pl.pallas_call(
        _ada_kernel,
        out_shape=jax.ShapeDtypeStruct((rows, n), F32),
        grid=(n // tn,),
        in_specs=[
            pl.BlockSpec((rows, d), lambda j: (0, 0)),
            pl.BlockSpec((d, tn), lambda j: (0, j)),
            pl.BlockSpec((1, tn), lambda j: (0, j)),
        ],
        out_specs=pl.BlockSpec((rows, tn), lambda j: (0, j)),
        compiler_params=_cparams(("arbitrary",)),
        name="ada_mod",
    )(c_all, w_ada, b_ada.reshape(1, n))


def _rope_table(pos, half, period, active_lanes):
    inv = ROPE_THETA ** (-jnp.arange(half, dtype=F32) / half)
    lane = np.arange(LANES)
    within = lane % period
    first = jnp.asarray((within < half) & (lane < active_lanes))
    second = jnp.asarray((within >= half) & (within < 2 * half) & (lane < active_lanes))
    ang = pos.astype(F32)[:, None] * inv[jnp.asarray(within % half)][None, :]
    cos, sin = jnp.cos(ang), jnp.sin(ang)
    c = jnp.where(first | second, cos, 1.0)
    s1 = jnp.where(first, -sin, 0.0)
    s2 = jnp.where(second, sin, 0.0)
    return jnp.stack([c, s1, s2]).astype(F32)


def _rope_apply(v, tab, half):
    return (v * tab[0] + pltpu.roll(v, LANES - half, 1) * tab[1] + pltpu.roll(v, half, 1) * tab[2])


def _tile_rows(tab_ref, bb):
    t = tab_ref[...]
    if bb == 1:
        return t
    tb = t.shape[1]
    return jnp.broadcast_to(t[:, None], (3, bb, tb, LANES)).reshape(3, bb * tb, LANES)


def _inproj_kernel(x_ref, mod_ref, gmix_ref, wbig_ref, wsmall_ref, gq_ref, gk_ref,
                   ropea_ref, ropei_ref, ropes_ref, wg2_ref, bg2_ref,
                   q_out, k_out, v_out, iq_out, small_out, lg_out, gq_out, gk_out, gv_out, gr_out,
                   h_scr):
    j = pl.program_id(2)
    bb, tb, d = x_ref.shape
    rows = bb * tb

    @pl.when(j == 0)
    def _():
        x = x_ref[...]
        ms = jnp.mean(x * x, axis=-1, keepdims=True)
        y = x * lax.rsqrt(ms + EPS) * gmix_ref[...]
        mod = mod_ref[...]
        h = y * (1.0 + mod[:, :, d:2 * d]) + mod[:, :, 0:d]
        hb = h.astype(BF16).reshape(rows, d)
        h_scr[...] = hb
        small = jnp.dot(hb, wsmall_ref[...], preferred_element_type=F32)
        small = _rope_apply(small, _tile_rows(ropes_ref, bb), IDX_ROPE_DIM // 2)
        small_out[...] = small.reshape(bb, tb, LANES)
        pre = jnp.dot(small.astype(BF16), wg2_ref[...], preferred_element_type=F32) + bg2_ref[...]
        lsig = jnp.minimum(pre, 0.0) - jnp.log(1.0 + jnp.exp(-jnp.abs(pre)))
        lg_out[...] = (lsig / GLA_GATE_TEMP).reshape(bb, tb, lg_out.shape[-1])

    y = jnp.dot(h_scr[...], wbig_ref[...], preferred_element_type=F32)

    def headnorm_rope(g_ref, scale):
        tab = _tile_rows(ropea_ref, bb)
        outs = []
        for hh in range(N_ATTN_HEADS):
            yh = y[:, hh * LANES:(hh + 1) * LANES]
            n = yh * lax.rsqrt(jnp.mean(yh * yh, axis=-1, keepdims=True) + EPS) * g_ref[...]
            r = _rope_apply(n, tab, ROPE_DIM // 2)
            outs.append(r * scale if scale != 1.0 else r)
        return outs

    @pl.when(j == 0)
    def _():
        for hh, r in enumerate(headnorm_rope(gq_ref, ATTN_HEAD_DIM ** -0.5 * math.log2(math.e))):
            q_out[:, :, hh * LANES:(hh + 1) * LANES] = r.astype(BF16).reshape(bb, tb, LANES)

    @pl.when(j == 1)
    def _():
        for hh, r in enumerate(headnorm_rope(gk_ref, 1.0)):
            k_out[:, :, hh * LANES:(hh + 1) * LANES] = r.reshape(bb, tb, LANES)

    @pl.when(j == 2)
    def _():
        v_out[...] = y.reshape(bb, tb, y.shape[-1])

    @pl.when(j == 3)
    def _():
        tab = _tile_rows(ropei_ref, bb)
        for c in range(y.shape[-1] // LANES):
            r = _rope_apply(y[:, c * LANES:(c + 1) * LANES], tab, IDX_ROPE_DIM // 2)
            iq_out[:, :, c * LANES:(c + 1) * LANES] = r.astype(BF16).reshape(bb, tb, LANES)

    @pl.when(j == 4)
    def _():
        half = y.shape[-1] // 2
        gq_out[...] = (y[:, :half] * GLA_DK ** -0.5).reshape(bb, tb, half)
        gk_out[...] = y[:, half:].reshape(bb, tb, half)

    @pl.when(j == 5)
    def _():
        gv_out[...] = y.astype(BF16).reshape(bb, tb, y.shape[-1])

    @pl.when(j == 6)
    def _():
        gr_out[...] = y.reshape(bb, tb, y.shape[-1])


def _inproj(x, mod, g_mix, wbig, wsmall, g_q, g_k, ropea, ropei, ropes, wg2p, bg2, bb, tb):
    b, t, d = x.shape
    gw = PROJ_GROUP
    ngroups = wbig.shape[1] // gw
    glak = GLA_HEADS * GLA_DK
    rows = bb * tb
    tok = lambda w, dt: jax.ShapeDtypeStruct((b, t, w), dt)
    tokspec = lambda w: pl.BlockSpec((bb, tb, w), lambda i, m, j: (i, m, 0))
    const = lambda shp: pl.BlockSpec(shp, lambda i, m, j: tuple(0 for _ in shp))
    tabspec = pl.BlockSpec((3, tb, LANES), lambda i, m, j: (0, m, 0))
    out_shapes = (tok(1024, BF16), tok(1024, F32), tok(1024, F32), tok(1024, BF16), tok(LANES, F32),
                  tok(glak, F32), tok(glak, F32), tok(glak, F32), tok(1024, BF16), tok(1024, F32))
    out_specs = tuple(tokspec(s.shape[-1]) for s in out_shapes)
    return pl.pallas_call(
        _inproj_kernel,
        out_shape=out_shapes,
        grid=(b // bb, t // tb, ngroups),
        in_specs=[
            tokspec(d),
            pl.BlockSpec((bb, 1, mod.shape[-1]), lambda i, m, j: (i, 0, 0)),
            const((1, 1, d)),
            pl.BlockSpec((d, gw), lambda i, m, j: (0, j)),
            const((d, LANES)),
            const((1, LANES)), const((1, LANES)),
            tabspec, tabspec, tabspec,
            const((LANES, glak)), const((1, glak)),
        ],
        out_specs=out_specs,
        scratch_shapes=[pltpu.VMEM((rows, d), BF16)],
        compiler_params=_cparams(("arbitrary", "arbitrary", "arbitrary")),
        name="norm_inproj",
    )(x, mod, g_mix.reshape(1, 1, d), wbig, wsmall, g_q.reshape(1, LANES), g_k.reshape(1, LANES),
      ropea, ropei, ropes, wg2p, bg2.reshape(1, glak))


def _dsa_kernel(iq_ref, iwt_ref, ik_ref, q_ref, k_hbm, vt_hbm, o_ref,
                u_scr, ub_scr, kbuf, vbuf, sem, m_scr, l_scr, acc_scr, s_scr, bias_scr, iqt_scr, qt_scr,
                *, past, seq_len, n_queries, topk, tq, tk):
    b = pl.program_id(0)
    i = pl.program_id(1)
    nkt = u_scr.shape[0]

    last_chunk = (past + (i + 1) * tq - 1) // CHUNK
    n_adm = jnp.minimum(seq_len, (last_chunk + 1) * CHUNK)
    nk = jnp.minimum((n_adm + tk - 1) // tk, nkt)

    def k_copy(j):
        off = pl.multiple_of(j * tk, tk)
        return pltpu.make_async_copy(k_hbm.at[b, pl.ds(off, tk)], kbuf.at[j % 2], sem.at[0, j % 2])

    def v_copy(j):
        off = pl.multiple_of(j * tk, tk)
        return pltpu.make_async_copy(vt_hbm.at[b, :, pl.ds(off, tk)], vbuf.at[j % 2], sem.at[1, j % 2])

    k_copy(0).start()
    v_copy(0).start()

    for c in range(N_ATTN_HEADS):
        cs = slice(c * LANES, (c + 1) * LANES)
        qt_scr[c] = q_ref[0, :, cs].astype(F32).T.astype(qt_scr.dtype)
        iqt_scr[c] = iq_ref[0, :, cs].astype(F32).T.astype(iqt_scr.dtype)

    q_chunk = (past + i * tq + lax.broadcasted_iota(jnp.int32, (tk, tq), 1)) // CHUNK
    key_iota = lax.broadcasted_iota(jnp.int32, (tk, tq), 0)
    idx_scale = (IDX_DIM * N_IDX_HEADS) ** -0.5

    def score_tile(j, carry):
        ik_t = ik_ref[0, pl.ds(pl.multiple_of(j * tk, tk), tk), :]
        acc = jnp.zeros((tk, tq), F32)
        for hh in range(N_IDX_HEADS):
            iq_h = iqt_scr[hh // 2, (hh % 2) * IDX_DIM:(hh % 2 + 1) * IDX_DIM, :]
            logits = jnp.dot(ik_t, iq_h, preferred_element_type=F32)
            acc = acc + iwt_ref[0, hh:hh + 1, :] * jnp.maximum(logits, 0.0)
        kpos = j * tk + key_iota
        adm = (kpos // CHUNK <= q_chunk) & (kpos < seq_len)
        u = jnp.where(adm, acc * idx_scale, -jnp.inf)
        u_scr[j] = u
        ub_scr[j] = u.astype(ub_scr.dtype)
        return carry

    lax.fori_loop(0, nk, score_tile, 0)

    group = 2 if nkt % 2 == 0 else 1
    n_groups = (nk + group - 1) // group

    @pl.when(n_groups * group > nk)
    def _():
        u_scr[nk] = jnp.full((tk, tq), -jnp.inf, F32)
        ub_scr[nk] = jnp.full((tk, tq), -jnp.inf, ub_scr.dtype)

    def count_ge(scr, thr_row):
        rows = SUBLANES * (4 // jnp.dtype(scr.dtype).itemsize)
        thr_b = jnp.broadcast_to(thr_row.astype(scr.dtype), (rows, tq))
        one, zero = jnp.ones((), scr.dtype), jnp.zeros((), scr.dtype)

        def body(g, cnt):
            for r in range(group):
                x = scr[g * group + r].reshape(tk // rows, rows, tq)
                hit = jnp.where(x >= thr_b[None], one, zero)
                parts = [hit[n] for n in range(tk // rows)]
                while len(parts) > 1:
                    parts = [parts[n] + parts[n + 1] for n in range(0, len(parts), 2)]
                cnt = cnt + parts[0].astype(F32)
            return cnt

        cnt = lax.fori_loop(0, n_groups, body, jnp.zeros((rows, tq), F32))
        return jnp.sum(cnt, axis=0, keepdims=True)

    int_min = jnp.int32(-2 ** 31)
    kf = float(topk)

    def coarse_value(t_biased):
        mono = t_biased ^ int_min
        bits = jnp.where(mono >= 0, mono, mono ^ jnp.int32(0x7FFF0000))
        return lax.bitcast_convert_type(bits, F32)

    def coarse_step(it, t_biased):
        cand = t_biased | lax.shift_left(jnp.int32(1), jnp.int32(31) - it)
        return jnp.where(count_ge(ub_scr, coarse_value(cand)) >= kf, cand, t_biased)

    t16 = lax.fori_loop(0, 16, coarse_step, jnp.zeros((1, tq), jnp.int32))
    few = (t16 >= 0) & (t16 <= jnp.int32(0x007F0000))

    def to_mono(bits):
        return jnp.where(bits >= 0, bits, bits ^ jnp.int32(0x7FFFFFFF))

    c_mono = to_mono(lax.bitcast_convert_type(coarse_value(t16), jnp.int32))
    lo0 = c_mono - jnp.int32(0x8000)
    hi0 = c_mono + jnp.int32(0x10000)
    padded = i * tq + lax.broadcasted_iota(jnp.int32, (1, tq), 1) >= n_queries
    hi0 = jnp.where(few | padded, lo0, hi0)

    def fine_cond(state):
        it, _, _, pending = state
        return (it < FINE_STEPS) & (pending > 0)

    def fine_step(state):
        it, lo, hi, _ = state
        mid = lo + lax.shift_right_arithmetic(hi - lo + 1, jnp.int32(1))
        total = count_ge(u_scr, lax.bitcast_convert_type(to_mono(mid), F32))
        ok = total >= kf
        lo = jnp.where(ok, mid, lo)
        hi = jnp.where(total == kf, mid, jnp.where(ok, hi, mid - 1))
        return it + 1, lo, hi, jnp.max(jnp.where(hi > lo, 1, 0))

    _, lo_fin, _, _ = lax.while_loop(fine_cond, fine_step,
                                     (jnp.int32(0), lo0, hi0, jnp.max(jnp.where(hi0 > lo0, 1, 0))))
    thr_fin = lax.bitcast_convert_type(to_mono(lo_fin), F32)
    thr = jnp.where(few, F32_LOWEST, jnp.maximum(thr_fin, F32_LOWEST))

    m_scr[...] = jnp.full(m_scr.shape, MASK_NEG, F32)
    l_scr[...] = jnp.zeros(l_scr.shape, F32)
    acc_scr[...] = jnp.zeros(acc_scr.shape, F32)

    def select_bias(j):
        bias_scr[...] = jnp.where(u_scr[j] >= thr, 0.0, MASK_NEG)

    n_slots = s_scr.shape[0]

    def scores_head(j, hh):
        hs = slice(hh * LANES, (hh + 1) * LANES)
        s_scr[hh % n_slots] = (jnp.dot(kbuf[j % 2, :, hs], qt_scr[hh], preferred_element_type=F32)
                               + bias_scr[...])

    def attend_head(j, hh):
        hs = slice(hh * LANES, (hh + 1) * LANES)
        s = s_scr[hh % n_slots]
        m_prev = m_scr[hh:hh + 1, :]
        m_new = jnp.maximum(m_prev, jnp.max(s, axis=0, keepdims=True))
        alpha = jnp.exp2(m_prev - m_new)
        p = jnp.exp2(s - m_new)
        l_scr[hh:hh + 1, :] = alpha * l_scr[hh:hh + 1, :] + jnp.sum(p, axis=0, keepdims=True)
        acc_scr[hs, :] = alpha * acc_scr[hs, :] + jnp.dot(vbuf[j % 2, hs, :], p.astype(BF16),
                                                          preferred_element_type=F32)
        m_scr[hh:hh + 1, :] = m_new

    def tile_steps(j, has_next):
        v_copy(j).wait()
        for hh in range(N_ATTN_HEADS):
            ahead = hh + SCORE_LOOKAHEAD
            if ahead < N_ATTN_HEADS:
                scores_head(j, ahead)
            elif has_next:
                if ahead == N_ATTN_HEADS:
                    @pl.when(j + 2 < nk)
                    def _():
                        k_copy(j + 2).start()

                    k_copy(j + 1).wait()
                    select_bias(j + 1)
                scores_head(j + 1, ahead - N_ATTN_HEADS)
            attend_head(j, hh)

    @pl.when(nk > 1)
    def _():
        k_copy(1).start()

    k_copy(0).wait()
    select_bias(0)
    for hh in range(SCORE_LOOKAHEAD):
        scores_head(0, hh)

    def attend_tile(j, carry):
        v_copy(j + 1).start()
        tile_steps(j, True)
        return carry

    lax.fori_loop(0, nk - 1, attend_tile, 0)
    tile_steps(nk - 1, False)

    for hh in range(N_ATTN_HEADS):
        hs = slice(hh * LANES, (hh + 1) * LANES)
        o_t = acc_scr[hs, :] / l_scr[hh:hh + 1, :]
        o_ref[0, :, hs] = o_t.T.astype(o_ref.dtype)


def _dsa(iq, iw_t, ik_bf, q_bf, k_bf, v_t, *, past, seq_len, n_queries, topk, tq, tk):
    b, t, hd = q_bf.shape
    lp = ik_bf.shape[1]
    nkt = lp // tk
    kern = functools.partial(_dsa_kernel, past=past, seq_len=seq_len, n_queries=n_queries, topk=topk,
                             tq=tq, tk=tk)
    return pl.pallas_call(
        kern,
        out_shape=jax.ShapeDtypeStruct((b, t, hd), BF16),
        grid=(b, t // tq),
        in_specs=[
            pl.BlockSpec((1, tq, N_IDX_HEADS * IDX_DIM), lambda bi, i: (bi, i, 0)),
            pl.BlockSpec((1, N_IDX_HEADS, tq), lambda bi, i: (bi, 0, i)),
            pl.BlockSpec((1, lp, IDX_DIM), lambda bi, i: (bi, 0, 0)),
            pl.BlockSpec((1, tq, hd), lambda bi, i: (bi, i, 0)),
            pl.BlockSpec(memory_space=pl.ANY),
            pl.BlockSpec(memory_space=pl.ANY),
        ],
        out_specs=pl.BlockSpec((1, tq, hd), lambda bi, i: (bi, i, 0)),
        scratch_shapes=[
            pltpu.VMEM((nkt, tk, tq), F32),
            pltpu.VMEM((nkt, tk, tq), jnp.bfloat16),
            pltpu.VMEM((2, tk, hd), BF16),
            pltpu.VMEM((2, hd, tk), BF16),
            pltpu.SemaphoreType.DMA((2, 2)),
            pltpu.VMEM((N_ATTN_HEADS, tq), F32),
            pltpu.VMEM((N_ATTN_HEADS, tq), F32),
            pltpu.VMEM((hd, tq), F32),
            pltpu.VMEM((SCORE_SLOTS, tk, tq), F32),
            pltpu.VMEM((tk, tq), F32),
            pltpu.VMEM((N_IDX_HEADS * IDX_DIM // LANES, LANES, tq), BF16),
            pltpu.VMEM((N_ATTN_HEADS, ATTN_HEAD_DIM, tq), BF16),
        ],
        compiler_params=_cparams(("arbitrary", "arbitrary")),
        name="dsa_attend",
    )(iq, iw_t, ik_bf, q_bf, k_bf, v_t)


def _gla_kernel(q_ref, k_ref, v_ref, lg_ref, gr_ref, g_ref, s0_ref, o_ref, sT_out, st_scr, *, n_chunks):
    tstep = pl.program_id(1)

    @pl.when(tstep == 0)
    def _():
        st_scr[...] = s0_ref[0]

    row = lax.broadcasted_iota(jnp.int32, (CHUNK, GLA_DK), 0)
    tri = (lax.broadcasted_iota(jnp.int32, (CHUNK, CHUNK), 0)
           >= lax.broadcasted_iota(jnp.int32, (CHUNK, CHUNK), 1))

    for c in range(n_chunks):
        rs = slice(c * CHUNK, (c + 1) * CHUNK)
        for hh in range(GLA_HEADS):
            ks = slice(hh * GLA_DK, (hh + 1) * GLA_DK)
            vs = slice(hh * GLA_DV, (hh + 1) * GLA_DV)
            bcum = lg_ref[0, rs, ks]
            shift = 1
            while shift < CHUNK:
                bcum = bcum + jnp.where(row >= shift, pltpu.roll(bcum, shift, 0), 0.0)
                shift *= 2
            bl = bcum[CHUNK - 1:CHUNK, :]
            q = q_ref[0, rs, ks]
            k = k_ref[0, rs, ks]
            v = v_ref[0, rs, vs]
            qe = (q * jnp.exp(bcum)).astype(BF16)
            ke = (k * jnp.exp(-bcum)).astype(BF16)
            kd = (k * jnp.exp(bl - bcum)).astype(BF16)
            st = st_scr[hh]
            inter = lax.dot_general(qe, st.astype(BF16), (((1,), (1,)), ((), ())),
                                    preferred_element_type=F32)
            a = lax.dot_general(qe, ke, (((1,), (1,)), ((), ())), preferred_element_type=F32)
            a = jnp.where(tri, a, 0.0)
            o = inter + jnp.dot(a.astype(BF16), v, preferred_element_type=F32)
            st_scr[hh] = st * jnp.exp(bl) + lax.dot_general(v, kd, (((0,), (0,)), ((), ())),
                                                            preferred_element_type=F32)
            n = o * lax.rsqrt(jnp.mean(o * o, axis=-1, keepdims=True) + EPS) * g_ref[...]
            o_ref[0, rs, vs] = (n * _silu(gr_ref[0, rs, vs])).astype(o_ref.dtype)

    @pl.when(tstep == pl.num_programs(1) - 1)
    def _():
        sT_out[0] = st_scr[...]


def _gla(gq, gk, gv, lg, gr, g_gla, s0t, tg):
    b, t, _ = gq.shape
    kern = functools.partial(_gla_kernel, n_chunks=tg // CHUNK)
    tok = lambda w: pl.BlockSpec((1, tg, w), lambda bi, m: (bi, m, 0))
    wk, wv = GLA_HEADS * GLA_DK, GLA_HEADS * GLA_DV
    st_spec = pl.BlockSpec((1, GLA_HEADS, GLA_DV, GLA_DK), lambda bi, m: (bi, 0, 0, 0))
    return pl.pallas_call(
        kern,
        out_shape=(jax.ShapeDtypeStruct((b, t, wv), BF16),
                   jax.ShapeDtypeStruct((b, GLA_HEADS, GLA_DV, GLA_DK), F32)),
        grid=(b, t // tg),
        in_specs=[tok(wk), tok(wk), tok(wv), tok(wk), tok(wv),
                  pl.BlockSpec((1, GLA_DV), lambda bi, m: (0, 0)), st_spec],
        out_specs=(tok(wv), st_spec),
        scratch_shapes=[pltpu.VMEM((GLA_HEADS, GLA_DV, GLA_DK), F32)],
        compiler_params=_cparams(("arbitrary", "arbitrary")),
        name="gla",
    )(gq, gk, gv, lg, gr, g_gla.reshape(1, GLA_DV), s0t)


def _outproj_kernel(oa_ref, og_ref, wa_ref, wg_ref, x_ref, mod_ref, gffn_ref, x1_out, h2_out):
    bb, tb, d = x_ref.shape
    rows = bb * tb
    oa = oa_ref[...].reshape(rows, oa_ref.shape[-1])
    og = og_ref[...].reshape(rows, og_ref.shape[-1])
    mix = (jnp.dot(oa, wa_ref[...], preferred_element_type=F32)
           + jnp.dot(og, wg_ref[...], preferred_element_type=F32)).reshape(bb, tb, d)
    mod = mod_ref[...]
    x1 = x_ref[...] + mod[:, :, 2 * d:3 * d] * mix
    x1_out[...] = x1
    y = x1 * lax.rsqrt(jnp.mean(x1 * x1, axis=-1, keepdims=True) + EPS) * gffn_ref[...]
    h2_out[...] = (y * (1.0 + mod[:, :, 4 * d:5 * d]) + mod[:, :, 3 * d:4 * d]).astype(BF16)


def _outproj(o_a, o_g, wout_a, wout_g, x, mod, g_ffn, bb, tb):
    b, t, d = x.shape
    tokspec = lambda w: pl.BlockSpec((bb, tb, w), lambda i, m: (i, m, 0))
    const = lambda shp: pl.BlockSpec(shp, lambda i, m: tuple(0 for _ in shp))
    return pl.pallas_call(
        _outproj_kernel,
        out_shape=(jax.ShapeDtypeStruct((b, t, d), F32), jax.ShapeDtypeStruct((b, t, d), BF16)),
        grid=(b // bb, t // tb),
        in_specs=[tokspec(o_a.shape[-1]), tokspec(o_g.shape[-1]), const(wout_a.shape), const(wout_g.shape),
                  tokspec(d), pl.BlockSpec((bb, 1, mod.shape[-1]), lambda i, m: (i, 0, 0)), const((1, 1, d))],
        out_specs=(tokspec(d), tokspec(d)),
        compiler_params=_cparams(("arbitrary", "arbitrary")),
        name="outproj_norm",
    )(o_a, o_g, wout_a, wout_g, x, mod, g_ffn.reshape(1, 1, d))


def _ffn_kernel(h_ref, halo_ref, wa_ref, wb_ref, wca_ref, wcb_ref, bca_ref, bcb_ref, sa_ref, sb_ref,
                wd_ref, x1_ref, mod_ref, y_out, nba_out, nbb_out, acc_scr, hext_scr, act_scr):
    m = pl.program_id(1)
    f = pl.program_id(2)
    nf = pl.num_programs(2) - 1
    bb, tb, d = h_ref.shape
    rows = bb * tb
    tf = wa_ref.shape[-1]
    halo = halo_ref.shape[1]

    @pl.when(f == 0)
    def _():
        hext_scr[:, 0:halo, :] = halo_ref[...]
        hext_scr[:, halo:, :] = h_ref[...]
        acc_scr[...] = jnp.zeros(acc_scr.shape, F32)
        act_scr[1] = jnp.zeros(act_scr.shape[1:], act_scr.dtype)

    def conv_branch(hext, tpos, w_ref, wc_ref, bc_ref, s_ref, nb_out):
        u_ext = jnp.dot(hext, w_ref[...], preferred_element_type=F32).reshape(bb, tb + halo, tf)
        u = u_ext[:, halo:, :].reshape(rows, tf)
        prev = jnp.where(m == 0, s_ref[...], u_ext[:, halo - (CONV_W - 1):halo, :])
        bcast = lambda r: jnp.broadcast_to(prev[:, r:r + 1, :], (bb, tb, tf)).reshape(rows, tf)
        p0, p1 = bcast(0), bcast(1)
        u1 = jnp.where(tpos == 0, p1, pltpu.roll(u, 1, 0))
        u2 = jnp.where(tpos == 0, p0, jnp.where(tpos == 1, p1, pltpu.roll(u, 2, 0)))
        wc = wc_ref[...]
        nb_out[:, 0] = u.reshape(bb, tb, tf)[:, tb - (CONV_W - 1):, :]
        return bc_ref[...] + wc[0:1] * u2 + wc[1:2] * u1 + wc[2:3] * u

    @pl.when(f < nf)
    def _():
        act_prev = act_scr[(f + 1) % 2]
        hext = hext_scr[...].reshape(bb * (tb + halo), d)
        tpos = lax.broadcasted_iota(jnp.int32, (rows, tf), 0) % tb
        ua = conv_branch(hext, tpos, wa_ref, wca_ref, bca_ref, sa_ref, nba_out)
        ub = conv_branch(hext, tpos, wb_ref, wcb_ref, bcb_ref, sb_ref, nbb_out)
        acc_scr[...] += jnp.dot(act_prev, wd_ref[...], preferred_element_type=F32)
        act_scr[f % 2] = (_silu(ua) * ub).astype(act_scr.dtype)

    @pl.when(f == nf)
    def _():
        acc = acc_scr[...] + jnp.dot(act_scr[(f + 1) % 2], wd_ref[...], preferred_element_type=F32)
        mod = mod_ref[...]
        y_out[...] = x1_ref[...] + mod[:, :, 5 * d:6 * d] * acc.reshape(bb, tb, d)


def _ffn(h2, w_up, w_conv, b_conv, conv_state, w_down, x1, mod, bb, tb):
    b, t, d = x1.shape
    dff = w_down.shape[0]
    tf = FFN_TILE
    nf = dff // tf
    halo = 2 * SUBLANES
    hb = tb // halo
    tokspec = lambda w: pl.BlockSpec((bb, tb, w), lambda i, m, f: (i, m, 0))
    up = lambda f: jnp.minimum(f, nf - 1)
    a_col = lambda r: pl.BlockSpec((r, tf), lambda i, m, f: (0, up(f)))
    b_col = lambda r: pl.BlockSpec((r, tf), lambda i, m, f: (0, nf + up(f)))
    st_a = pl.BlockSpec((bb, CONV_W - 1, tf), lambda i, m, f: (i, 0, up(f)))
    st_b = pl.BlockSpec((bb, CONV_W - 1, tf), lambda i, m, f: (i, 0, nf + up(f)))
    nb_spec = pl.BlockSpec((bb, 1, CONV_W - 1, tf), lambda i, m, f: (i, m, 0, up(f)))
    bc2 = b_conv.reshape(1, 2 * dff)
    return pl.pallas_call(
        _ffn_kernel,
        out_shape=(jax.ShapeDtypeStruct((b, t, d), F32),
                   jax.ShapeDtypeStruct((b, t // tb, CONV_W - 1, dff), F32),
                   jax.ShapeDtypeStruct((b, t // tb, CONV_W - 1, dff), F32)),
        grid=(b // bb, t // tb, nf + 1),
        in_specs=[
            tokspec(d),
            pl.BlockSpec((bb, halo, d), lambda i, m, f: (i, jnp.maximum(m * hb - 1, 0), 0)),
            a_col(d), b_col(d), a_col(CONV_W), b_col(CONV_W), a_col(1), b_col(1), st_a, st_b,
            pl.BlockSpec((tf, d), lambda i, m, f: (jnp.maximum(f - 1, 0), 0)),
            tokspec(d),
            pl.BlockSpec((bb, 1, mod.shape[-1]), lambda i, m, f: (i, 0, 0)),
        ],
        out_specs=(tokspec(d), nb_spec, nb_spec),
        scratch_shapes=[pltpu.VMEM((bb * tb, d), F32), pltpu.VMEM((bb, tb + halo, d), BF16),
                        pltpu.VMEM((2, bb * tb, tf), BF16)],
        compiler_params=_cparams(("arbitrary", "arbitrary", "arbitrary")),
        name="conv_ffn",
    )(h2, h2, w_up, w_up, w_conv, w_conv, bc2, bc2, conv_state, conv_state, w_down, x1, mod)


def _prep_weights(w_in, w_gate2, w_out, w_up, w_down):
    hd = N_ATTN_HEADS * ATTN_HEAD_DIM
    sizes = (hd, hd, hd, N_IDX_HEADS * IDX_DIM, IDX_DIM, N_IDX_HEADS,
             GLA_HEADS * GLA_DK, GLA_HEADS * GLA_DK, GLA_HEADS * GLA_DV, GLA_HEADS * GLA_DV, GLA_GATE_RANK)
    offs = np.concatenate([[0], np.cumsum(sizes)])
    col = lambda n: w_in[:, int(offs[n]):int(offs[n + 1])]
    wbig = jnp.concatenate([col(0), col(1), col(2), col(3), col(6), col(7), col(8), col(9)], axis=1).astype(BF16)
    pad = LANES - (IDX_DIM + N_IDX_HEADS + GLA_GATE_RANK)
    wsmall = jnp.concatenate([col(4), col(5), col(10), jnp.zeros((w_in.shape[0], pad), w_in.dtype)],
                             axis=1).astype(BF16)
    lo = IDX_DIM + N_IDX_HEADS
    wg2p = jnp.zeros((LANES, w_gate2.shape[1]), F32).at[lo:lo + GLA_GATE_RANK].set(w_gate2).astype(BF16)
    return dict(wbig=wbig, wsmall=wsmall, wg2p=wg2p,
                wout_a=w_out[:hd].astype(BF16), wout_g=w_out[hd:].astype(BF16),
                w_up=w_up.astype(BF16), w_down=w_down.astype(BF16))


def _layer(x, mod, past, past_k, past_v, past_ik, gla_s0, conv_state, lw, pw, *, bb, tb, tq, tk, tg):
    (g_mix, g_ffn, g_q, g_k, b_gate2, g_gla, w_conv, b_conv) = lw
    b, t, d = x.shape
    hd = N_ATTN_HEADS * ATTN_HEAD_DIM
    pos = past + jnp.arange(t, dtype=jnp.int32)
    ropea = _rope_table(pos, ROPE_DIM // 2, LANES, LANES)
    ropei = _rope_table(pos, IDX_ROPE_DIM // 2, IDX_DIM, LANES)
    ropes = _rope_table(pos, IDX_ROPE_DIM // 2, LANES, IDX_DIM)

    (q_bf, k, v, iq, small, lg, gq, gk, gv, gr) = _inproj(
        x, mod, g_mix, pw["wbig"], pw["wsmall"], g_q, g_k, ropea, ropei, ropes, pw["wg2p"], b_gate2, bb, tb)
    ik = small[:, :, :IDX_DIM]
    iw = small[:, :, IDX_DIM:IDX_DIM + N_IDX_HEADS]

    k_bf, v_bf, ik_bf = k.astype(BF16), v.astype(BF16), ik.astype(BF16)
    if past_k is not None:
        k_bf = jnp.concatenate([past_k.reshape(b, past, hd).astype(BF16), k_bf], axis=1)
        v_bf = jnp.concatenate([past_v.reshape(b, past, hd).astype(BF16), v_bf], axis=1)
        ik_bf = jnp.concatenate([past_ik.astype(BF16), ik_bf], axis=1)
    seq_len = past + t
    lp = -(-seq_len // tk) * tk
    if lp != seq_len:
        padw = ((0, 0), (0, lp - seq_len), (0, 0))
        k_bf, v_bf, ik_bf = jnp.pad(k_bf, padw), jnp.pad(v_bf, padw), jnp.pad(ik_bf, padw)
    tpad = -(-t // tq) * tq
    qpad = lambda a: jnp.pad(a, ((0, 0), (0, tpad - t), (0, 0))) if tpad != t else a
    iw_t = qpad(iw).transpose(0, 2, 1)
    v_t = v_bf.transpose(0, 2, 1)
    topk = min(TOPK_MAX, seq_len // 4)
    o_a = _dsa(qpad(iq), iw_t, ik_bf, qpad(q_bf), k_bf, v_t, past=past, seq_len=seq_len, n_queries=t,
               topk=topk, tq=tq, tk=tk)
    o_a = o_a[:, :t]

    s0t = jnp.swapaxes(gla_s0, -1, -2)
    o_g, s_t = _gla(gq, gk, gv, lg, gr, g_gla, s0t, tg)
    s_new = jnp.swapaxes(s_t, -1, -2)

    x1, h2 = _outproj(o_a, o_g, pw["wout_a"], pw["wout_g"], x, mod, g_ffn, bb, tb)
    y, nb_a, nb_b = _ffn(h2, pw["w_up"], w_conv, b_conv, conv_state, pw["w_down"], x1, mod, bb, tb)
    new_buf = jnp.concatenate([nb_a[:, -1], nb_b[:, -1]], axis=-1)
    return (y, k.reshape(b, t, N_ATTN_HEADS, ATTN_HEAD_DIM), v.reshape(b, t, N_ATTN_HEADS, ATTN_HEAD_DIM),
            ik, s_new, new_buf)


def _tiles(b, t, seq_len):
    if t >= 512:
        bb, tb = 1, 512
    else:
        bb, tb = min(b, 512 // t), t
    tq = 256 if t % 256 == 0 else LANES
    tk = 256 if seq_len % 256 == 0 else LANES
    tg = min(t, 256)
    return dict(bb=bb, tb=tb, tq=tq, tk=tk, tg=tg)


def kernel(x_prompt, x_sample, c_prompt, c_sample, cache_k, cache_v, cache_idx_k, state_gla, state_ffn_conv,
           w_ada, b_ada, g_mix, g_ffn, w_in, g_q, g_k, w_gate2, b_gate2, g_gla, w_out, w_up, w_conv, b_conv,
           w_down):
    bp, s, d = x_prompt.shape
    bs, ts, _ = x_sample.shape
    depth = w_ada.shape[0]
    past = cache_k.shape[2]
    dff = w_down.shape[1]
    dt = x_prompt.dtype

    y_p, y_s = x_prompt, x_sample
    outs = [[] for _ in range(10)]
    for l in range(depth):
        n_c = bp + bs
        rows = -(-n_c // SUBLANES) * SUBLANES
        c_all = jnp.concatenate([c_prompt, c_sample, jnp.zeros((rows - n_c, d), dt)], axis=0)
        mod = _ada(c_all, w_ada[l], b_ada[l])
        mod_p = mod[:bp].reshape(bp, 1, 6 * d)
        mod_s = mod[bp:n_c].reshape(bs, 1, 6 * d)
        pw = _prep_weights(w_in[l], w_gate2[l], w_out[l], w_up[l], w_down[l])
        lw = (g_mix[l], g_ffn[l], g_q[l], g_k[l], b_gate2[l], g_gla[l], w_conv[l], b_conv[l])

        gla0 = jnp.zeros((bp, GLA_HEADS, GLA_DK, GLA_DV), dt)
        conv0 = jnp.zeros((bp, CONV_W - 1, 2 * dff), dt)
        y_p, k1, v1, ik1, s1, c1 = _layer(y_p, mod_p, 0, None, None, None, gla0, conv0, lw, pw,
                                          **_tiles(bp, s, s))
        y_s, k2, v2, ik2, s2, c2 = _layer(y_s, mod_s, past, cache_k[l], cache_v[l], cache_idx_k[l],
                                          state_gla[l], state_ffn_conv[l], lw, pw,
                                          **_tiles(bs, ts, past + ts))
        for lst, val in zip(outs, (k1, v1, ik1, s1, c1, k2, v2, ik2, s2, c2)):
            lst.append(val)

    return (y_p, y_s) + tuple(jnp.stack(o) for o in outs)
```

```python
import functools
import math

import jax
import jax.numpy as jnp
import numpy as np
from jax import lax
from jax.experimental import pallas as pl
from jax.experimental.pallas import tpu as pltpu

F32 = jnp.float32
BF16 = jnp.bfloat16

CHUNK = 64
N_ATTN_HEADS = 8
ATTN_HEAD_DIM = 128
ROPE_DIM = 32
ROPE_THETA = 500000.0
N_IDX_HEADS = 16
IDX_DIM = 64
IDX_ROPE_DIM = 16
TOPK_MAX = 256
GLA_HEADS = 4
GLA_DK = 128
GLA_DV = 256
GLA_GATE_RANK = 16
GLA_GATE_TEMP = 16.0
CONV_W = 3
EPS = 1e-6

LANES = 128
SUBLANES = 8
VMEM_LIMIT_BYTES = 56 * 1024 * 1024

SCORE_LOOKAHEAD = 4
SCORE_SLOTS = 8
FINE_STEPS = 17
PROJ_GROUP = 1024
FFN_TILE = 512

MASK_NEG = -1e30
F32_LOWEST = -3.4028235e38


def _cparams(sem):
    return pltpu.CompilerParams(dimension_semantics=sem, vmem_limit_bytes=VMEM_LIMIT_BYTES)


def _sigmoid(x):
    return 1.0 / (1.0 + jnp.exp(-x))


def _silu(x):
    return x * _sigmoid(x)


def _ada_kernel(c_ref, w_ref, b_ref, o_ref):
    s = _silu(c_ref[...]).astype(BF16)
    o_ref[...] = jnp.dot(s, w_ref[...].astype(BF16), preferred_element_type=F32) + b_ref[...]


def _ada(c_all, w_ada, b_ada):
    rows, d = c_all.shape
    n = w_ada.shape[1]
    tn = 1536 if n % 1536 == 0 else n
    return pl.pallas_call(
        _ada_kernel,
        out_shape=jax.ShapeDtypeStruct((rows, n), F32),
        grid=(n // tn,),
        in_specs=[
            pl.BlockSpec((rows, d), lambda j: (0, 0)),
            pl.BlockSpec((d, tn), lambda j: (0, j)),
            pl.BlockSpec((1, tn), lambda j: (0, j)),
        ],
        out_specs=pl.BlockSpec((rows, tn), lambda j: (0, j)),
        compiler_params=_cparams(("arbitrary",)),
        name="ada_mod",
    )(c_all, w_ada, b_ada.reshape(1, n))


def _rope_table(pos, half, period, active_lanes):
    inv = ROPE_THETA ** (-jnp.arange(half, dtype=F32) / half)
    lane = np.arange(LANES)
    within = lane % period
    first = jnp.asarray((within < half) & (lane < active_lanes))
    second = jnp.asarray((within >= half) & (within < 2 * half) & (lane < active_lanes))
    ang = pos.astype(F32)[:, None] * inv[jnp.asarray(within % half)][None, :]
    cos, sin = jnp.cos(ang), jnp.sin(ang)
    c = jnp.where(first | second, cos, 1.0)
    s1 = jnp.where(first, -sin, 0.0)
    s2 = jnp.where(second, sin, 0.0)
    return jnp.stack([c, s1, s2]).astype(F32)


def _rope_apply(v, tab, half):
    return (v * tab[0] + pltpu.roll(v, LANES - half, 1) * tab[1] + pltpu.roll(v, half, 1) * tab[2])


def _tile_rows(tab_ref, bb):
    t = tab_ref[...]
    if bb == 1:
        return t
    tb = t.shape[1]
    return jnp.broadcast_to(t[:, None], (3, bb, tb, LANES)).reshape(3, bb * tb, LANES)


def _inproj_kernel(x_ref, mod_ref, gmix_ref, wbig_ref, wsmall_ref, gq_ref, gk_ref,
                   ropea_ref, ropei_ref, ropes_ref, wg2_ref, bg2_ref,
                   q_out, k_out, v_out, iq_out, small_out, lg_out, gq_out, gk_out, gv_out, gr_out,
                   h_scr):
    j = pl.program_id(2)
    bb, tb, d = x_ref.shape
    rows = bb * tb

    @pl.when(j == 0)
    def _():
        x = x_ref[...]
        ms = jnp.mean(x * x, axis=-1, keepdims=True)
        y = x * lax.rsqrt(ms + EPS) * gmix_ref[...]
        mod = mod_ref[...]
        h = y * (1.0 + mod[:, :, d:2 * d]) + mod[:, :, 0:d]
        hb = h.astype(BF16).reshape(rows, d)
        h_scr[...] = hb
        small = jnp.dot(hb, wsmall_ref[...], preferred_element_type=F32)
        small = _rope_apply(small, _tile_rows(ropes_ref, bb), IDX_ROPE_DIM // 2)
        small_out[...] = small.reshape(bb, tb, LANES)
        pre = jnp.dot(small.astype(BF16), wg2_ref[...], preferred_element_type=F32) + bg2_ref[...]
        lsig = jnp.minimum(pre, 0.0) - jnp.log(1.0 + jnp.exp(-jnp.abs(pre)))
        lg_out[...] = (lsig / GLA_GATE_TEMP).reshape(bb, tb, lg_out.shape[-1])

    y = jnp.dot(h_scr[...], wbig_ref[...], preferred_element_type=F32)

    def headnorm_rope(g_ref, scale):
        tab = _tile_rows(ropea_ref, bb)
        outs = []
        for hh in range(N_ATTN_HEADS):
            yh = y[:, hh * LANES:(hh + 1) * LANES]
            n = yh * lax.rsqrt(jnp.mean(yh * yh, axis=-1, keepdims=True) + EPS) * g_ref[...]
            r = _rope_apply(n, tab, ROPE_DIM // 2)
            outs.append(r * scale if scale != 1.0 else r)
        return outs

    @pl.when(j == 0)
    def _():
        for hh, r in enumerate(headnorm_rope(gq_ref, ATTN_HEAD_DIM ** -0.5 * math.log2(math.e))):
            q_out[:, :, hh * LANES:(hh + 1) * LANES] = r.astype(BF16).reshape(bb, tb, LANES)

    @pl.when(j == 1)
    def _():
        for hh, r in enumerate(headnorm_rope(gk_ref, 1.0)):
            k_out[:, :, hh * LANES:(hh + 1) * LANES] = r.reshape(bb, tb, LANES)

    @pl.when(j == 2)
    def _():
        v_out[...] = y.reshape(bb, tb, y.shape[-1])

    @pl.when(j == 3)
    def _():
        tab = _tile_rows(ropei_ref, bb)
        for c in range(y.shape[-1] // LANES):
            r = _rope_apply(y[:, c * LANES:(c + 1) * LANES], tab, IDX_ROPE_DIM // 2)
            iq_out[:, :, c * LANES:(c + 1) * LANES] = r.astype(BF16).reshape(bb, tb, LANES)

    @pl.when(j == 4)
    def _():
        half = y.shape[-1] // 2
        gq_out[...] = (y[:, :half] * GLA_DK ** -0.5).reshape(bb, tb, half)
        gk_out[...] = y[:, half:].reshape(bb, tb, half)

    @pl.when(j == 5)
    def _():
        gv_out[...] = y.astype(BF16).reshape(bb, tb, y.shape[-1])

    @pl.when(j == 6)
    def _():
        gr_out[...] = y.reshape(bb, tb, y.shape[-1])


def _inproj(x, mod, g_mix, wbig, wsmall, g_q, g_k, ropea, ropei, ropes, wg2p, bg2, bb, tb):
    b, t, d = x.shape
    gw = PROJ_GROUP
    ngroups = wbig.shape[1] // gw
    glak = GLA_HEADS * GLA_DK
    rows = bb * tb
    tok = lambda w, dt: jax.ShapeDtypeStruct((b, t, w), dt)
    tokspec = lambda w: pl.BlockSpec((bb, tb, w), lambda i, m, j: (i, m, 0))
    const = lambda shp: pl.BlockSpec(shp, lambda i, m, j: tuple(0 for _ in shp))
    tabspec = pl.BlockSpec((3, tb, LANES), lambda i, m, j: (0, m, 0))
    out_shapes = (tok(1024, BF16), tok(1024, F32), tok(1024, F32), tok(1024, BF16), tok(LANES, F32),
                  tok(glak, F32), tok(glak, F32), tok(glak, F32), tok(1024, BF16), tok(1024, F32))
    out_specs = tuple(tokspec(s.shape[-1]) for s in out_shapes)
    return pl.pallas_call(
        _inproj_kernel,
        out_shape=out_shapes,
        grid=(b // bb, t // tb, ngroups),
        in_specs=[
            tokspec(d),
            pl.BlockSpec((bb, 1, mod.shape[-1]), lambda i, m, j: (i, 0, 0)),
            const((1, 1, d)),
            pl.BlockSpec((d, gw), lambda i, m, j: (0, j)),
            const((d, LANES)),
            const((1, LANES)), const((1, LANES)),
            tabspec, tabspec, tabspec,
            const((LANES, glak)), const((1, glak)),
        ],
        out_specs=out_specs,
        scratch_shapes=[pltpu.VMEM((rows, d), BF16)],
        compiler_params=_cparams(("arbitrary", "arbitrary", "arbitrary")),
        name="norm_inproj",
    )(x, mod, g_mix.reshape(1, 1, d), wbig, wsmall, g_q.reshape(1, LANES), g_k.reshape(1, LANES),
      ropea, ropei, ropes, wg2p, bg2.reshape(1, glak))


def _dsa_kernel(iq_ref, iwt_ref, ik_ref, q_ref, k_hbm, vt_hbm, o_ref,
                u_scr, ub_scr, kbuf, vbuf, sem, m_scr, l_scr, acc_scr, s_scr, bias_scr, iqt_scr, qt_scr,
                *, past, seq_len, n_queries, topk, tq, tk):
    b = pl.program_id(0)
    i = pl.program_id(1)
    nkt = u_scr.shape[0]

    last_chunk = (past + (i + 1) * tq - 1) // CHUNK
    n_adm = jnp.minimum(seq_len, (last_chunk + 1) * CHUNK)
    nk = jnp.minimum((n_adm + tk - 1) // tk, nkt)

    def k_copy(j):
        off = pl.multiple_of(j * tk, tk)
        return pltpu.make_async_copy(k_hbm.at[b, pl.ds(off, tk)], kbuf.at[j % 2], sem.at[0, j % 2])

    def v_copy(j):
        off = pl.multiple_of(j * tk, tk)
        return pltpu.make_async_copy(vt_hbm.at[b, :, pl.ds(off, tk)], vbuf.at[j % 2], sem.at[1, j % 2])

    k_copy(0).start()
    v_copy(0).start()

    for c in range(N_ATTN_HEADS):
        cs = slice(c * LANES, (c + 1) * LANES)
        qt_scr[c] = q_ref[0, :, cs].astype(F32).T.astype(qt_scr.dtype)
        iqt_scr[c] = iq_ref[0, :, cs].astype(F32).T.astype(iqt_scr.dtype)

    q_chunk = (past + i * tq + lax.broadcasted_iota(jnp.int32, (tk, tq), 1)) // CHUNK
    key_iota = lax.broadcasted_iota(jnp.int32, (tk, tq), 0)
    idx_scale = (IDX_DIM * N_IDX_HEADS) ** -0.5

    group = 2 if nkt % 2 == 0 else 1
    n_groups = (nk + group - 1) // group

    def score_tile(j):
        ik_t = ik_ref[0, pl.ds(pl.multiple_of(j * tk, tk), tk), :]
        acc = jnp.zeros((tk, tq), F32)
        for hh in range(N_IDX_HEADS):
            iq_h = iqt_scr[hh // 2, (hh % 2) * IDX_DIM:(hh % 2 + 1) * IDX_DIM, :]
            logits = jnp.dot(ik_t, iq_h, preferred_element_type=F32)
            acc = acc + iwt_ref[0, hh:hh + 1, :] * jnp.maximum(logits, 0.0)
        kpos = j * tk + key_iota
        adm = (kpos // CHUNK <= q_chunk) & (kpos < seq_len)
        u = jnp.where(adm, acc * idx_scale, -jnp.inf)
        u_scr[j] = u
        ub_scr[j] = u.astype(ub_scr.dtype)

    def score_group(g, carry):
        for r in range(group):
            score_tile(g * group + r)
        return carry

    lax.fori_loop(0, n_groups, score_group, 0)

    def count_ge(scr, thr_row):
        rows = SUBLANES * (4 // jnp.dtype(scr.dtype).itemsize)
        thr_b = jnp.broadcast_to(thr_row.astype(scr.dtype), (rows, tq))
        one, zero = jnp.ones((), scr.dtype), jnp.zeros((), scr.dtype)

        def body(g, cnt):
            for r in range(group):
                x = scr[g * group + r].reshape(tk // rows, rows, tq)
                hit = jnp.where(x >= thr_b[None], one, zero)
                parts = [hit[n] for n in range(tk // rows)]
                while len(parts) > 1:
                    parts = [parts[n] + parts[n + 1] for n in range(0, len(parts), 2)]
                cnt = cnt + parts[0].astype(F32)
            return cnt

        cnt = lax.fori_loop(0, n_groups, body, jnp.zeros((rows, tq), F32))
        return jnp.sum(cnt, axis=0, keepdims=True)

    int_min = jnp.int32(-2 ** 31)
    kf = float(topk)

    def coarse_value(t_biased):
        mono = t_biased ^ int_min
        bits = jnp.where(mono >= 0, mono, mono ^ jnp.int32(0x7FFF0000))
        return lax.bitcast_convert_type(bits, F32)

    def coarse_step(it, t_biased):
        cand = t_biased | lax.shift_left(jnp.int32(1), jnp.int32(31) - it)
        return jnp.where(count_ge(ub_scr, coarse_value(cand)) >= kf, cand, t_biased)

    t16 = lax.fori_loop(0, 16, coarse_step, jnp.zeros((1, tq), jnp.int32))
    few = (t16 >= 0) & (t16 <= jnp.int32(0x007F0000))

    def to_mono(bits):
        return jnp.where(bits >= 0, bits, bits ^ jnp.int32(0x7FFFFFFF))

    c_mono = to_mono(lax.bitcast_convert_type(coarse_value(t16), jnp.int32))
    lo0 = c_mono - jnp.int32(0x8000)
    hi0 = c_mono + jnp.int32(0x10000)
    padded = i * tq + lax.broadcasted_iota(jnp.int32, (1, tq), 1) >= n_queries
    hi0 = jnp.where(few | padded, lo0, hi0)

    def fine_cond(state):
        it, _, _, pending = state
        return (it < FINE_STEPS) & (pending > 0)

    def fine_step(state):
        it, lo, hi, _ = state
        mid = lo + lax.shift_right_arithmetic(hi - lo + 1, jnp.int32(1))
        total = count_ge(u_scr, lax.bitcast_convert_type(to_mono(mid), F32))
        ok = total >= kf
        lo = jnp.where(ok, mid, lo)
        hi = jnp.where(total == kf, mid, jnp.where(ok, hi, mid - 1))
        return it + 1, lo, hi, jnp.max(jnp.where(hi > lo, 1, 0))

    _, lo_fin, _, _ = lax.while_loop(fine_cond, fine_step,
                                     (jnp.int32(0), lo0, hi0, jnp.max(jnp.where(hi0 > lo0, 1, 0))))
    thr_fin = lax.bitcast_convert_type(to_mono(lo_fin), F32)
    thr = jnp.where(few, F32_LOWEST, jnp.maximum(thr_fin, F32_LOWEST))

    m_scr[...] = jnp.full(m_scr.shape, MASK_NEG, F32)
    l_scr[...] = jnp.zeros(l_scr.shape, F32)
    acc_scr[...] = jnp.zeros(acc_scr.shape, F32)

    def select_bias(j):
        bias_scr[...] = jnp.where(u_scr[j] >= thr, 0.0, MASK_NEG)

    n_slots = s_scr.shape[0]

    def scores_head(j, hh):
        hs = slice(hh * LANES, (hh + 1) * LANES)
        s_scr[hh % n_slots] = (jnp.dot(kbuf[j % 2, :, hs], qt_scr[hh], preferred_element_type=F32)
                               + bias_scr[...])

    def attend_head(j, hh):
        hs = slice(hh * LANES, (hh + 1) * LANES)
        s = s_scr[hh % n_slots]
        m_prev = m_scr[hh:hh + 1, :]
        m_new = jnp.maximum(m_prev, jnp.max(s, axis=0, keepdims=True))
        alpha = jnp.exp2(m_prev - m_new)
        p = jnp.exp2(s - m_new)
        l_scr[hh:hh + 1, :] = alpha * l_scr[hh:hh + 1, :] + jnp.sum(p, axis=0, keepdims=True)
        acc_scr[hs, :] = alpha * acc_scr[hs, :] + jnp.dot(vbuf[j % 2, hs, :], p.astype(BF16),
                                                          preferred_element_type=F32)
        m_scr[hh:hh + 1, :] = m_new

    def tile_steps(j, has_next):
        v_copy(j).wait()
        for hh in range(N_ATTN_HEADS):
            ahead = hh + SCORE_LOOKAHEAD
            if ahead < N_ATTN_HEADS:
                scores_head(j, ahead)
            elif has_next:
                if ahead == N_ATTN_HEADS:
                    @pl.when(j + 2 < nk)
                    def _():
                        k_copy(j + 2).start()

                    k_copy(j + 1).wait()
                    select_bias(j + 1)
                scores_head(j + 1, ahead - N_ATTN_HEADS)
            attend_head(j, hh)

    @pl.when(nk > 1)
    def _():
        k_copy(1).start()

    k_copy(0).wait()
    select_bias(0)
    for hh in range(SCORE_LOOKAHEAD):
        scores_head(0, hh)

    def attend_tile(j, carry):
        v_copy(j + 1).start()
        tile_steps(j, True)
        return carry

    lax.fori_loop(0, nk - 1, attend_tile, 0)
    tile_steps(nk - 1, False)

    for hh in range(N_ATTN_HEADS):
        hs = slice(hh * LANES, (hh + 1) * LANES)
        o_t = acc_scr[hs, :] / l_scr[hh:hh + 1, :]
        o_ref[0, :, hs] = o_t.T.astype(o_ref.dtype)


def _dsa(iq, iw_t, ik_bf, q_bf, k_bf, v_t, *, past, seq_len, n_queries, topk, tq, tk):
    b, t, hd = q_bf.shape
    lp = ik_bf.shape[1]
    nkt = lp // tk
    kern = functools.partial(_dsa_kernel, past=past, seq_len=seq_len, n_queries=n_queries, topk=topk,
                             tq=tq, tk=tk)
    return pl.pallas_call(
        kern,
        out_shape=jax.ShapeDtypeStruct((b, t, hd), BF16),
        grid=(b, t // tq),
        in_specs=[
            pl.BlockSpec((1, tq, N_IDX_HEADS * IDX_DIM), lambda bi, i: (bi, i, 0)),
            pl.BlockSpec((1, N_IDX_HEADS, tq), lambda bi, i: (bi, 0, i)),
            pl.BlockSpec((1, lp, IDX_DIM), lambda bi, i: (bi, 0, 0)),
            pl.BlockSpec((1, tq, hd), lambda bi, i: (bi, i, 0)),
            pl.BlockSpec(memory_space=pl.ANY),
            pl.BlockSpec(memory_space=pl.ANY),
        ],
        out_specs=pl.BlockSpec((1, tq, hd), lambda bi, i: (bi, i, 0)),
        scratch_shapes=[
            pltpu.VMEM((nkt, tk, tq), F32),
            pltpu.VMEM((nkt, tk, tq), jnp.bfloat16),
            pltpu.VMEM((2, tk, hd), BF16),
            pltpu.VMEM((2, hd, tk), BF16),
            pltpu.SemaphoreType.DMA((2, 2)),
            pltpu.VMEM((N_ATTN_HEADS, tq), F32),
            pltpu.VMEM((N_ATTN_HEADS, tq), F32),
            pltpu.VMEM((hd, tq), F32),
            pltpu.VMEM((SCORE_SLOTS, tk, tq), F32),
            pltpu.VMEM((tk, tq), F32),
            pltpu.VMEM((N_IDX_HEADS * IDX_DIM // LANES, LANES, tq), BF16),
            pltpu.VMEM((N_ATTN_HEADS, ATTN_HEAD_DIM, tq), BF16),
        ],
        compiler_params=_cparams(("arbitrary", "arbitrary")),
        name="dsa_attend",
    )(iq, iw_t, ik_bf, q_bf, k_bf, v_t)


def _gla_kernel(q_ref, k_ref, v_ref, lg_ref, gr_ref, g_ref, s0_ref, o_ref, sT_out, st_scr, *, n_chunks):
    tstep = pl.program_id(1)

    @pl.when(tstep == 0)
    def _():
        st_scr[...] = s0_ref[0]

    row = lax.broadcasted_iota(jnp.int32, (CHUNK, GLA_DK), 0)
    tri = (lax.broadcasted_iota(jnp.int32, (CHUNK, CHUNK), 0)
           >= lax.broadcasted_iota(jnp.int32, (CHUNK, CHUNK), 1))

    for c in range(n_chunks):
        rs = slice(c * CHUNK, (c + 1) * CHUNK)
        for hh in range(GLA_HEADS):
            ks = slice(hh * GLA_DK, (hh + 1) * GLA_DK)
            vs = slice(hh * GLA_DV, (hh + 1) * GLA_DV)
            bcum = lg_ref[0, rs, ks]
            shift = 1
            while shift < CHUNK:
                bcum = bcum + jnp.where(row >= shift, pltpu.roll(bcum, shift, 0), 0.0)
                shift *= 2
            bl = bcum[CHUNK - 1:CHUNK, :]
            q = q_ref[0, rs, ks]
            k = k_ref[0, rs, ks]
            v = v_ref[0, rs, vs]
            qe = (q * jnp.exp(bcum)).astype(BF16)
            ke = (k * jnp.exp(-bcum)).astype(BF16)
            kd = (k * jnp.exp(bl - bcum)).astype(BF16)
            st = st_scr[hh]
            inter = lax.dot_general(qe, st.astype(BF16), (((1,), (1,)), ((), ())),
                                    preferred_element_type=F32)
            a = lax.dot_general(qe, ke, (((1,), (1,)), ((), ())), preferred_element_type=F32)
            a = jnp.where(tri, a, 0.0)
            o = inter + jnp.dot(a.astype(BF16), v, preferred_element_type=F32)
            st_scr[hh] = st * jnp.exp(bl) + lax.dot_general(v, kd, (((0,), (0,)), ((), ())),
                                                            preferred_element_type=F32)
            n = o * lax.rsqrt(jnp.mean(o * o, axis=-1, keepdims=True) + EPS) * g_ref[...]
            o_ref[0, rs, vs] = (n * _silu(gr_ref[0, rs, vs])).astype(o_ref.dtype)

    @pl.when(tstep == pl.num_programs(1) - 1)
    def _():
        sT_out[0] = st_scr[...]


def _gla(gq, gk, gv, lg, gr, g_gla, s0t, tg):
    b, t, _ = gq.shape
    kern = functools.partial(_gla_kernel, n_chunks=tg // CHUNK)
    tok = lambda w: pl.BlockSpec((1, tg, w), lambda bi, m: (bi, m, 0))
    wk, wv = GLA_HEADS * GLA_DK, GLA_HEADS * GLA_DV
    st_spec = pl.BlockSpec((1, GLA_HEADS, GLA_DV, GLA_DK), lambda bi, m: (bi, 0, 0, 0))
    return pl.pallas_call(
        kern,
        out_shape=(jax.ShapeDtypeStruct((b, t, wv), BF16),
                   jax.ShapeDtypeStruct((b, GLA_HEADS, GLA_DV, GLA_DK), F32)),
        grid=(b, t // tg),
        in_specs=[tok(wk), tok(wk), tok(wv), tok(wk), tok(wv),
                  pl.BlockSpec((1, GLA_DV), lambda bi, m: (0, 0)), st_spec],
        out_specs=(tok(wv), st_spec),
        scratch_shapes=[pltpu.VMEM((GLA_HEADS, GLA_DV, GLA_DK), F32)],
        compiler_params=_cparams(("arbitrary", "arbitrary")),
        name="gla",
    )(gq, gk, gv, lg, gr, g_gla.reshape(1, GLA_DV), s0t)


def _outproj_kernel(oa_ref, og_ref, wa_ref, wg_ref, x_ref, mod_ref, gffn_ref, x1_out, h2_out):
    bb, tb, d = x_ref.shape
    rows = bb * tb
    oa = oa_ref[...].reshape(rows, oa_ref.shape[-1])
    og = og_ref[...].reshape(rows, og_ref.shape[-1])
    mix = (jnp.dot(oa, wa_ref[...], preferred_element_type=F32)
           + jnp.dot(og, wg_ref[...], preferred_element_type=F32)).reshape(bb, tb, d)
    mod = mod_ref[...]
    x1 = x_ref[...] + mod[:, :, 2 * d:3 * d] * mix
    x1_out[...] = x1
    y = x1 * lax.rsqrt(jnp.mean(x1 * x1, axis=-1, keepdims=True) + EPS) * gffn_ref[...]
    h2_out[...] = (y * (1.0 + mod[:, :, 4 * d:5 * d]) + mod[:, :, 3 * d:4 * d]).astype(BF16)


def _outproj(o_a, o_g, wout_a, wout_g, x, mod, g_ffn, bb, tb):
    b, t, d = x.shape
    tokspec = lambda w: pl.BlockSpec((bb, tb, w), lambda i, m: (i, m, 0))
    const = lambda shp: pl.BlockSpec(shp, lambda i, m: tuple(0 for _ in shp))
    return pl.pallas_call(
        _outproj_kernel,
        out_shape=(jax.ShapeDtypeStruct((b, t, d), F32), jax.ShapeDtypeStruct((b, t, d), BF16)),
        grid=(b // bb, t // tb),
        in_specs=[tokspec(o_a.shape[-1]), tokspec(o_g.shape[-1]), const(wout_a.shape), const(wout_g.shape),
                  tokspec(d), pl.BlockSpec((bb, 1, mod.shape[-1]), lambda i, m: (i, 0, 0)), const((1, 1, d))],
        out_specs=(tokspec(d), tokspec(d)),
        compiler_params=_cparams(("arbitrary", "arbitrary")),
        name="outproj_norm",
    )(o_a, o_g, wout_a, wout_g, x, mod, g_ffn.reshape(1, 1, d))


def _ffn_kernel(h_ref, halo_ref, wa_ref, wb_ref, wca_ref, wcb_ref, bca_ref, bcb_ref, sa_ref, sb_ref,
                wd_ref, x1_ref, mod_ref, y_out, nba_out, nbb_out, acc_scr, hext_scr, act_scr):
    m = pl.program_id(1)
    f = pl.program_id(2)
    nf = pl.num_programs(2) - 1
    bb, tb, d = h_ref.shape
    rows = bb * tb
    tf = wa_ref.shape[-1]
    halo = halo_ref.shape[1]

    @pl.when(f == 0)
    def _():
        hext_scr[:, 0:halo, :] = halo_ref[...]
        hext_scr[:, halo:, :] = h_ref[...]
        acc_scr[...] = jnp.zeros(acc_scr.shape, F32)
        act_scr[1] = jnp.zeros(act_scr.shape[1:], act_scr.dtype)

    def conv_branch(hext, tpos, w_ref, wc_ref, bc_ref, s_ref, nb_out):
        u_ext = jnp.dot(hext, w_ref[...], preferred_element_type=F32).reshape(bb, tb + halo, tf)
        u = u_ext[:, halo:, :].reshape(rows, tf)
        prev = jnp.where(m == 0, s_ref[...], u_ext[:, halo - (CONV_W - 1):halo, :])
        bcast = lambda r: jnp.broadcast_to(prev[:, r:r + 1, :], (bb, tb, tf)).reshape(rows, tf)
        p0, p1 = bcast(0), bcast(1)
        u1 = jnp.where(tpos == 0, p1, pltpu.roll(u, 1, 0))
        u2 = jnp.where(tpos == 0, p0, jnp.where(tpos == 1, p1, pltpu.roll(u, 2, 0)))
        wc = wc_ref[...]
        nb_out[:, 0] = u.reshape(bb, tb, tf)[:, tb - (CONV_W - 1):, :]
        return bc_ref[...] + wc[0:1] * u2 + wc[1:2] * u1 + wc[2:3] * u

    @pl.when(f < nf)
    def _():
        act_prev = act_scr[(f + 1) % 2]
        hext = hext_scr[...].reshape(bb * (tb + halo), d)
        tpos = lax.broadcasted_iota(jnp.int32, (rows, tf), 0) % tb
        ua = conv_branch(hext, tpos, wa_ref, wca_ref, bca_ref, sa_ref, nba_out)
        ub = conv_branch(hext, tpos, wb_ref, wcb_ref, bcb_ref, sb_ref, nbb_out)
        acc_scr[...] += jnp.dot(act_prev, wd_ref[...], preferred_element_type=F32)
        act_scr[f % 2] = (_silu(ua) * ub).astype(act_scr.dtype)

    @pl.when(f == nf)
    def _():
        acc = acc_scr[...] + jnp.dot(act_scr[(f + 1) % 2], wd_ref[...], preferred_element_type=F32)
        mod = mod_ref[...]
        y_out[...] = x1_ref[...] + mod[:, :, 5 * d:6 * d] * acc.reshape(bb, tb, d)


def _ffn(h2, w_up, w_conv, b_conv, conv_state, w_down, x1, mod, bb, tb):
    b, t, d = x1.shape
    dff = w_down.shape[0]
    tf = FFN_TILE
    nf = dff // tf
    halo = 2 * SUBLANES
    hb = tb // halo
    tokspec = lambda w: pl.BlockSpec((bb, tb, w), lambda i, m, f: (i, m, 0))
    up = lambda f: jnp.minimum(f, nf - 1)
    a_col = lambda r: pl.BlockSpec((r, tf), lambda i, m, f: (0, up(f)))
    b_col = lambda r: pl.BlockSpec((r, tf), lambda i, m, f: (0, nf + up(f)))
    st_a = pl.BlockSpec((bb, CONV_W - 1, tf), lambda i, m, f: (i, 0, up(f)))
    st_b = pl.BlockSpec((bb, CONV_W - 1, tf), lambda i, m, f: (i, 0, nf + up(f)))
    nb_spec = pl.BlockSpec((bb, 1, CONV_W - 1, tf), lambda i, m, f: (i, m, 0, up(f)))
    bc2 = b_conv.reshape(1, 2 * dff)
    return pl.pallas_call(
        _ffn_kernel,
        out_shape=(jax.ShapeDtypeStruct((b, t, d), F32),
                   jax.ShapeDtypeStruct((b, t // tb, CONV_W - 1, dff), F32),
                   jax.ShapeDtypeStruct((b, t // tb, CONV_W - 1, dff), F32)),
        grid=(b // bb, t // tb, nf + 1),
        in_specs=[
            tokspec(d),
            pl.BlockSpec((bb, halo, d), lambda i, m, f: (i, jnp.maximum(m * hb - 1, 0), 0)),
            a_col(d), b_col(d), a_col(CONV_W), b_col(CONV_W), a_col(1), b_col(1), st_a, st_b,
            pl.BlockSpec((tf, d), lambda i, m, f: (jnp.maximum(f - 1, 0), 0)),
            tokspec(d),
            pl.BlockSpec((bb, 1, mod.shape[-1]), lambda i, m, f: (i, 0, 0)),
        ],
        out_specs=(tokspec(d), nb_spec, nb_spec),
        scratch_shapes=[pltpu.VMEM((bb * tb, d), F32), pltpu.VMEM((bb, tb + halo, d), BF16),
                        pltpu.VMEM((2, bb * tb, tf), BF16)],
        compiler_params=_cparams(("arbitrary", "arbitrary", "arbitrary")),
        name="conv_ffn",
    )(h2, h2, w_up, w_up, w_conv, w_conv, bc2, bc2, conv_state, conv_state, w_down, x1, mod)


def _prep_weights(w_in, w_gate2, w_out, w_up, w_down):
    hd = N_ATTN_HEADS * ATTN_HEAD_DIM
    sizes = (hd, hd, hd, N_IDX_HEADS * IDX_DIM, IDX_DIM, N_IDX_HEADS,
             GLA_HEADS * GLA_DK, GLA_HEADS * GLA_DK, GLA_HEADS * GLA_DV, GLA_HEADS * GLA_DV, GLA_GATE_RANK)
    offs = np.concatenate([[0], np.cumsum(sizes)])
    col = lambda n: w_in[:, int(offs[n]):int(offs[n + 1])]
    wbig = jnp.concatenate([col(0), col(1), col(2), col(3), col(6), col(7), col(8), col(9)], axis=1).astype(BF16)
    pad = LANES - (IDX_DIM + N_IDX_HEADS + GLA_GATE_RANK)
    wsmall = jnp.concatenate([col(4), col(5), col(10), jnp.zeros((w_in.shape[0], pad), w_in.dtype)],
                             axis=1).astype(BF16)
    lo = IDX_DIM + N_IDX_HEADS
    wg2p = jnp.zeros((LANES, w_gate2.shape[1]), F32).at[lo:lo + GLA_GATE_RANK].set(w_gate2).astype(BF16)
    return dict(wbig=wbig, wsmall=wsmall, wg2p=wg2p,
                wout_a=w_out[:hd].astype(BF16), wout_g=w_out[hd:].astype(BF16),
                w_up=w_up.astype(BF16), w_down=w_down.astype(BF16))


def _layer(x, mod, past, past_k, past_v, past_ik, gla_s0, conv_state, lw, pw, *, bb, tb, tq, tk, tg):
    (g_mix, g_ffn, g_q, g_k, b_gate2, g_gla, w_conv, b_conv) = lw
    b, t, d = x.shape
    hd = N_ATTN_HEADS * ATTN_HEAD_DIM
    pos = past + jnp.arange(t, dtype=jnp.int32)
    ropea = _rope_table(pos, ROPE_DIM // 2, LANES, LANES)
    ropei = _rope_table(pos, IDX_ROPE_DIM // 2, IDX_DIM, LANES)
    ropes = _rope_table(pos, IDX_ROPE_DIM // 2, LANES, IDX_DIM)

    (q_bf, k, v, iq, small, lg, gq, gk, gv, gr) = _inproj(
        x, mod, g_mix, pw["wbig"], pw["wsmall"], g_q, g_k, ropea, ropei, ropes, pw["wg2p"], b_gate2, bb, tb)
    ik = small[:, :, :IDX_DIM]
    iw = small[:, :, IDX_DIM:IDX_DIM + N_IDX_HEADS]

    k_bf, v_bf, ik_bf = k.astype(BF16), v.astype(BF16), ik.astype(BF16)
    if past_k is not None:
        k_bf = jnp.concatenate([past_k.reshape(b, past, hd).astype(BF16), k_bf], axis=1)
        v_bf = jnp.concatenate([past_v.reshape(b, past, hd).astype(BF16), v_bf], axis=1)
        ik_bf = jnp.concatenate([past_ik.astype(BF16), ik_bf], axis=1)
    seq_len = past + t
    lp = -(-seq_len // tk) * tk
    if lp != seq_len:
        padw = ((0, 0), (0, lp - seq_len), (0, 0))
        k_bf, v_bf, ik_bf = jnp.pad(k_bf, padw), jnp.pad(v_bf, padw), jnp.pad(ik_bf, padw)
    tpad = -(-t // tq) * tq
    qpad = lambda a: jnp.pad(a, ((0, 0), (0, tpad - t), (0, 0))) if tpad != t else a
    iw_t = qpad(iw).transpose(0, 2, 1)
    v_t = v_bf.transpose(0, 2, 1)
    topk = min(TOPK_MAX, seq_len // 4)
    o_a = _dsa(qpad(iq), iw_t, ik_bf, qpad(q_bf), k_bf, v_t, past=past, seq_len=seq_len, n_queries=t,
               topk=topk, tq=tq, tk=tk)
    o_a = o_a[:, :t]

    s0t = jnp.swapaxes(gla_s0, -1, -2)
    o_g, s_t = _gla(gq, gk, gv, lg, gr, g_gla, s0t, tg)
    s_new = jnp.swapaxes(s_t, -1, -2)

    x1, h2 = _outproj(o_a, o_g, pw["wout_a"], pw["wout_g"], x, mod, g_ffn, bb, tb)
    y, nb_a, nb_b = _ffn(h2, pw["w_up"], w_conv, b_conv, conv_state, pw["w_down"], x1, mod, bb, tb)
    new_buf = jnp.concatenate([nb_a[:, -1], nb_b[:, -1]], axis=-1)
    return (y, k.reshape(b, t, N_ATTN_HEADS, ATTN_HEAD_DIM), v.reshape(b, t, N_ATTN_HEADS, ATTN_HEAD_DIM),
            ik, s_new, new_buf)


def _tiles(b, t, seq_len):
    if t >= 512:
        bb, tb = 1, 512
    else:
        bb, tb = min(b, 512 // t), t
    tq = 256 if t % 256 == 0 else LANES
    tk = 256 if seq_len % 256 == 0 else LANES
    tg = min(t, 256)
    return dict(bb=bb, tb=tb, tq=tq, tk=tk, tg=tg)


def kernel(x_prompt, x_sample, c_prompt, c_sample, cache_k, cache_v, cache_idx_k, state_gla, state_ffn_conv,
           w_ada, b_ada, g_mix, g_ffn, w_in, g_q, g_k, w_gate2, b_gate2, g_gla, w_out, w_up, w_conv, b_conv,
           w_down):
    bp, s, d = x_prompt.shape
    bs, ts, _ = x_sample.shape
    depth = w_ada.shape[0]
    past = cache_k.shape[2]
    dff = w_down.shape[1]
    dt = x_prompt.dtype

    y_p, y_s = x_prompt, x_sample
    outs = [[] for _ in range(10)]
    for l in range(depth):
        n_c = bp + bs
        rows = -(-n_c // SUBLANES) * SUBLANES
        c_all = jnp.concatenate([c_prompt, c_sample, jnp.zeros((rows - n_c, d), dt)], axis=0)
        mod = _ada(c_all, w_ada[l], b_ada[l])
        mod_p = mod[:bp].reshape(bp, 1, 6 * d)
        mod_s = mod[bp:n_c].reshape(bs, 1, 6 * d)
        pw = _prep_weights(w_in[l], w_gate2[l], w_out[l], w_up[l], w_down[l])
        lw = (g_mix[l], g_ffn[l], g_q[l], g_k[l], b_gate2[l], g_gla[l], w_conv[l], b_conv[l])

        gla0 = jnp.zeros((bp, GLA_HEADS, GLA_DK, GLA_DV), dt)
        conv0 = jnp.zeros((bp, CONV_W - 1, 2 * dff), dt)
        y_p, k1, v1, ik1, s1, c1 = _layer(y_p, mod_p, 0, None, None, None, gla0, conv0, lw, pw,
                                          **_tiles(bp, s, s))
        y_s, k2, v2, ik2, s2, c2 = _layer(y_s, mod_s, past, cache_k[l], cache_v[l], cache_idx_k[l],
                                          state_gla[l], state_ffn_conv[l], lw, pw,
                                          **_tiles(bs, ts, past + ts))
        for lst, val in zip(outs, (k1, v1, ik1, s1, c1, k2, v2, ik2, s2, c2)):
            lst.append(val)

    return (y_p, y_s) + tuple(o[0][None] if depth == 1 else jnp.stack(o) for o in outs)
```

```python
import functools
import math

import jax
import jax.numpy as jnp
import numpy as np
from jax import lax
from jax.experimental import pallas as pl
from jax.experimental.pallas import tpu as pltpu

F32 = jnp.float32
BF16 = jnp.bfloat16

CHUNK = 64
N_ATTN_HEADS = 8
ATTN_HEAD_DIM = 128
ROPE_DIM = 32
ROPE_THETA = 500000.0
N_IDX_HEADS = 16
IDX_DIM = 64
IDX_ROPE_DIM = 16
TOPK_MAX = 256
GLA_HEADS = 4
GLA_DK = 128
GLA_DV = 256
GLA_GATE_RANK = 16
GLA_GATE_TEMP = 16.0
CONV_W = 3
EPS = 1e-6

LANES = 128
SUBLANES = 8
VMEM_LIMIT_BYTES = 56 * 1024 * 1024

SCORE_LOOKAHEAD = 4
SCORE_SLOTS = 8
FINE_STEPS = 17
PROJ_GROUP = 1024
FFN_TILE = 512

MASK_NEG = -1e30
F32_LOWEST = -3.4028235e38


def _cparams(sem):
    return pltpu.CompilerParams(dimension_semantics=sem, vmem_limit_bytes=VMEM_LIMIT_BYTES)


def _sigmoid(x):
    return 1.0 / (1.0 + jnp.exp(-x))


def _silu(x):
    return x * _sigmoid(x)


def _ada_kernel(c_ref, w_ref, b_ref, o_ref):
    s = _silu(c_ref[...]).astype(BF16)
    o_ref[...] = jnp.dot(s, w_ref[...].astype(BF16), preferred_element_type=F32) + b_ref[...]


def _ada(c_all, w_ada, b_ada):
    rows, d = c_all.shape
    n = w_ada.shape[1]
    tn = 1536 if n % 1536 == 0 else n
    return pl.pallas_call(
        _ada_kernel,
        out_shape=jax.ShapeDtypeStruct((rows, n), F32),
        grid=(n // tn,),
        in_specs=[
            pl.BlockSpec((rows, d), lambda j: (0, 0)),
            pl.BlockSpec((d, tn), lambda j: (0, j)),
            pl.BlockSpec((1, tn), lambda j: (0, j)),
        ],
        out_specs=pl.BlockSpec((rows, tn), lambda j: (0, j)),
        compiler_params=_cparams(("arbitrary",)),
        name="ada_mod",
    )(c_all, w_ada, b_ada.reshape(1, n))


def _rope_table(pos, half, period, active_lanes):
    inv = ROPE_THETA ** (-jnp.arange(half, dtype=F32) / half)
    lane = np.arange(LANES)
    within = lane % period
    first = jnp.asarray((within < half) & (lane < active_lanes))
    second = jnp.asarray((within >= half) & (within < 2 * half) & (lane < active_lanes))
    ang = pos.astype(F32)[:, None] * inv[jnp.asarray(within % half)][None, :]
    cos, sin = jnp.cos(ang), jnp.sin(ang)
    c = jnp.where(first | second, cos, 1.0)
    s1 = jnp.where(first, -sin, 0.0)
    s2 = jnp.where(second, sin, 0.0)
    return jnp.stack([c, s1, s2]).astype(F32)


def _rope_apply(v, tab, half):
    return (v * tab[0] + pltpu.roll(v, LANES - half, 1) * tab[1] + pltpu.roll(v, half, 1) * tab[2])


def _tile_rows(tab_ref, bb):
    t = tab_ref[...]
    if bb == 1:
        return t
    tb = t.shape[1]
    return jnp.broadcast_to(t[:, None], (3, bb, tb, LANES)).reshape(3, bb * tb, LANES)


def _inproj_kernel(x_ref, mod_ref, gmix_ref, wbig_ref, wsmall_ref, gq_ref, gk_ref,
                   ropea_ref, ropei_ref, ropes_ref, wg2_ref, bg2_ref,
                   q_out, k_out, v_out, iq_out, small_out, lg_out, gq_out, gk_out, gv_out, gr_out,
                   *rest):
    *attn_copies, h_scr = rest
    j = pl.program_id(2)
    bb, tb, d = x_ref.shape
    rows = bb * tb

    @pl.when(j == 0)
    def _():
        x = x_ref[...]
        ms = jnp.mean(x * x, axis=-1, keepdims=True)
        y = x * lax.rsqrt(ms + EPS) * gmix_ref[...]
        mod = mod_ref[...]
        h = y * (1.0 + mod[:, :, d:2 * d]) + mod[:, :, 0:d]
        hb = h.astype(BF16).reshape(rows, d)
        h_scr[...] = hb
        small = jnp.dot(hb, wsmall_ref[...], preferred_element_type=F32)
        small = _rope_apply(small, _tile_rows(ropes_ref, bb), IDX_ROPE_DIM // 2)
        small_out[...] = small.reshape(bb, tb, LANES)
        pre = jnp.dot(small.astype(BF16), wg2_ref[...], preferred_element_type=F32) + bg2_ref[...]
        lsig = jnp.minimum(pre, 0.0) - jnp.log(1.0 + jnp.exp(-jnp.abs(pre)))
        lg_out[...] = (lsig / GLA_GATE_TEMP).reshape(bb, tb, lg_out.shape[-1])

    y = jnp.dot(h_scr[...], wbig_ref[...], preferred_element_type=F32)

    def headnorm_rope(g_ref, scale):
        tab = _tile_rows(ropea_ref, bb)
        outs = []
        for hh in range(N_ATTN_HEADS):
            yh = y[:, hh * LANES:(hh + 1) * LANES]
            n = yh * lax.rsqrt(jnp.mean(yh * yh, axis=-1, keepdims=True) + EPS) * g_ref[...]
            r = _rope_apply(n, tab, ROPE_DIM // 2)
            outs.append(r * scale if scale != 1.0 else r)
        return outs

    @pl.when(j == 0)
    def _():
        for hh, r in enumerate(headnorm_rope(gq_ref, ATTN_HEAD_DIM ** -0.5 * math.log2(math.e))):
            q_out[:, :, hh * LANES:(hh + 1) * LANES] = r.astype(BF16).reshape(bb, tb, LANES)

    @pl.when(j == 1)
    def _():
        for hh, r in enumerate(headnorm_rope(gk_ref, 1.0)):
            k_out[:, :, hh * LANES:(hh + 1) * LANES] = r.reshape(bb, tb, LANES)
            if attn_copies:
                attn_copies[0][:, :, hh * LANES:(hh + 1) * LANES] = r.astype(BF16).reshape(bb, tb, LANES)

    @pl.when(j == 2)
    def _():
        v_out[...] = y.reshape(bb, tb, y.shape[-1])
        if attn_copies:
            attn_copies[1][0] = y.T.astype(BF16)

    @pl.when(j == 3)
    def _():
        tab = _tile_rows(ropei_ref, bb)
        for c in range(y.shape[-1] // LANES):
            r = _rope_apply(y[:, c * LANES:(c + 1) * LANES], tab, IDX_ROPE_DIM // 2)
            iq_out[:, :, c * LANES:(c + 1) * LANES] = r.astype(BF16).reshape(bb, tb, LANES)

    @pl.when(j == 4)
    def _():
        half = y.shape[-1] // 2
        gq_out[...] = (y[:, :half] * GLA_DK ** -0.5).reshape(bb, tb, half)
        gk_out[...] = y[:, half:].reshape(bb, tb, half)

    @pl.when(j == 5)
    def _():
        gv_out[...] = y.astype(BF16).reshape(bb, tb, y.shape[-1])

    @pl.when(j == 6)
    def _():
        gr_out[...] = y.reshape(bb, tb, y.shape[-1])


def _inproj(x, mod, g_mix, wbig, wsmall, g_q, g_k, ropea, ropei, ropes, wg2p, bg2, bb, tb, attn_copies):
    b, t, d = x.shape
    gw = PROJ_GROUP
    ngroups = wbig.shape[1] // gw
    glak = GLA_HEADS * GLA_DK
    rows = bb * tb
    tok = lambda w, dt: jax.ShapeDtypeStruct((b, t, w), dt)
    tokspec = lambda w: pl.BlockSpec((bb, tb, w), lambda i, m, j: (i, m, 0))
    const = lambda shp: pl.BlockSpec(shp, lambda i, m, j: tuple(0 for _ in shp))
    tabspec = pl.BlockSpec((3, tb, LANES), lambda i, m, j: (0, m, 0))
    out_shapes = (tok(1024, BF16), tok(1024, F32), tok(1024, F32), tok(1024, BF16), tok(LANES, F32),
                  tok(glak, F32), tok(glak, F32), tok(glak, F32), tok(1024, BF16), tok(1024, F32))
    out_specs = tuple(tokspec(s.shape[-1]) for s in out_shapes)
    if attn_copies:
        assert bb == 1
        out_shapes += (tok(1024, BF16), jax.ShapeDtypeStruct((b, 1024, t), BF16))
        out_specs += (tokspec(1024), pl.BlockSpec((1, 1024, tb), lambda i, m, j: (i, 0, m)))
    return pl.pallas_call(
        _inproj_kernel,
        out_shape=out_shapes,
        grid=(b // bb, t // tb, ngroups),
        in_specs=[
            tokspec(d),
            pl.BlockSpec((bb, 1, mod.shape[-1]), lambda i, m, j: (i, 0, 0)),
            const((1, 1, d)),
            pl.BlockSpec((d, gw), lambda i, m, j: (0, j)),
            const((d, LANES)),
            const((1, LANES)), const((1, LANES)),
            tabspec, tabspec, tabspec,
            const((LANES, glak)), const((1, glak)),
        ],
        out_specs=out_specs,
        scratch_shapes=[pltpu.VMEM((rows, d), BF16)],
        compiler_params=_cparams(("arbitrary", "arbitrary", "arbitrary")),
        name="norm_inproj",
    )(x, mod, g_mix.reshape(1, 1, d), wbig, wsmall, g_q.reshape(1, LANES), g_k.reshape(1, LANES),
      ropea, ropei, ropes, wg2p, bg2.reshape(1, glak))


def _dsa_kernel(iq_ref, iwt_ref, ik_ref, q_ref, k_hbm, vt_hbm, o_ref,
                u_scr, ub_scr, kbuf, vbuf, sem, m_scr, l_scr, acc_scr, s_scr, bias_scr, iqt_scr, qt_scr,
                *, past, seq_len, n_queries, topk, tq, tk):
    b = pl.program_id(0)
    i = pl.program_id(1)
    nkt = u_scr.shape[0]

    last_chunk = (past + (i + 1) * tq - 1) // CHUNK
    n_adm = jnp.minimum(seq_len, (last_chunk + 1) * CHUNK)
    nk = jnp.minimum((n_adm + tk - 1) // tk, nkt)

    def k_copy(j):
        off = pl.multiple_of(j * tk, tk)
        return pltpu.make_async_copy(k_hbm.at[b, pl.ds(off, tk)], kbuf.at[j % 2], sem.at[0, j % 2])

    def v_copy(j):
        off = pl.multiple_of(j * tk, tk)
        return pltpu.make_async_copy(vt_hbm.at[b, :, pl.ds(off, tk)], vbuf.at[j % 2], sem.at[1, j % 2])

    k_copy(0).start()
    v_copy(0).start()

    for c in range(N_ATTN_HEADS):
        cs = slice(c * LANES, (c + 1) * LANES)
        qt_scr[c] = q_ref[0, :, cs].astype(F32).T.astype(qt_scr.dtype)
        iqt_scr[c] = iq_ref[0, :, cs].astype(F32).T.astype(iqt_scr.dtype)

    q_chunk = (past + i * tq + lax.broadcasted_iota(jnp.int32, (tk, tq), 1)) // CHUNK
    key_iota = lax.broadcasted_iota(jnp.int32, (tk, tq), 0)
    idx_scale = (IDX_DIM * N_IDX_HEADS) ** -0.5

    group = 2 if nkt % 2 == 0 else 1
    n_groups = (nk + group - 1) // group

    def score_tile(j):
        ik_t = ik_ref[0, pl.ds(pl.multiple_of(j * tk, tk), tk), :]
        acc = jnp.zeros((tk, tq), F32)
        for hh in range(N_IDX_HEADS):
            iq_h = iqt_scr[hh // 2, (hh % 2) * IDX_DIM:(hh % 2 + 1) * IDX_DIM, :]
            logits = jnp.dot(ik_t, iq_h, preferred_element_type=F32)
            acc = acc + iwt_ref[0, hh:hh + 1, :] * jnp.maximum(logits, 0.0)
        kpos = j * tk + key_iota
        adm = (kpos // CHUNK <= q_chunk) & (kpos < seq_len)
        u = jnp.where(adm, acc * idx_scale, -jnp.inf)
        u_scr[j] = u
        ub_scr[j] = u.astype(ub_scr.dtype)

    def score_group(g, carry):
        for r in range(group):
            score_tile(g * group + r)
        return carry

    lax.fori_loop(0, n_groups, score_group, 0)

    def count_ge(scr, thr_row):
        rows = SUBLANES * (4 // jnp.dtype(scr.dtype).itemsize)
        thr_b = jnp.broadcast_to(thr_row.astype(scr.dtype), (rows, tq))
        one, zero = jnp.ones((), scr.dtype), jnp.zeros((), scr.dtype)

        def body(g, cnt):
            for r in range(group):
                x = scr[g * group + r].reshape(tk // rows, rows, tq)
                hit = jnp.where(x >= thr_b[None], one, zero)
                parts = [hit[n] for n in range(tk // rows)]
                while len(parts) > 1:
                    parts = [parts[n] + parts[n + 1] for n in range(0, len(parts), 2)]
                cnt = cnt + parts[0].astype(F32)
            return cnt

        cnt = lax.fori_loop(0, n_groups, body, jnp.zeros((rows, tq), F32))
        return jnp.sum(cnt, axis=0, keepdims=True)

    int_min = jnp.int32(-2 ** 31)
    kf = float(topk)

    def coarse_value(t_biased):
        mono = t_biased ^ int_min
        bits = jnp.where(mono >= 0, mono, mono ^ jnp.int32(0x7FFF0000))
        return lax.bitcast_convert_type(bits, F32)

    def coarse_step(it, t_biased):
        cand = t_biased | lax.shift_left(jnp.int32(1), jnp.int32(31) - it)
        return jnp.where(count_ge(ub_scr, coarse_value(cand)) >= kf, cand, t_biased)

    t16 = lax.fori_loop(0, 16, coarse_step, jnp.zeros((1, tq), jnp.int32))
    few = (t16 >= 0) & (t16 <= jnp.int32(0x007F0000))

    def to_mono(bits):
        return jnp.where(bits >= 0, bits, bits ^ jnp.int32(0x7FFFFFFF))

    c_mono = to_mono(lax.bitcast_convert_type(coarse_value(t16), jnp.int32))
    lo0 = c_mono - jnp.int32(0x8000)
    hi0 = c_mono + jnp.int32(0x10000)
    padded = i * tq + lax.broadcasted_iota(jnp.int32, (1, tq), 1) >= n_queries
    hi0 = jnp.where(few | padded, lo0, hi0)

    def fine_cond(state):
        it, _, _, pending = state
        return (it < FINE_STEPS) & (pending > 0)

    def fine_step(state):
        it, lo, hi, _ = state
        mid = lo + lax.shift_right_arithmetic(hi - lo + 1, jnp.int32(1))
        total = count_ge(u_scr, lax.bitcast_convert_type(to_mono(mid), F32))
        ok = total >= kf
        lo = jnp.where(ok, mid, lo)
        hi = jnp.where(total == kf, mid, jnp.where(ok, hi, mid - 1))
        return it + 1, lo, hi, jnp.max(jnp.where(hi > lo, 1, 0))

    _, lo_fin, _, _ = lax.while_loop(fine_cond, fine_step,
                                     (jnp.int32(0), lo0, hi0, jnp.max(jnp.where(hi0 > lo0, 1, 0))))
    thr_fin = lax.bitcast_convert_type(to_mono(lo_fin), F32)
    thr = jnp.where(few, F32_LOWEST, jnp.maximum(thr_fin, F32_LOWEST))

    m_scr[...] = jnp.full(m_scr.shape, MASK_NEG, F32)
    l_scr[...] = jnp.zeros(l_scr.shape, F32)
    acc_scr[...] = jnp.zeros(acc_scr.shape, F32)

    def select_bias(j):
        bias_scr[...] = jnp.where(u_scr[j] >= thr, 0.0, MASK_NEG)

    n_slots = s_scr.shape[0]

    def scores_head(j, hh):
        hs = slice(hh * LANES, (hh + 1) * LANES)
        s_scr[hh % n_slots] = (jnp.dot(kbuf[j % 2, :, hs], qt_scr[hh], preferred_element_type=F32)
                               + bias_scr[...])

    def attend_head(j, hh):
        hs = slice(hh * LANES, (hh + 1) * LANES)
        s = s_scr[hh % n_slots]
        m_prev = m_scr[hh:hh + 1, :]
        m_new = jnp.maximum(m_prev, jnp.max(s, axis=0, keepdims=True))
        alpha = jnp.exp2(m_prev - m_new)
        p = jnp.exp2(s - m_new)
        l_scr[hh:hh + 1, :] = alpha * l_scr[hh:hh + 1, :] + jnp.sum(p, axis=0, keepdims=True)
        acc_scr[hs, :] = alpha * acc_scr[hs, :] + jnp.dot(vbuf[j % 2, hs, :], p.astype(BF16),
                                                          preferred_element_type=F32)
        m_scr[hh:hh + 1, :] = m_new

    def tile_steps(j, has_next):
        v_copy(j).wait()
        for hh in range(N_ATTN_HEADS):
            ahead = hh + SCORE_LOOKAHEAD
            if ahead < N_ATTN_HEADS:
                scores_head(j, ahead)
            elif has_next:
                if ahead == N_ATTN_HEADS:
                    @pl.when(j + 2 < nk)
                    def _():
                        k_copy(j + 2).start()

                    k_copy(j + 1).wait()
                    select_bias(j + 1)
                scores_head(j + 1, ahead - N_ATTN_HEADS)
            attend_head(j, hh)

    @pl.when(nk > 1)
    def _():
        k_copy(1).start()

    k_copy(0).wait()
    select_bias(0)
    for hh in range(SCORE_LOOKAHEAD):
        scores_head(0, hh)

    def attend_tile(j, carry):
        v_copy(j + 1).start()
        tile_steps(j, True)
        return carry

    lax.fori_loop(0, nk - 1, attend_tile, 0)
    tile_steps(nk - 1, False)

    for hh in range(N_ATTN_HEADS):
        hs = slice(hh * LANES, (hh + 1) * LANES)
        o_t = acc_scr[hs, :] / l_scr[hh:hh + 1, :]
        o_ref[0, :, hs] = o_t.T.astype(o_ref.dtype)


def _dsa(iq, iw_t, ik_bf, q_bf, k_bf, v_t, *, past, seq_len, n_queries, topk, tq, tk):
    b, t, hd = q_bf.shape
    lp = ik_bf.shape[1]
    nkt = lp // tk
    kern = functools.partial(_dsa_kernel, past=past, seq_len=seq_len, n_queries=n_queries, topk=topk,
                             tq=tq, tk=tk)
    return pl.pallas_call(
        kern,
        out_shape=jax.ShapeDtypeStruct((b, t, hd), BF16),
        grid=(b, t // tq),
        in_specs=[
            pl.BlockSpec((1, tq, N_IDX_HEADS * IDX_DIM), lambda bi, i: (bi, i, 0)),
            pl.BlockSpec((1, N_IDX_HEADS, tq), lambda bi, i: (bi, 0, i)),
            pl.BlockSpec((1, lp, IDX_DIM), lambda bi, i: (bi, 0, 0)),
            pl.BlockSpec((1, tq, hd), lambda bi, i: (bi, i, 0)),
            pl.BlockSpec(memory_space=pl.ANY),
            pl.BlockSpec(memory_space=pl.ANY),
        ],
        out_specs=pl.BlockSpec((1, tq, hd), lambda bi, i: (bi, i, 0)),
        scratch_shapes=[
            pltpu.VMEM((nkt, tk, tq), F32),
            pltpu.VMEM((nkt, tk, tq), jnp.bfloat16),
            pltpu.VMEM((2, tk, hd), BF16),
            pltpu.VMEM((2, hd, tk), BF16),
            pltpu.SemaphoreType.DMA((2, 2)),
            pltpu.VMEM((N_ATTN_HEADS, tq), F32),
            pltpu.VMEM((N_ATTN_HEADS, tq), F32),
            pltpu.VMEM((hd, tq), F32),
            pltpu.VMEM((SCORE_SLOTS, tk, tq), F32),
            pltpu.VMEM((tk, tq), F32),
            pltpu.VMEM((N_IDX_HEADS * IDX_DIM // LANES, LANES, tq), BF16),
            pltpu.VMEM((N_ATTN_HEADS, ATTN_HEAD_DIM, tq), BF16),
        ],
        compiler_params=_cparams(("arbitrary", "arbitrary")),
        name="dsa_attend",
    )(iq, iw_t, ik_bf, q_bf, k_bf, v_t)


def _gla_kernel(q_ref, k_ref, v_ref, lg_ref, gr_ref, g_ref, s0_ref, o_ref, sT_out, st_scr, *, n_chunks):
    tstep = pl.program_id(1)

    @pl.when(tstep == 0)
    def _():
        st_scr[...] = s0_ref[0]

    row = lax.broadcasted_iota(jnp.int32, (CHUNK, GLA_DK), 0)
    tri = (lax.broadcasted_iota(jnp.int32, (CHUNK, CHUNK), 0)
           >= lax.broadcasted_iota(jnp.int32, (CHUNK, CHUNK), 1))

    for c in range(n_chunks):
        rs = slice(c * CHUNK, (c + 1) * CHUNK)
        for hh in range(GLA_HEADS):
            ks = slice(hh * GLA_DK, (hh + 1) * GLA_DK)
            vs = slice(hh * GLA_DV, (hh + 1) * GLA_DV)
            bcum = lg_ref[0, rs, ks]
            shift = 1
            while shift < CHUNK:
                bcum = bcum + jnp.where(row >= shift, pltpu.roll(bcum, shift, 0), 0.0)
                shift *= 2
            bl = bcum[CHUNK - 1:CHUNK, :]
            q = q_ref[0, rs, ks]
            k = k_ref[0, rs, ks]
            v = v_ref[0, rs, vs]
            qe = (q * jnp.exp(bcum)).astype(BF16)
            ke = (k * jnp.exp(-bcum)).astype(BF16)
            kd = (k * jnp.exp(bl - bcum)).astype(BF16)
            st = st_scr[hh]
            inter = lax.dot_general(qe, st.astype(BF16), (((1,), (1,)), ((), ())),
                                    preferred_element_type=F32)
            a = lax.dot_general(qe, ke, (((1,), (1,)), ((), ())), preferred_element_type=F32)
            a = jnp.where(tri, a, 0.0)
            o = inter + jnp.dot(a.astype(BF16), v, preferred_element_type=F32)
            st_scr[hh] = st * jnp.exp(bl) + lax.dot_general(v, kd, (((0,), (0,)), ((), ())),
                                                            preferred_element_type=F32)
            n = o * lax.rsqrt(jnp.mean(o * o, axis=-1, keepdims=True) + EPS) * g_ref[...]
            o_ref[0, rs, vs] = (n * _silu(gr_ref[0, rs, vs])).astype(o_ref.dtype)

    @pl.when(tstep == pl.num_programs(1) - 1)
    def _():
        sT_out[0] = st_scr[...]


def _gla(gq, gk, gv, lg, gr, g_gla, s0t, tg):
    b, t, _ = gq.shape
    kern = functools.partial(_gla_kernel, n_chunks=tg // CHUNK)
    tok = lambda w: pl.BlockSpec((1, tg, w), lambda bi, m: (bi, m, 0))
    wk, wv = GLA_HEADS * GLA_DK, GLA_HEADS * GLA_DV
    st_spec = pl.BlockSpec((1, GLA_HEADS, GLA_DV, GLA_DK), lambda bi, m: (bi, 0, 0, 0))
    return pl.pallas_call(
        kern,
        out_shape=(jax.ShapeDtypeStruct((b, t, wv), BF16),
                   jax.ShapeDtypeStruct((b, GLA_HEADS, GLA_DV, GLA_DK), F32)),
        grid=(b, t // tg),
        in_specs=[tok(wk), tok(wk), tok(wv), tok(wk), tok(wv),
                  pl.BlockSpec((1, GLA_DV), lambda bi, m: (0, 0)), st_spec],
        out_specs=(tok(wv), st_spec),
        scratch_shapes=[pltpu.VMEM((GLA_HEADS, GLA_DV, GLA_DK), F32)],
        compiler_params=_cparams(("arbitrary", "arbitrary")),
        name="gla",
    )(gq, gk, gv, lg, gr, g_gla.reshape(1, GLA_DV), s0t)


def _outproj_kernel(oa_ref, og_ref, wa_ref, wg_ref, x_ref, mod_ref, gffn_ref, x1_out, h2_out):
    bb, tb, d = x_ref.shape
    rows = bb * tb
    oa = oa_ref[...].reshape(rows, oa_ref.shape[-1])
    og = og_ref[...].reshape(rows, og_ref.shape[-1])
    mix = (jnp.dot(oa, wa_ref[...], preferred_element_type=F32)
           + jnp.dot(og, wg_ref[...], preferred_element_type=F32)).reshape(bb, tb, d)
    mod = mod_ref[...]
    x1 = x_ref[...] + mod[:, :, 2 * d:3 * d] * mix
    x1_out[...] = x1
    y = x1 * lax.rsqrt(jnp.mean(x1 * x1, axis=-1, keepdims=True) + EPS) * gffn_ref[...]
    h2_out[...] = (y * (1.0 + mod[:, :, 4 * d:5 * d]) + mod[:, :, 3 * d:4 * d]).astype(BF16)


def _outproj(o_a, o_g, wout_a, wout_g, x, mod, g_ffn, bb, tb):
    b, t, d = x.shape
    tokspec = lambda w: pl.BlockSpec((bb, tb, w), lambda i, m: (i, m, 0))
    const = lambda shp: pl.BlockSpec(shp, lambda i, m: tuple(0 for _ in shp))
    return pl.pallas_call(
        _outproj_kernel,
        out_shape=(jax.ShapeDtypeStruct((b, t, d), F32), jax.ShapeDtypeStruct((b, t, d), BF16)),
        grid=(b // bb, t // tb),
        in_specs=[tokspec(o_a.shape[-1]), tokspec(o_g.shape[-1]), const(wout_a.shape), const(wout_g.shape),
                  tokspec(d), pl.BlockSpec((bb, 1, mod.shape[-1]), lambda i, m: (i, 0, 0)), const((1, 1, d))],
        out_specs=(tokspec(d), tokspec(d)),
        compiler_params=_cparams(("arbitrary", "arbitrary")),
        name="outproj_norm",
    )(o_a, o_g, wout_a, wout_g, x, mod, g_ffn.reshape(1, 1, d))


def _ffn_kernel(h_ref, halo_ref, wa_ref, wb_ref, wca_ref, wcb_ref, bca_ref, bcb_ref, sa_ref, sb_ref,
                wd_ref, x1_ref, mod_ref, y_out, nba_out, nbb_out, acc_scr, hext_scr, act_scr):
    m = pl.program_id(1)
    f = pl.program_id(2)
    nf = pl.num_programs(2) - 1
    bb, tb, d = h_ref.shape
    rows = bb * tb
    tf = wa_ref.shape[-1]
    halo = halo_ref.shape[1]

    @pl.when(f == 0)
    def _():
        hext_scr[:, 0:halo, :] = halo_ref[...]
        hext_scr[:, halo:, :] = h_ref[...]
        acc_scr[...] = jnp.zeros(acc_scr.shape, F32)
        act_scr[1] = jnp.zeros(act_scr.shape[1:], act_scr.dtype)

    def conv_branch(hext, tpos, w_ref, wc_ref, bc_ref, s_ref, nb_out):
        u_ext = jnp.dot(hext, w_ref[...], preferred_element_type=F32).reshape(bb, tb + halo, tf)
        u = u_ext[:, halo:, :].reshape(rows, tf)
        prev = jnp.where(m == 0, s_ref[...], u_ext[:, halo - (CONV_W - 1):halo, :])
        bcast = lambda r: jnp.broadcast_to(prev[:, r:r + 1, :], (bb, tb, tf)).reshape(rows, tf)
        p0, p1 = bcast(0), bcast(1)
        u1 = jnp.where(tpos == 0, p1, pltpu.roll(u, 1, 0))
        u2 = jnp.where(tpos == 0, p0, jnp.where(tpos == 1, p1, pltpu.roll(u, 2, 0)))
        wc = wc_ref[...]
        nb_out[:, 0] = u.reshape(bb, tb, tf)[:, tb - (CONV_W - 1):, :]
        return bc_ref[...] + wc[0:1] * u2 + wc[1:2] * u1 + wc[2:3] * u

    @pl.when(f < nf)
    def _():
        act_prev = act_scr[(f + 1) % 2]
        hext = hext_scr[...].reshape(bb * (tb + halo), d)
        tpos = lax.broadcasted_iota(jnp.int32, (rows, tf), 0) % tb
        ua = conv_branch(hext, tpos, wa_ref, wca_ref, bca_ref, sa_ref, nba_out)
        ub = conv_branch(hext, tpos, wb_ref, wcb_ref, bcb_ref, sb_ref, nbb_out)
        acc_scr[...] += jnp.dot(act_prev, wd_ref[...], preferred_element_type=F32)
        act_scr[f % 2] = (_silu(ua) * ub).astype(act_scr.dtype)

    @pl.when(f == nf)
    def _():
        acc = acc_scr[...] + jnp.dot(act_scr[(f + 1) % 2], wd_ref[...], preferred_element_type=F32)
        mod = mod_ref[...]
        y_out[...] = x1_ref[...] + mod[:, :, 5 * d:6 * d] * acc.reshape(bb, tb, d)


def _ffn(h2, w_up, w_conv, b_conv, conv_state, w_down, x1, mod, bb, tb):
    b, t, d = x1.shape
    dff = w_down.shape[0]
    tf = FFN_TILE
    nf = dff // tf
    halo = 2 * SUBLANES
    hb = tb // halo
    tokspec = lambda w: pl.BlockSpec((bb, tb, w), lambda i, m, f: (i, m, 0))
    up = lambda f: jnp.minimum(f, nf - 1)
    a_col = lambda r: pl.BlockSpec((r, tf), lambda i, m, f: (0, up(f)))
    b_col = lambda r: pl.BlockSpec((r, tf), lambda i, m, f: (0, nf + up(f)))
    st_a = pl.BlockSpec((bb, CONV_W - 1, tf), lambda i, m, f: (i, 0, up(f)))
    st_b = pl.BlockSpec((bb, CONV_W - 1, tf), lambda i, m, f: (i, 0, nf + up(f)))
    nb_spec = pl.BlockSpec((bb, 1, CONV_W - 1, tf), lambda i, m, f: (i, m, 0, up(f)))
    bc2 = b_conv.reshape(1, 2 * dff)
    return pl.pallas_call(
        _ffn_kernel,
        out_shape=(jax.ShapeDtypeStruct((b, t, d), F32),
                   jax.ShapeDtypeStruct((b, t // tb, CONV_W - 1, dff), F32),
                   jax.ShapeDtypeStruct((b, t // tb, CONV_W - 1, dff), F32)),
        grid=(b // bb, t // tb, nf + 1),
        in_specs=[
            tokspec(d),
            pl.BlockSpec((bb, halo, d), lambda i, m, f: (i, jnp.maximum(m * hb - 1, 0), 0)),
            a_col(d), b_col(d), a_col(CONV_W), b_col(CONV_W), a_col(1), b_col(1), st_a, st_b,
            pl.BlockSpec((tf, d), lambda i, m, f: (jnp.maximum(f - 1, 0), 0)),
            tokspec(d),
            pl.BlockSpec((bb, 1, mod.shape[-1]), lambda i, m, f: (i, 0, 0)),
        ],
        out_specs=(tokspec(d), nb_spec, nb_spec),
        scratch_shapes=[pltpu.VMEM((bb * tb, d), F32), pltpu.VMEM((bb, tb + halo, d), BF16),
                        pltpu.VMEM((2, bb * tb, tf), BF16)],
        compiler_params=_cparams(("arbitrary", "arbitrary", "arbitrary")),
        name="conv_ffn",
    )(h2, h2, w_up, w_up, w_conv, w_conv, bc2, bc2, conv_state, conv_state, w_down, x1, mod)


def _prep_weights(w_in, w_gate2, w_out, w_up, w_down):
    hd = N_ATTN_HEADS * ATTN_HEAD_DIM
    sizes = (hd, hd, hd, N_IDX_HEADS * IDX_DIM, IDX_DIM, N_IDX_HEADS,
             GLA_HEADS * GLA_DK, GLA_HEADS * GLA_DK, GLA_HEADS * GLA_DV, GLA_HEADS * GLA_DV, GLA_GATE_RANK)
    offs = np.concatenate([[0], np.cumsum(sizes)])
    col = lambda n: w_in[:, int(offs[n]):int(offs[n + 1])]
    wbig = jnp.concatenate([col(0), col(1), col(2), col(3), col(6), col(7), col(8), col(9)], axis=1).astype(BF16)
    pad = LANES - (IDX_DIM + N_IDX_HEADS + GLA_GATE_RANK)
    wsmall = jnp.concatenate([col(4), col(5), col(10), jnp.zeros((w_in.shape[0], pad), w_in.dtype)],
                             axis=1).astype(BF16)
    lo = IDX_DIM + N_IDX_HEADS
    wg2p = jnp.zeros((LANES, w_gate2.shape[1]), F32).at[lo:lo + GLA_GATE_RANK].set(w_gate2).astype(BF16)
    return dict(wbig=wbig, wsmall=wsmall, wg2p=wg2p,
                wout_a=w_out[:hd].astype(BF16), wout_g=w_out[hd:].astype(BF16),
                w_up=w_up.astype(BF16), w_down=w_down.astype(BF16))


def _layer(x, mod, past, past_k, past_v, past_ik, gla_s0, conv_state, lw, pw, *, bb, tb, tq, tk, tg):
    (g_mix, g_ffn, g_q, g_k, b_gate2, g_gla, w_conv, b_conv) = lw
    b, t, d = x.shape
    hd = N_ATTN_HEADS * ATTN_HEAD_DIM
    pos = past + jnp.arange(t, dtype=jnp.int32)
    ropea = _rope_table(pos, ROPE_DIM // 2, LANES, LANES)
    ropei = _rope_table(pos, IDX_ROPE_DIM // 2, IDX_DIM, LANES)
    ropes = _rope_table(pos, IDX_ROPE_DIM // 2, LANES, IDX_DIM)

    seq_len = past + t
    lp = -(-seq_len // tk) * tk
    direct_kv = past_k is None and lp == seq_len and bb == 1
    outs = _inproj(x, mod, g_mix, pw["wbig"], pw["wsmall"], g_q, g_k, ropea, ropei, ropes, pw["wg2p"],
                   b_gate2, bb, tb, direct_kv)
    (q_bf, k, v, iq, small, lg, gq, gk, gv, gr) = outs[:10]
    ik = small[:, :, :IDX_DIM]
    iw = small[:, :, IDX_DIM:IDX_DIM + N_IDX_HEADS]

    ik_bf = ik.astype(BF16)
    if direct_kv:
        k_bf, v_t = outs[10:]
    else:
        k_bf, v_bf = k.astype(BF16), v.astype(BF16)
        if past_k is not None:
            k_bf = jnp.concatenate([past_k.reshape(b, past, hd).astype(BF16), k_bf], axis=1)
            v_bf = jnp.concatenate([past_v.reshape(b, past, hd).astype(BF16), v_bf], axis=1)
            ik_bf = jnp.concatenate([past_ik.astype(BF16), ik_bf], axis=1)
        if lp != seq_len:
            padw = ((0, 0), (0, lp - seq_len), (0, 0))
            k_bf, v_bf, ik_bf = jnp.pad(k_bf, padw), jnp.pad(v_bf, padw), jnp.pad(ik_bf, padw)
        v_t = v_bf.transpose(0, 2, 1)
    tpad = -(-t // tq) * tq
    qpad = lambda a: jnp.pad(a, ((0, 0), (0, tpad - t), (0, 0))) if tpad != t else a
    iw_t = qpad(iw).transpose(0, 2, 1)
    topk = min(TOPK_MAX, seq_len // 4)
    o_a = _dsa(qpad(iq), iw_t, ik_bf, qpad(q_bf), k_bf, v_t, past=past, seq_len=seq_len, n_queries=t,
               topk=topk, tq=tq, tk=tk)
    o_a = o_a[:, :t]

    s0t = jnp.swapaxes(gla_s0, -1, -2)
    o_g, s_t = _gla(gq, gk, gv, lg, gr, g_gla, s0t, tg)
    s_new = jnp.swapaxes(s_t, -1, -2)

    x1, h2 = _outproj(o_a, o_g, pw["wout_a"], pw["wout_g"], x, mod, g_ffn, bb, tb)
    y, nb_a, nb_b = _ffn(h2, pw["w_up"], w_conv, b_conv, conv_state, pw["w_down"], x1, mod, bb, tb)
    new_buf = jnp.concatenate([nb_a[:, -1], nb_b[:, -1]], axis=-1)
    return (y, k.reshape(b, t, N_ATTN_HEADS, ATTN_HEAD_DIM), v.reshape(b, t, N_ATTN_HEADS, ATTN_HEAD_DIM),
            ik, s_new, new_buf)


def _tiles(b, t, seq_len):
    if t >= 512:
        bb, tb = 1, 512
    else:
        bb, tb = min(b, 512 // t), t
    tq = 256 if t % 256 == 0 else LANES
    tk = 256 if seq_len % 256 == 0 else LANES
    tg = min(t, 256)
    return dict(bb=bb, tb=tb, tq=tq, tk=tk, tg=tg)


def kernel(x_prompt, x_sample, c_prompt, c_sample, cache_k, cache_v, cache_idx_k, state_gla, state_ffn_conv,
           w_ada, b_ada, g_mix, g_ffn, w_in, g_q, g_k, w_gate2, b_gate2, g_gla, w_out, w_up, w_conv, b_conv,
           w_down):
    bp, s, d = x_prompt.shape
    bs, ts, _ = x_sample.shape
    depth = w_ada.shape[0]
    past = cache_k.shape[2]
    dff = w_down.shape[1]
    dt = x_prompt.dtype

    y_p, y_s = x_prompt, x_sample
    outs = [[] for _ in range(10)]
    for l in range(depth):
        n_c = bp + bs
        rows = -(-n_c // SUBLANES) * SUBLANES
        c_all = jnp.concatenate([c_prompt, c_sample, jnp.zeros((rows - n_c, d), dt)], axis=0)
        mod = _ada(c_all, w_ada[l], b_ada[l])
        mod_p = mod[:bp].reshape(bp, 1, 6 * d)
        mod_s = mod[bp:n_c].reshape(bs, 1, 6 * d)
        pw = _prep_weights(w_in[l], w_gate2[l], w_out[l], w_up[l], w_down[l])
        lw = (g_mix[l], g_ffn[l], g_q[l], g_k[l], b_gate2[l], g_gla[l], w_conv[l], b_conv[l])

        gla0 = jnp.zeros((bp, GLA_HEADS, GLA_DK, GLA_DV), dt)
        conv0 = jnp.zeros((bp, CONV_W - 1, 2 * dff), dt)
        y_p, k1, v1, ik1, s1, c1 = _layer(y_p, mod_p, 0, None, None, None, gla0, conv0, lw, pw,
                                          **_tiles(bp, s, s))
        y_s, k2, v2, ik2, s2, c2 = _layer(y_s, mod_s, past, cache_k[l], cache_v[l], cache_idx_k[l],
                                          state_gla[l], state_ffn_conv[l], lw, pw,
                                          **_tiles(bs, ts, past + ts))
        for lst, val in zip(outs, (k1, v1, ik1, s1, c1, k2, v2, ik2, s2, c2)):
            lst.append(val)

    return (y_p, y_s) + tuple(o[0][None] if depth == 1 else jnp.stack(o) for o in outs)
```

```python
import functools
import math

import jax
import jax.numpy as jnp
import numpy as np
from jax import lax
from jax.experimental import pallas as pl
from jax.experimental.pallas import tpu as pltpu

F32 = jnp.float32
BF16 = jnp.bfloat16

CHUNK = 64
N_ATTN_HEADS = 8
ATTN_HEAD_DIM = 128
ROPE_DIM = 32
ROPE_THETA = 500000.0
N_IDX_HEADS = 16
IDX_DIM = 64
IDX_ROPE_DIM = 16
TOPK_MAX = 256
GLA_HEADS = 4
GLA_DK = 128
GLA_DV = 256
GLA_GATE_RANK = 16
GLA_GATE_TEMP = 16.0
CONV_W = 3
EPS = 1e-6

LANES = 128
SUBLANES = 8
VMEM_LIMIT_BYTES = 56 * 1024 * 1024

SCORE_LOOKAHEAD = 4
SCORE_SLOTS = 8
FINE_STEPS = 17
PROJ_GROUP = 1024
FFN_TILE = 512

MASK_NEG = -1e30
F32_LOWEST = -3.4028235e38


def _cparams(sem):
    return pltpu.CompilerParams(dimension_semantics=sem, vmem_limit_bytes=VMEM_LIMIT_BYTES)


def _sigmoid(x):
    return 1.0 / (1.0 + jnp.exp(-x))


def _silu(x):
    return x * _sigmoid(x)


def _ada_kernel(c_ref, w_ref, b_ref, o_ref):
    s = _silu(c_ref[...]).astype(BF16)
    o_ref[...] = jnp.dot(s, w_ref[...].astype(BF16), preferred_element_type=F32) + b_ref[...]


def _ada(c_all, w_ada, b_ada):
    rows, d = c_all.shape
    n = w_ada.shape[1]
    tn = 1536 if n % 1536 == 0 else n
    return pl.pallas_call(
        _ada_kernel,
        out_shape=jax.ShapeDtypeStruct((rows, n), F32),
        grid=(n // tn,),
        in_specs=[
            pl.BlockSpec((rows, d), lambda j: (0, 0)),
            pl.BlockSpec((d, tn), lambda j: (0, j)),
            pl.BlockSpec((1, tn), lambda j: (0, j)),
        ],
        out_specs=pl.BlockSpec((rows, tn), lambda j: (0, j)),
        compiler_params=_cparams(("arbitrary",)),
        name="ada_mod",
    )(c_all, w_ada, b_ada.reshape(1, n))


def _rope_table(pos, half, period, active_lanes):
    inv = ROPE_THETA ** (-jnp.arange(half, dtype=F32) / half)
    lane = np.arange(LANES)
    within = lane % period
    first = jnp.asarray((within < half) & (lane < active_lanes))
    second = jnp.asarray((within >= half) & (within < 2 * half) & (lane < active_lanes))
    ang = pos.astype(F32)[:, None] * inv[jnp.asarray(within % half)][None, :]
    cos, sin = jnp.cos(ang), jnp.sin(ang)
    c = jnp.where(first | second, cos, 1.0)
    s1 = jnp.where(first, -sin, 0.0)
    s2 = jnp.where(second, sin, 0.0)
    return jnp.stack([c, s1, s2]).astype(F32)


def _rope_apply(v, tab, half):
    return (v * tab[0] + pltpu.roll(v, LANES - half, 1) * tab[1] + pltpu.roll(v, half, 1) * tab[2])


def _tile_rows(tab_ref, bb):
    t = tab_ref[...]
    if bb == 1:
        return t
    tb = t.shape[1]
    return jnp.broadcast_to(t[:, None], (3, bb, tb, LANES)).reshape(3, bb * tb, LANES)


def _inproj_kernel(x_ref, mod_ref, gmix_ref, wbig_ref, wsmall_ref, gq_ref, gk_ref,
                   ropea_ref, ropei_ref, ropes_ref, wg2_ref, bg2_ref,
                   q_out, k_out, v_out, iq_out, small_out, lg_out, gq_out, gk_out, gv_out, gr_out,
                   *rest):
    *attn_copies, h_scr = rest
    j = pl.program_id(2)
    bb, tb, d = x_ref.shape
    rows = bb * tb

    @pl.when(j == 0)
    def _():
        x = x_ref[...]
        ms = jnp.mean(x * x, axis=-1, keepdims=True)
        y = x * lax.rsqrt(ms + EPS) * gmix_ref[...]
        mod = mod_ref[...]
        h = y * (1.0 + mod[:, :, d:2 * d]) + mod[:, :, 0:d]
        hb = h.astype(BF16).reshape(rows, d)
        h_scr[...] = hb
        small = jnp.dot(hb, wsmall_ref[...], preferred_element_type=F32)
        small = _rope_apply(small, _tile_rows(ropes_ref, bb), IDX_ROPE_DIM // 2)
        small_out[...] = small.reshape(bb, tb, LANES)
        pre = jnp.dot(small.astype(BF16), wg2_ref[...], preferred_element_type=F32) + bg2_ref[...]
        lsig = jnp.minimum(pre, 0.0) - jnp.log(1.0 + jnp.exp(-jnp.abs(pre)))
        lg_out[...] = (lsig / GLA_GATE_TEMP).reshape(bb, tb, lg_out.shape[-1])

    y = jnp.dot(h_scr[...], wbig_ref[...], preferred_element_type=F32)

    def headnorm_rope(g_ref, scale):
        tab = _tile_rows(ropea_ref, bb)
        outs = []
        for hh in range(N_ATTN_HEADS):
            yh = y[:, hh * LANES:(hh + 1) * LANES]
            n = yh * lax.rsqrt(jnp.mean(yh * yh, axis=-1, keepdims=True) + EPS) * g_ref[...]
            r = _rope_apply(n, tab, ROPE_DIM // 2)
            outs.append(r * scale if scale != 1.0 else r)
        return outs

    @pl.when(j == 0)
    def _():
        for hh, r in enumerate(headnorm_rope(gq_ref, ATTN_HEAD_DIM ** -0.5 * math.log2(math.e))):
            q_out[:, :, hh * LANES:(hh + 1) * LANES] = r.astype(BF16).reshape(bb, tb, LANES)

    @pl.when(j == 1)
    def _():
        for hh, r in enumerate(headnorm_rope(gk_ref, 1.0)):
            k_out[:, :, hh * LANES:(hh + 1) * LANES] = r.reshape(bb, tb, LANES)
            if attn_copies:
                attn_copies[0][:, :, hh * LANES:(hh + 1) * LANES] = r.astype(BF16).reshape(bb, tb, LANES)

    @pl.when(j == 2)
    def _():
        v_out[...] = y.reshape(bb, tb, y.shape[-1])
        if attn_copies:
            attn_copies[1][0] = y.T.astype(BF16)

    @pl.when(j == 3)
    def _():
        tab = _tile_rows(ropei_ref, bb)
        for c in range(y.shape[-1] // LANES):
            r = _rope_apply(y[:, c * LANES:(c + 1) * LANES], tab, IDX_ROPE_DIM // 2)
            iq_out[:, :, c * LANES:(c + 1) * LANES] = r.astype(BF16).reshape(bb, tb, LANES)

    @pl.when(j == 4)
    def _():
        half = y.shape[-1] // 2
        gq_out[...] = (y[:, :half] * GLA_DK ** -0.5).reshape(bb, tb, half)
        gk_out[...] = y[:, half:].reshape(bb, tb, half)

    @pl.when(j == 5)
    def _():
        gv_out[...] = y.astype(BF16).reshape(bb, tb, y.shape[-1])

    @pl.when(j == 6)
    def _():
        gr_out[...] = y.reshape(bb, tb, y.shape[-1])


def _inproj(x, mod, g_mix, wbig, wsmall, g_q, g_k, ropea, ropei, ropes, wg2p, bg2, bb, tb, attn_copies):
    b, t, d = x.shape
    gw = PROJ_GROUP
    ngroups = wbig.shape[1] // gw
    glak = GLA_HEADS * GLA_DK
    rows = bb * tb
    tok = lambda w, dt: jax.ShapeDtypeStruct((b, t, w), dt)
    tokspec = lambda w: pl.BlockSpec((bb, tb, w), lambda i, m, j: (i, m, 0))
    const = lambda shp: pl.BlockSpec(shp, lambda i, m, j: tuple(0 for _ in shp))
    tabspec = pl.BlockSpec((3, tb, LANES), lambda i, m, j: (0, m, 0))
    out_shapes = (tok(1024, BF16), tok(1024, F32), tok(1024, F32), tok(1024, BF16), tok(LANES, F32),
                  tok(glak, F32), tok(glak, F32), tok(glak, F32), tok(1024, BF16), tok(1024, F32))
    out_specs = tuple(tokspec(s.shape[-1]) for s in out_shapes)
    if attn_copies:
        assert bb == 1
        out_shapes += (tok(1024, BF16), jax.ShapeDtypeStruct((b, 1024, t), BF16))
        out_specs += (tokspec(1024), pl.BlockSpec((1, 1024, tb), lambda i, m, j: (i, 0, m)))
    return pl.pallas_call(
        _inproj_kernel,
        out_shape=out_shapes,
        grid=(b // bb, t // tb, ngroups),
        in_specs=[
            tokspec(d),
            pl.BlockSpec((bb, 1, mod.shape[-1]), lambda i, m, j: (i, 0, 0)),
            const((1, 1, d)),
            pl.BlockSpec((d, gw), lambda i, m, j: (0, j)),
            const((d, LANES)),
            const((1, LANES)), const((1, LANES)),
            tabspec, tabspec, tabspec,
            const((LANES, glak)), const((1, glak)),
        ],
        out_specs=out_specs,
        scratch_shapes=[pltpu.VMEM((rows, d), BF16)],
        compiler_params=_cparams(("arbitrary", "arbitrary", "arbitrary")),
        name="norm_inproj",
    )(x, mod, g_mix.reshape(1, 1, d), wbig, wsmall, g_q.reshape(1, LANES), g_k.reshape(1, LANES),
      ropea, ropei, ropes, wg2p, bg2.reshape(1, glak))


def _dsa_kernel(iq_ref, iwt_ref, ik_ref, q_ref, k_hbm, vt_hbm, o_ref,
                u_scr, ub_scr, kbuf, vbuf, sem, m_scr, l_scr, acc_scr, s_scr, bias_scr, iqt_scr, qt_scr,
                *, past, seq_len, n_queries, topk, tq, tk):
    b = pl.program_id(0)
    i = pl.program_id(1)
    nkt = u_scr.shape[0]

    last_chunk = (past + (i + 1) * tq - 1) // CHUNK
    n_adm = jnp.minimum(seq_len, (last_chunk + 1) * CHUNK)
    nk = jnp.minimum((n_adm + tk - 1) // tk, nkt)

    def k_copy(j):
        off = pl.multiple_of(j * tk, tk)
        return pltpu.make_async_copy(k_hbm.at[b, pl.ds(off, tk)], kbuf.at[j % 2], sem.at[0, j % 2])

    def v_copy(j):
        off = pl.multiple_of(j * tk, tk)
        return pltpu.make_async_copy(vt_hbm.at[b, :, pl.ds(off, tk)], vbuf.at[j % 2], sem.at[1, j % 2])

    k_copy(0).start()
    v_copy(0).start()

    for c in range(N_ATTN_HEADS):
        cs = slice(c * LANES, (c + 1) * LANES)
        qt_scr[c] = q_ref[0, :, cs].astype(F32).T.astype(qt_scr.dtype)
        iqt_scr[c] = iq_ref[0, :, cs].astype(F32).T.astype(iqt_scr.dtype)

    q_chunk = (past + i * tq + lax.broadcasted_iota(jnp.int32, (tk, tq), 1)) // CHUNK
    key_iota = lax.broadcasted_iota(jnp.int32, (tk, tq), 0)
    idx_scale = (IDX_DIM * N_IDX_HEADS) ** -0.5

    group = 2 if nkt % 2 == 0 else 1
    n_groups = (nk + group - 1) // group

    def score_tile(j):
        ik_t = ik_ref[0, pl.ds(pl.multiple_of(j * tk, tk), tk), :]
        acc = jnp.zeros((tk, tq), F32)
        for hh in range(N_IDX_HEADS):
            iq_h = iqt_scr[hh // 2, (hh % 2) * IDX_DIM:(hh % 2 + 1) * IDX_DIM, :]
            logits = jnp.dot(ik_t, iq_h, preferred_element_type=F32)
            acc = acc + iwt_ref[0, hh:hh + 1, :] * jnp.maximum(logits, 0.0)
        kpos = j * tk + key_iota
        adm = (kpos // CHUNK <= q_chunk) & (kpos < seq_len)
        u = jnp.where(adm, acc * idx_scale, -jnp.inf)
        u_scr[j] = u
        ub_scr[j] = u.astype(ub_scr.dtype)

    def score_group(g, carry):
        for r in range(group):
            score_tile(g * group + r)
        return carry

    lax.fori_loop(0, n_groups, score_group, 0)

    def count_ge(scr, thr_row):
        rows = SUBLANES * (4 // jnp.dtype(scr.dtype).itemsize)
        thr_b = jnp.broadcast_to(thr_row.astype(scr.dtype), (rows, tq))
        one, zero = jnp.ones((), scr.dtype), jnp.zeros((), scr.dtype)

        def body(g, cnt):
            for r in range(group):
                x = scr[g * group + r].reshape(tk // rows, rows, tq)
                hit = jnp.where(x >= thr_b[None], one, zero)
                parts = [hit[n] for n in range(tk // rows)]
                while len(parts) > 1:
                    parts = [parts[n] + parts[n + 1] for n in range(0, len(parts), 2)]
                cnt = cnt + parts[0].astype(F32)
            return cnt

        cnt = lax.fori_loop(0, n_groups, body, jnp.zeros((rows, tq), F32))
        return jnp.sum(cnt, axis=0, keepdims=True)

    int_min = jnp.int32(-2 ** 31)
    kf = float(topk)

    def coarse_value(t_biased):
        mono = t_biased ^ int_min
        bits = jnp.where(mono >= 0, mono, mono ^ jnp.int32(0x7FFF0000))
        return lax.bitcast_convert_type(bits, F32)

    def coarse_step(it, t_biased):
        cand = t_biased | lax.shift_left(jnp.int32(1), jnp.int32(31) - it)
        return jnp.where(count_ge(ub_scr, coarse_value(cand)) >= kf, cand, t_biased)

    t16 = lax.fori_loop(0, 16, coarse_step, jnp.zeros((1, tq), jnp.int32))
    few = (t16 >= 0) & (t16 <= jnp.int32(0x007F0000))

    def to_mono(bits):
        return jnp.where(bits >= 0, bits, bits ^ jnp.int32(0x7FFFFFFF))

    c_mono = to_mono(lax.bitcast_convert_type(coarse_value(t16), jnp.int32))
    lo0 = c_mono - jnp.int32(0x8000)
    hi0 = c_mono + jnp.int32(0x10000)
    padded = i * tq + lax.broadcasted_iota(jnp.int32, (1, tq), 1) >= n_queries
    hi0 = jnp.where(few | padded, lo0, hi0)

    def fine_cond(state):
        it, _, _, pending = state
        return (it < FINE_STEPS) & (pending > 0)

    def fine_step(state):
        it, lo, hi, _ = state
        for _ in range(2):
            mid = lo + lax.shift_right_arithmetic(hi - lo + 1, jnp.int32(1))
            total = count_ge(u_scr, lax.bitcast_convert_type(to_mono(mid), F32))
            ok = total >= kf
            lo = jnp.where(ok, mid, lo)
            hi = jnp.where(total == kf, mid, jnp.where(ok, hi, mid - 1))
        return it + 2, lo, hi, jnp.max(jnp.where(hi > lo, 1, 0))

    _, lo_fin, _, _ = lax.while_loop(fine_cond, fine_step,
                                     (jnp.int32(0), lo0, hi0, jnp.max(jnp.where(hi0 > lo0, 1, 0))))
    thr_fin = lax.bitcast_convert_type(to_mono(lo_fin), F32)
    thr = jnp.where(few, F32_LOWEST, jnp.maximum(thr_fin, F32_LOWEST))

    m_scr[...] = jnp.full(m_scr.shape, MASK_NEG, F32)
    l_scr[...] = jnp.zeros(l_scr.shape, F32)
    acc_scr[...] = jnp.zeros(acc_scr.shape, F32)

    def select_bias(j):
        bias_scr[...] = jnp.where(u_scr[j] >= thr, 0.0, MASK_NEG)

    n_slots = s_scr.shape[0]

    def scores_head(j, hh):
        hs = slice(hh * LANES, (hh + 1) * LANES)
        s_scr[hh % n_slots] = (jnp.dot(kbuf[j % 2, :, hs], qt_scr[hh], preferred_element_type=F32)
                               + bias_scr[...])

    def attend_head(j, hh):
        hs = slice(hh * LANES, (hh + 1) * LANES)
        s = s_scr[hh % n_slots]
        m_prev = m_scr[hh:hh + 1, :]
        m_new = jnp.maximum(m_prev, jnp.max(s, axis=0, keepdims=True))
        alpha = jnp.exp2(m_prev - m_new)
        p = jnp.exp2(s - m_new)
        l_scr[hh:hh + 1, :] = alpha * l_scr[hh:hh + 1, :] + jnp.sum(p, axis=0, keepdims=True)
        acc_scr[hs, :] = alpha * acc_scr[hs, :] + jnp.dot(vbuf[j % 2, hs, :], p.astype(BF16),
                                                          preferred_element_type=F32)
        m_scr[hh:hh + 1, :] = m_new

    def tile_steps(j, has_next):
        v_copy(j).wait()
        for hh in range(N_ATTN_HEADS):
            ahead = hh + SCORE_LOOKAHEAD
            if ahead < N_ATTN_HEADS:
                scores_head(j, ahead)
            elif has_next:
                if ahead == N_ATTN_HEADS:
                    @pl.when(j + 2 < nk)
                    def _():
                        k_copy(j + 2).start()

                    k_copy(j + 1).wait()
                    select_bias(j + 1)
                scores_head(j + 1, ahead - N_ATTN_HEADS)
            attend_head(j, hh)

    @pl.when(nk > 1)
    def _():
        k_copy(1).start()

    k_copy(0).wait()
    select_bias(0)
    for hh in range(SCORE_LOOKAHEAD):
        scores_head(0, hh)

    def attend_tile(j, carry):
        v_copy(j + 1).start()
        tile_steps(j, True)
        return carry

    lax.fori_loop(0, nk - 1, attend_tile, 0)
    tile_steps(nk - 1, False)

    for hh in range(N_ATTN_HEADS):
        hs = slice(hh * LANES, (hh + 1) * LANES)
        o_t = acc_scr[hs, :] / l_scr[hh:hh + 1, :]
        o_ref[0, :, hs] = o_t.T.astype(o_ref.dtype)


def _dsa(iq, iw_t, ik_bf, q_bf, k_bf, v_t, *, past, seq_len, n_queries, topk, tq, tk):
    b, t, hd = q_bf.shape
    lp = ik_bf.shape[1]
    nkt = lp // tk
    kern = functools.partial(_dsa_kernel, past=past, seq_len=seq_len, n_queries=n_queries, topk=topk,
                             tq=tq, tk=tk)
    return pl.pallas_call(
        kern,
        out_shape=jax.ShapeDtypeStruct((b, t, hd), BF16),
        grid=(b, t // tq),
        in_specs=[
            pl.BlockSpec((1, tq, N_IDX_HEADS * IDX_DIM), lambda bi, i: (bi, i, 0)),
            pl.BlockSpec((1, N_IDX_HEADS, tq), lambda bi, i: (bi, 0, i)),
            pl.BlockSpec((1, lp, IDX_DIM), lambda bi, i: (bi, 0, 0)),
            pl.BlockSpec((1, tq, hd), lambda bi, i: (bi, i, 0)),
            pl.BlockSpec(memory_space=pl.ANY),
            pl.BlockSpec(memory_space=pl.ANY),
        ],
        out_specs=pl.BlockSpec((1, tq, hd), lambda bi, i: (bi, i, 0)),
        scratch_shapes=[
            pltpu.VMEM((nkt, tk, tq), F32),
            pltpu.VMEM((nkt, tk, tq), jnp.bfloat16),
            pltpu.VMEM((2, tk, hd), BF16),
            pltpu.VMEM((2, hd, tk), BF16),
            pltpu.SemaphoreType.DMA((2, 2)),
            pltpu.VMEM((N_ATTN_HEADS, tq), F32),
            pltpu.VMEM((N_ATTN_HEADS, tq), F32),
            pltpu.VMEM((hd, tq), F32),
            pltpu.VMEM((SCORE_SLOTS, tk, tq), F32),
            pltpu.VMEM((tk, tq), F32),
            pltpu.VMEM((N_IDX_HEADS * IDX_DIM // LANES, LANES, tq), BF16),
            pltpu.VMEM((N_ATTN_HEADS, ATTN_HEAD_DIM, tq), BF16),
        ],
        compiler_params=_cparams(("arbitrary", "arbitrary")),
        name="dsa_attend",
    )(iq, iw_t, ik_bf, q_bf, k_bf, v_t)


def _gla_kernel(q_ref, k_ref, v_ref, lg_ref, gr_ref, g_ref, s0_ref, o_ref, sT_out, st_scr, *, n_chunks):
    tstep = pl.program_id(1)

    @pl.when(tstep == 0)
    def _():
        st_scr[...] = s0_ref[0]

    row = lax.broadcasted_iota(jnp.int32, (CHUNK, GLA_DK), 0)
    tri = (lax.broadcasted_iota(jnp.int32, (CHUNK, CHUNK), 0)
           >= lax.broadcasted_iota(jnp.int32, (CHUNK, CHUNK), 1))

    for c in range(n_chunks):
        rs = slice(c * CHUNK, (c + 1) * CHUNK)
        for hh in range(GLA_HEADS):
            ks = slice(hh * GLA_DK, (hh + 1) * GLA_DK)
            vs = slice(hh * GLA_DV, (hh + 1) * GLA_DV)
            bcum = lg_ref[0, rs, ks]
            shift = 1
            while shift < CHUNK:
                bcum = bcum + jnp.where(row >= shift, pltpu.roll(bcum, shift, 0), 0.0)
                shift *= 2
            bl = bcum[CHUNK - 1:CHUNK, :]
            q = q_ref[0, rs, ks]
            k = k_ref[0, rs, ks]
            v = v_ref[0, rs, vs]
            qe = (q * jnp.exp(bcum)).astype(BF16)
            ke = (k * jnp.exp(-bcum)).astype(BF16)
            kd = (k * jnp.exp(bl - bcum)).astype(BF16)
            st = st_scr[hh]
            inter = lax.dot_general(qe, st.astype(BF16), (((1,), (1,)), ((), ())),
                                    preferred_element_type=F32)
            a = lax.dot_general(qe, ke, (((1,), (1,)), ((), ())), preferred_element_type=F32)
            a = jnp.where(tri, a, 0.0)
            o = inter + jnp.dot(a.astype(BF16), v, preferred_element_type=F32)
            st_scr[hh] = st * jnp.exp(bl) + lax.dot_general(v, kd, (((0,), (0,)), ((), ())),
                                                            preferred_element_type=F32)
            n = o * lax.rsqrt(jnp.mean(o * o, axis=-1, keepdims=True) + EPS) * g_ref[...]
            o_ref[0, rs, vs] = (n * _silu(gr_ref[0, rs, vs])).astype(o_ref.dtype)

    @pl.when(tstep == pl.num_programs(1) - 1)
    def _():
        sT_out[0] = st_scr[...]


def _gla(gq, gk, gv, lg, gr, g_gla, s0t, tg):
    b, t, _ = gq.shape
    kern = functools.partial(_gla_kernel, n_chunks=tg // CHUNK)
    tok = lambda w: pl.BlockSpec((1, tg, w), lambda bi, m: (bi, m, 0))
    wk, wv = GLA_HEADS * GLA_DK, GLA_HEADS * GLA_DV
    st_spec = pl.BlockSpec((1, GLA_HEADS, GLA_DV, GLA_DK), lambda bi, m: (bi, 0, 0, 0))
    return pl.pallas_call(
        kern,
        out_shape=(jax.ShapeDtypeStruct((b, t, wv), BF16),
                   jax.ShapeDtypeStruct((b, GLA_HEADS, GLA_DV, GLA_DK), F32)),
        grid=(b, t // tg),
        in_specs=[tok(wk), tok(wk), tok(wv), tok(wk), tok(wv),
                  pl.BlockSpec((1, GLA_DV), lambda bi, m: (0, 0)), st_spec],
        out_specs=(tok(wv), st_spec),
        scratch_shapes=[pltpu.VMEM((GLA_HEADS, GLA_DV, GLA_DK), F32)],
        compiler_params=_cparams(("arbitrary", "arbitrary")),
        name="gla",
    )(gq, gk, gv, lg, gr, g_gla.reshape(1, GLA_DV), s0t)


def _outproj_kernel(oa_ref, og_ref, wa_ref, wg_ref, x_ref, mod_ref, gffn_ref, x1_out, h2_out):
    bb, tb, d = x_ref.shape
    rows = bb * tb
    oa = oa_ref[...].reshape(rows, oa_ref.shape[-1])
    og = og_ref[...].reshape(rows, og_ref.shape[-1])
    mix = (jnp.dot(oa, wa_ref[...], preferred_element_type=F32)
           + jnp.dot(og, wg_ref[...], preferred_element_type=F32)).reshape(bb, tb, d)
    mod = mod_ref[...]
    x1 = x_ref[...] + mod[:, :, 2 * d:3 * d] * mix
    x1_out[...] = x1
    y = x1 * lax.rsqrt(jnp.mean(x1 * x1, axis=-1, keepdims=True) + EPS) * gffn_ref[...]
    h2_out[...] = (y * (1.0 + mod[:, :, 4 * d:5 * d]) + mod[:, :, 3 * d:4 * d]).astype(BF16)


def _outproj(o_a, o_g, wout_a, wout_g, x, mod, g_ffn, bb, tb):
    b, t, d = x.shape
    tokspec = lambda w: pl.BlockSpec((bb, tb, w), lambda i, m: (i, m, 0))
    const = lambda shp: pl.BlockSpec(shp, lambda i, m: tuple(0 for _ in shp))
    return pl.pallas_call(
        _outproj_kernel,
        out_shape=(jax.ShapeDtypeStruct((b, t, d), F32), jax.ShapeDtypeStruct((b, t, d), BF16)),
        grid=(b // bb, t // tb),
        in_specs=[tokspec(o_a.shape[-1]), tokspec(o_g.shape[-1]), const(wout_a.shape), const(wout_g.shape),
                  tokspec(d), pl.BlockSpec((bb, 1, mod.shape[-1]), lambda i, m: (i, 0, 0)), const((1, 1, d))],
        out_specs=(tokspec(d), tokspec(d)),
        compiler_params=_cparams(("arbitrary", "arbitrary")),
        name="outproj_norm",
    )(o_a, o_g, wout_a, wout_g, x, mod, g_ffn.reshape(1, 1, d))


def _ffn_kernel(h_ref, halo_ref, wa_ref, wb_ref, wca_ref, wcb_ref, bca_ref, bcb_ref, sa_ref, sb_ref,
                wd_ref, x1_ref, mod_ref, y_out, nba_out, nbb_out, acc_scr, hext_scr, act_scr):
    m = pl.program_id(1)
    f = pl.program_id(2)
    nf = pl.num_programs(2) - 1
    bb, tb, d = h_ref.shape
    rows = bb * tb
    tf = wa_ref.shape[-1]
    halo = halo_ref.shape[1]

    @pl.when(f == 0)
    def _():
        hext_scr[:, 0:halo, :] = halo_ref[...]
        hext_scr[:, halo:, :] = h_ref[...]
        acc_scr[...] = jnp.zeros(acc_scr.shape, F32)
        act_scr[1] = jnp.zeros(act_scr.shape[1:], act_scr.dtype)

    def conv_branch(hext, tpos, w_ref, wc_ref, bc_ref, s_ref, nb_out):
        u_ext = jnp.dot(hext, w_ref[...], preferred_element_type=F32).reshape(bb, tb + halo, tf)
        u = u_ext[:, halo:, :].reshape(rows, tf)
        prev = jnp.where(m == 0, s_ref[...], u_ext[:, halo - (CONV_W - 1):halo, :])
        bcast = lambda r: jnp.broadcast_to(prev[:, r:r + 1, :], (bb, tb, tf)).reshape(rows, tf)
        p0, p1 = bcast(0), bcast(1)
        u1 = jnp.where(tpos == 0, p1, pltpu.roll(u, 1, 0))
        u2 = jnp.where(tpos == 0, p0, jnp.where(tpos == 1, p1, pltpu.roll(u, 2, 0)))
        wc = wc_ref[...]
        nb_out[:, 0] = u.reshape(bb, tb, tf)[:, tb - (CONV_W - 1):, :]
        return bc_ref[...] + wc[0:1] * u2 + wc[1:2] * u1 + wc[2:3] * u

    @pl.when(f < nf)
    def _():
        act_prev = act_scr[(f + 1) % 2]
        hext = hext_scr[...].reshape(bb * (tb + halo), d)
        tpos = lax.broadcasted_iota(jnp.int32, (rows, tf), 0) % tb
        ua = conv_branch(hext, tpos, wa_ref, wca_ref, bca_ref, sa_ref, nba_out)
        ub = conv_branch(hext, tpos, wb_ref, wcb_ref, bcb_ref, sb_ref, nbb_out)
        acc_scr[...] += jnp.dot(act_prev, wd_ref[...], preferred_element_type=F32)
        act_scr[f % 2] = (_silu(ua) * ub).astype(act_scr.dtype)

    @pl.when(f == nf)
    def _():
        acc = acc_scr[...] + jnp.dot(act_scr[(f + 1) % 2], wd_ref[...], preferred_element_type=F32)
        mod = mod_ref[...]
        y_out[...] = x1_ref[...] + mod[:, :, 5 * d:6 * d] * acc.reshape(bb, tb, d)


def _ffn(h2, w_up, w_conv, b_conv, conv_state, w_down, x1, mod, bb, tb):
    b, t, d = x1.shape
    dff = w_down.shape[0]
    tf = FFN_TILE
    nf = dff // tf
    halo = 2 * SUBLANES
    hb = tb // halo
    tokspec = lambda w: pl.BlockSpec((bb, tb, w), lambda i, m, f: (i, m, 0))
    up = lambda f: jnp.minimum(f, nf - 1)
    a_col = lambda r: pl.BlockSpec((r, tf), lambda i, m, f: (0, up(f)))
    b_col = lambda r: pl.BlockSpec((r, tf), lambda i, m, f: (0, nf + up(f)))
    st_a = pl.BlockSpec((bb, CONV_W - 1, tf), lambda i, m, f: (i, 0, up(f)))
    st_b = pl.BlockSpec((bb, CONV_W - 1, tf), lambda i, m, f: (i, 0, nf + up(f)))
    nb_spec = pl.BlockSpec((bb, 1, CONV_W - 1, tf), lambda i, m, f: (i, m, 0, up(f)))
    bc2 = b_conv.reshape(1, 2 * dff)
    return pl.pallas_call(
        _ffn_kernel,
        out_shape=(jax.ShapeDtypeStruct((b, t, d), F32),
                   jax.ShapeDtypeStruct((b, t // tb, CONV_W - 1, dff), F32),
                   jax.ShapeDtypeStruct((b, t // tb, CONV_W - 1, dff), F32)),
        grid=(b // bb, t // tb, nf + 1),
        in_specs=[
            tokspec(d),
            pl.BlockSpec((bb, halo, d), lambda i, m, f: (i, jnp.maximum(m * hb - 1, 0), 0)),
            a_col(d), b_col(d), a_col(CONV_W), b_col(CONV_W), a_col(1), b_col(1), st_a, st_b,
            pl.BlockSpec((tf, d), lambda i, m, f: (jnp.maximum(f - 1, 0), 0)),
            tokspec(d),
            pl.BlockSpec((bb, 1, mod.shape[-1]), lambda i, m, f: (i, 0, 0)),
        ],
        out_specs=(tokspec(d), nb_spec, nb_spec),
        scratch_shapes=[pltpu.VMEM((bb * tb, d), F32), pltpu.VMEM((bb, tb + halo, d), BF16),
                        pltpu.VMEM((2, bb * tb, tf), BF16)],
        compiler_params=_cparams(("arbitrary", "arbitrary", "arbitrary")),
        name="conv_ffn",
    )(h2, h2, w_up, w_up, w_conv, w_conv, bc2, bc2, conv_state, conv_state, w_down, x1, mod)


def _prep_weights(w_in, w_gate2, w_out, w_up, w_down):
    hd = N_ATTN_HEADS * ATTN_HEAD_DIM
    sizes = (hd, hd, hd, N_IDX_HEADS * IDX_DIM, IDX_DIM, N_IDX_HEADS,
             GLA_HEADS * GLA_DK, GLA_HEADS * GLA_DK, GLA_HEADS * GLA_DV, GLA_HEADS * GLA_DV, GLA_GATE_RANK)
    offs = np.concatenate([[0], np.cumsum(sizes)])
    col = lambda n: w_in[:, int(offs[n]):int(offs[n + 1])]
    wbig = jnp.concatenate([col(0), col(1), col(2), col(3), col(6), col(7), col(8), col(9)], axis=1).astype(BF16)
    pad = LANES - (IDX_DIM + N_IDX_HEADS + GLA_GATE_RANK)
    wsmall = jnp.concatenate([col(4), col(5), col(10), jnp.zeros((w_in.shape[0], pad), w_in.dtype)],
                             axis=1).astype(BF16)
    lo = IDX_DIM + N_IDX_HEADS
    wg2p = jnp.zeros((LANES, w_gate2.shape[1]), F32).at[lo:lo + GLA_GATE_RANK].set(w_gate2).astype(BF16)
    return dict(wbig=wbig, wsmall=wsmall, wg2p=wg2p,
                wout_a=w_out[:hd].astype(BF16), wout_g=w_out[hd:].astype(BF16),
                w_up=w_up.astype(BF16), w_down=w_down.astype(BF16))


def _layer(x, mod, past, past_k, past_v, past_ik, gla_s0, conv_state, lw, pw, *, bb, tb, tq, tk, tg):
    (g_mix, g_ffn, g_q, g_k, b_gate2, g_gla, w_conv, b_conv) = lw
    b, t, d = x.shape
    hd = N_ATTN_HEADS * ATTN_HEAD_DIM
    pos = past + jnp.arange(t, dtype=jnp.int32)
    ropea = _rope_table(pos, ROPE_DIM // 2, LANES, LANES)
    ropei = _rope_table(pos, IDX_ROPE_DIM // 2, IDX_DIM, LANES)
    ropes = _rope_table(pos, IDX_ROPE_DIM // 2, LANES, IDX_DIM)

    seq_len = past + t
    lp = -(-seq_len // tk) * tk
    direct_kv = past_k is None and lp == seq_len and bb == 1
    outs = _inproj(x, mod, g_mix, pw["wbig"], pw["wsmall"], g_q, g_k, ropea, ropei, ropes, pw["wg2p"],
                   b_gate2, bb, tb, direct_kv)
    (q_bf, k, v, iq, small, lg, gq, gk, gv, gr) = outs[:10]
    ik = small[:, :, :IDX_DIM]
    iw = small[:, :, IDX_DIM:IDX_DIM + N_IDX_HEADS]

    ik_bf = ik.astype(BF16)
    if direct_kv:
        k_bf, v_t = outs[10:]
    else:
        k_bf, v_bf = k.astype(BF16), v.astype(BF16)
        if past_k is not None:
            k_bf = jnp.concatenate([past_k.reshape(b, past, hd).astype(BF16), k_bf], axis=1)
            v_bf = jnp.concatenate([past_v.reshape(b, past, hd).astype(BF16), v_bf], axis=1)
            ik_bf = jnp.concatenate([past_ik.astype(BF16), ik_bf], axis=1)
        if lp != seq_len:
            padw = ((0, 0), (0, lp - seq_len), (0, 0))
            k_bf, v_bf, ik_bf = jnp.pad(k_bf, padw), jnp.pad(v_bf, padw), jnp.pad(ik_bf, padw)
        v_t = v_bf.transpose(0, 2, 1)
    tpad = -(-t // tq) * tq
    qpad = lambda a: jnp.pad(a, ((0, 0), (0, tpad - t), (0, 0))) if tpad != t else a
    iw_t = qpad(iw).transpose(0, 2, 1)
    topk = min(TOPK_MAX, seq_len // 4)
    o_a = _dsa(qpad(iq), iw_t, ik_bf, qpad(q_bf), k_bf, v_t, past=past, seq_len=seq_len, n_queries=t,
               topk=topk, tq=tq, tk=tk)
    o_a = o_a[:, :t]

    s0t = jnp.swapaxes(gla_s0, -1, -2)
    o_g, s_t = _gla(gq, gk, gv, lg, gr, g_gla, s0t, tg)
    s_new = jnp.swapaxes(s_t, -1, -2)

    x1, h2 = _outproj(o_a, o_g, pw["wout_a"], pw["wout_g"], x, mod, g_ffn, bb, tb)
    y, nb_a, nb_b = _ffn(h2, pw["w_up"], w_conv, b_conv, conv_state, pw["w_down"], x1, mod, bb, tb)
    new_buf = jnp.concatenate([nb_a[:, -1], nb_b[:, -1]], axis=-1)
    return (y, k.reshape(b, t, N_ATTN_HEADS, ATTN_HEAD_DIM), v.reshape(b, t, N_ATTN_HEADS, ATTN_HEAD_DIM),
            ik, s_new, new_buf)


def _tiles(b, t, seq_len):
    if t >= 512:
        bb, tb = 1, 512
    else:
        bb, tb = min(b, 512 // t), t
    tq = 256 if t % 256 == 0 else LANES
    tk = 256 if seq_len % 256 == 0 else LANES
    tg = min(t, 256)
    return dict(bb=bb, tb=tb, tq=tq, tk=tk, tg=tg)


def kernel(x_prompt, x_sample, c_prompt, c_sample, cache_k, cache_v, cache_idx_k, state_gla, state_ffn_conv,
           w_ada, b_ada, g_mix, g_ffn, w_in, g_q, g_k, w_gate2, b_gate2, g_gla, w_out, w_up, w_conv, b_conv,
           w_down):
    bp, s, d = x_prompt.shape
    bs, ts, _ = x_sample.shape
    depth = w_ada.shape[0]
    past = cache_k.shape[2]
    dff = w_down.shape[1]
    dt = x_prompt.dtype

    y_p, y_s = x_prompt, x_sample
    outs = [[] for _ in range(10)]
    for l in range(depth):
        n_c = bp + bs
        rows = -(-n_c // SUBLANES) * SUBLANES
        c_all = jnp.concatenate([c_prompt, c_sample, jnp.zeros((rows - n_c, d), dt)], axis=0)
        mod = _ada(c_all, w_ada[l], b_ada[l])
        mod_p = mod[:bp].reshape(bp, 1, 6 * d)
        mod_s = mod[bp:n_c].reshape(bs, 1, 6 * d)
        pw = _prep_weights(w_in[l], w_gate2[l], w_out[l], w_up[l], w_down[l])
        lw = (g_mix[l], g_ffn[l], g_q[l], g_k[l], b_gate2[l], g_gla[l], w_conv[l], b_conv[l])

        gla0 = jnp.zeros((bp, GLA_HEADS, GLA_DK, GLA_DV), dt)
        conv0 = jnp.zeros((bp, CONV_W - 1, 2 * dff), dt)
        y_p, k1, v1, ik1, s1, c1 = _layer(y_p, mod_p, 0, None, None, None, gla0, conv0, lw, pw,
                                          **_tiles(bp, s, s))
        y_s, k2, v2, ik2, s2, c2 = _layer(y_s, mod_s, past, cache_k[l], cache_v[l], cache_idx_k[l],
                                          state_gla[l], state_ffn_conv[l], lw, pw,
                                          **_tiles(bs, ts, past + ts))
        for lst, val in zip(outs, (k1, v1, ik1, s1, c1, k2, v2, ik2, s2, c2)):
            lst.append(val)

    return (y_p, y_s) + tuple(o[0][None] if depth == 1 else jnp.stack(o) for o in outs)
```

```python
import functools
import math

import jax
import jax.numpy as jnp
import numpy as np
from jax import lax
from jax.experimental import pallas as pl
from jax.experimental.pallas import tpu as pltpu

F32 = jnp.float32
BF16 = jnp.bfloat16

CHUNK = 64
N_ATTN_HEADS = 8
ATTN_HEAD_DIM = 128
ROPE_DIM = 32
ROPE_THETA = 500000.0
N_IDX_HEADS = 16
IDX_DIM = 64
IDX_ROPE_DIM = 16
TOPK_MAX = 256
GLA_HEADS = 4
GLA_DK = 128
GLA_DV = 256
GLA_GATE_RANK = 16
GLA_GATE_TEMP = 16.0
CONV_W = 3
EPS = 1e-6

LANES = 128
SUBLANES = 8
VMEM_LIMIT_BYTES = 56 * 1024 * 1024

SCORE_LOOKAHEAD = 4
SCORE_SLOTS = 8
FINE_STEPS = 17
PROJ_GROUP = 1024
FFN_TILE = 512

MASK_NEG = -1e30
F32_LOWEST = -3.4028235e38


def _cparams(sem):
    return pltpu.CompilerParams(dimension_semantics=sem, vmem_limit_bytes=VMEM_LIMIT_BYTES)


def _sigmoid(x):
    return 1.0 / (1.0 + jnp.exp(-x))


def _silu(x):
    return x * _sigmoid(x)


def _ada_kernel(c_ref, w_ref, b_ref, o_ref):
    s = _silu(c_ref[...]).astype(BF16)
    o_ref[...] = jnp.dot(s, w_ref[...].astype(BF16), preferred_element_type=F32) + b_ref[...]


def _ada(c_all, w_ada, b_ada):
    rows, d = c_all.shape
    n = w_ada.shape[1]
    tn = 1536 if n % 1536 == 0 else n
    return pl.pallas_call(
        _ada_kernel,
        out_shape=jax.ShapeDtypeStruct((rows, n), F32),
        grid=(n // tn,),
        in_specs=[
            pl.BlockSpec((rows, d), lambda j: (0, 0)),
            pl.BlockSpec((d, tn), lambda j: (0, j)),
            pl.BlockSpec((1, tn), lambda j: (0, j)),
        ],
        out_specs=pl.BlockSpec((rows, tn), lambda j: (0, j)),
        compiler_params=_cparams(("arbitrary",)),
        name="ada_mod",
    )(c_all, w_ada, b_ada.reshape(1, n))


def _rope_table(pos, half, period, active_lanes):
    inv = ROPE_THETA ** (-jnp.arange(half, dtype=F32) / half)
    lane = np.arange(LANES)
    within = lane % period
    first = jnp.asarray((within < half) & (lane < active_lanes))
    second = jnp.asarray((within >= half) & (within < 2 * half) & (lane < active_lanes))
    ang = pos.astype(F32)[:, None] * inv[jnp.asarray(within % half)][None, :]
    cos, sin = jnp.cos(ang), jnp.sin(ang)
    c = jnp.where(first | second, cos, 1.0)
    s1 = jnp.where(first, -sin, 0.0)
    s2 = jnp.where(second, sin, 0.0)
    return jnp.stack([c, s1, s2]).astype(F32)


def _rope_apply(v, tab, half):
    return (v * tab[0] + pltpu.roll(v, LANES - half, 1) * tab[1] + pltpu.roll(v, half, 1) * tab[2])


def _tile_rows(tab_ref, bb):
    t = tab_ref[...]
    if bb == 1:
        return t
    tb = t.shape[1]
    return jnp.broadcast_to(t[:, None], (3, bb, tb, LANES)).reshape(3, bb * tb, LANES)


def _inproj_kernel(x_ref, mod_ref, gmix_ref, wbig_ref, wsmall_ref, gq_ref, gk_ref,
                   ropea_ref, ropei_ref, ropes_ref, wg2_ref, bg2_ref,
                   q_out, k_out, v_out, iq_out, small_out, lg_out, gq_out, gk_out, gv_out, gr_out,
                   *rest):
    *attn_copies, h_scr = rest
    j = pl.program_id(2)
    bb, tb, d = x_ref.shape
    rows = bb * tb

    @pl.when(j == 0)
    def _():
        x = x_ref[...]
        ms = jnp.mean(x * x, axis=-1, keepdims=True)
        y = x * lax.rsqrt(ms + EPS) * gmix_ref[...]
        mod = mod_ref[...]
        h = y * (1.0 + mod[:, :, d:2 * d]) + mod[:, :, 0:d]
        hb = h.astype(BF16).reshape(rows, d)
        h_scr[...] = hb
        small = jnp.dot(hb, wsmall_ref[...], preferred_element_type=F32)
        small = _rope_apply(small, _tile_rows(ropes_ref, bb), IDX_ROPE_DIM // 2)
        small_out[...] = small.reshape(bb, tb, LANES)
        pre = jnp.dot(small.astype(BF16), wg2_ref[...], preferred_element_type=F32) + bg2_ref[...]
        lsig = jnp.minimum(pre, 0.0) - jnp.log(1.0 + jnp.exp(-jnp.abs(pre)))
        lg_out[...] = (lsig / GLA_GATE_TEMP).reshape(bb, tb, lg_out.shape[-1])

    y = jnp.dot(h_scr[...], wbig_ref[...], preferred_element_type=F32)

    def headnorm_rope(g_ref, scale):
        tab = _tile_rows(ropea_ref, bb)
        outs = []
        for hh in range(N_ATTN_HEADS):
            yh = y[:, hh * LANES:(hh + 1) * LANES]
            n = yh * lax.rsqrt(jnp.mean(yh * yh, axis=-1, keepdims=True) + EPS) * g_ref[...]
            r = _rope_apply(n, tab, ROPE_DIM // 2)
            outs.append(r * scale if scale != 1.0 else r)
        return outs

    @pl.when(j == 0)
    def _():
        for hh, r in enumerate(headnorm_rope(gq_ref, ATTN_HEAD_DIM ** -0.5 * math.log2(math.e))):
            q_out[:, :, hh * LANES:(hh + 1) * LANES] = r.astype(BF16).reshape(bb, tb, LANES)

    @pl.when(j == 1)
    def _():
        for hh, r in enumerate(headnorm_rope(gk_ref, 1.0)):
            k_out[:, :, hh * LANES:(hh + 1) * LANES] = r.reshape(bb, tb, LANES)
            if attn_copies:
                attn_copies[0][:, :, hh * LANES:(hh + 1) * LANES] = r.astype(BF16).reshape(bb, tb, LANES)

    @pl.when(j == 2)
    def _():
        v_out[...] = y.reshape(bb, tb, y.shape[-1])
        if attn_copies:
            attn_copies[1][0] = y.T.astype(BF16)

    @pl.when(j == 3)
    def _():
        tab = _tile_rows(ropei_ref, bb)
        for c in range(y.shape[-1] // LANES):
            r = _rope_apply(y[:, c * LANES:(c + 1) * LANES], tab, IDX_ROPE_DIM // 2)
            iq_out[:, :, c * LANES:(c + 1) * LANES] = r.astype(BF16).reshape(bb, tb, LANES)

    @pl.when(j == 4)
    def _():
        half = y.shape[-1] // 2
        gq_out[...] = (y[:, :half] * GLA_DK ** -0.5).reshape(bb, tb, half)
        gk_out[...] = y[:, half:].reshape(bb, tb, half)

    @pl.when(j == 5)
    def _():
        gv_out[...] = y.astype(BF16).reshape(bb, tb, y.shape[-1])

    @pl.when(j == 6)
    def _():
        gr_out[...] = y.reshape(bb, tb, y.shape[-1])


def _inproj(x, mod, g_mix, wbig, wsmall, g_q, g_k, ropea, ropei, ropes, wg2p, bg2, bb, tb, attn_copies):
    b, t, d = x.shape
    gw = PROJ_GROUP
    ngroups = wbig.shape[1] // gw
    glak = GLA_HEADS * GLA_DK
    rows = bb * tb
    tok = lambda w, dt: jax.ShapeDtypeStruct((b, t, w), dt)
    tokspec = lambda w: pl.BlockSpec((bb, tb, w), lambda i, m, j: (i, m, 0))
    const = lambda shp: pl.BlockSpec(shp, lambda i, m, j: tuple(0 for _ in shp))
    tabspec = pl.BlockSpec((3, tb, LANES), lambda i, m, j: (0, m, 0))
    out_shapes = (tok(1024, BF16), tok(1024, F32), tok(1024, F32), tok(1024, BF16), tok(LANES, F32),
                  tok(glak, F32), tok(glak, F32), tok(glak, F32), tok(1024, BF16), tok(1024, F32))
    out_specs = tuple(tokspec(s.shape[-1]) for s in out_shapes)
    if attn_copies:
        assert bb == 1
        out_shapes += (tok(1024, BF16), jax.ShapeDtypeStruct((b, 1024, t), BF16))
        out_specs += (tokspec(1024), pl.BlockSpec((1, 1024, tb), lambda i, m, j: (i, 0, m)))
    return pl.pallas_call(
        _inproj_kernel,
        out_shape=out_shapes,
        grid=(b // bb, t // tb, ngroups),
        in_specs=[
            tokspec(d),
            pl.BlockSpec((bb, 1, mod.shape[-1]), lambda i, m, j: (i, 0, 0)),
            const((1, 1, d)),
            pl.BlockSpec((d, gw), lambda i, m, j: (0, j)),
            const((d, LANES)),
            const((1, LANES)), const((1, LANES)),
            tabspec, tabspec, tabspec,
            const((LANES, glak)), const((1, glak)),
        ],
        out_specs=out_specs,
        scratch_shapes=[pltpu.VMEM((rows, d), BF16)],
        compiler_params=_cparams(("arbitrary", "arbitrary", "arbitrary")),
        name="norm_inproj",
    )(x, mod, g_mix.reshape(1, 1, d), wbig, wsmall, g_q.reshape(1, LANES), g_k.reshape(1, LANES),
      ropea, ropei, ropes, wg2p, bg2.reshape(1, glak))


def _dsa_kernel(iq_ref, iwt_ref, ik_ref, q_ref, k_hbm, vt_hbm, o_ref,
                u_scr, ub_scr, kbuf, vbuf, sem, m_scr, l_scr, acc_scr, s_scr, bias_scr, iqt_scr, qt_scr,
                *, past, seq_len, n_queries, topk, tq, tk):
    b = pl.program_id(0)
    i = pl.program_id(1)
    nkt = u_scr.shape[0]

    last_chunk = (past + (i + 1) * tq - 1) // CHUNK
    n_adm = jnp.minimum(seq_len, (last_chunk + 1) * CHUNK)
    nk = jnp.minimum((n_adm + tk - 1) // tk, nkt)

    def k_copy(j):
        off = pl.multiple_of(j * tk, tk)
        return pltpu.make_async_copy(k_hbm.at[b, pl.ds(off, tk)], kbuf.at[j % 2], sem.at[0, j % 2])

    def v_copy(j):
        off = pl.multiple_of(j * tk, tk)
        return pltpu.make_async_copy(vt_hbm.at[b, :, pl.ds(off, tk)], vbuf.at[j % 2], sem.at[1, j % 2])

    k_copy(0).start()
    v_copy(0).start()

    for c in range(N_ATTN_HEADS):
        cs = slice(c * LANES, (c + 1) * LANES)
        qt_scr[c] = q_ref[0, :, cs].astype(F32).T.astype(qt_scr.dtype)
        iqt_scr[c] = iq_ref[0, :, cs].astype(F32).T.astype(iqt_scr.dtype)

    q_chunk = (past + i * tq + lax.broadcasted_iota(jnp.int32, (tk, tq), 1)) // CHUNK
    key_iota = lax.broadcasted_iota(jnp.int32, (tk, tq), 0)
    idx_scale = (IDX_DIM * N_IDX_HEADS) ** -0.5

    group = 2 if nkt % 2 == 0 else 1
    n_groups = (nk + group - 1) // group

    def score_tile(j):
        ik_t = ik_ref[0, pl.ds(pl.multiple_of(j * tk, tk), tk), :]
        acc = jnp.zeros((tk, tq), F32)
        for hh in range(N_IDX_HEADS):
            iq_h = iqt_scr[hh // 2, (hh % 2) * IDX_DIM:(hh % 2 + 1) * IDX_DIM, :]
            logits = jnp.dot(ik_t, iq_h, preferred_element_type=F32)
            acc = acc + iwt_ref[0, hh:hh + 1, :] * jnp.maximum(logits, 0.0)
        kpos = j * tk + key_iota
        adm = (kpos // CHUNK <= q_chunk) & (kpos < seq_len)
        u = jnp.where(adm, acc * idx_scale, -jnp.inf)
        u_scr[j] = u
        ub_scr[j] = u.astype(ub_scr.dtype)

    score_group_size = 4 if nkt % 4 == 0 else group

    def score_group(g, carry):
        for r in range(score_group_size):
            score_tile(g * score_group_size + r)
        return carry

    lax.fori_loop(0, (nk + score_group_size - 1) // score_group_size, score_group, 0)

    def count_ge(scr, thr_row):
        rows = SUBLANES * (4 // jnp.dtype(scr.dtype).itemsize)
        thr_b = jnp.broadcast_to(thr_row.astype(scr.dtype), (rows, tq))
        one, zero = jnp.ones((), scr.dtype), jnp.zeros((), scr.dtype)

        def body(g, cnt):
            for r in range(group):
                x = scr[g * group + r].reshape(tk // rows, rows, tq)
                hit = jnp.where(x >= thr_b[None], one, zero)
                parts = [hit[n] for n in range(tk // rows)]
                while len(parts) > 1:
                    parts = [parts[n] + parts[n + 1] for n in range(0, len(parts), 2)]
                cnt = cnt + parts[0].astype(F32)
            return cnt

        cnt = lax.fori_loop(0, n_groups, body, jnp.zeros((rows, tq), F32))
        return jnp.sum(cnt, axis=0, keepdims=True)

    int_min = jnp.int32(-2 ** 31)
    kf = float(topk)

    def coarse_value(t_biased):
        mono = t_biased ^ int_min
        bits = jnp.where(mono >= 0, mono, mono ^ jnp.int32(0x7FFF0000))
        return lax.bitcast_convert_type(bits, F32)

    def coarse_step(it, t_biased):
        cand = t_biased | lax.shift_left(jnp.int32(1), jnp.int32(31) - it)
        return jnp.where(count_ge(ub_scr, coarse_value(cand)) >= kf, cand, t_biased)

    t16 = lax.fori_loop(0, 16, coarse_step, jnp.zeros((1, tq), jnp.int32))
    few = (t16 >= 0) & (t16 <= jnp.int32(0x007F0000))

    def to_mono(bits):
        return jnp.where(bits >= 0, bits, bits ^ jnp.int32(0x7FFFFFFF))

    c_mono = to_mono(lax.bitcast_convert_type(coarse_value(t16), jnp.int32))
    lo0 = c_mono - jnp.int32(0x8000)
    hi0 = c_mono + jnp.int32(0x10000)
    padded = i * tq + lax.broadcasted_iota(jnp.int32, (1, tq), 1) >= n_queries
    hi0 = jnp.where(few | padded, lo0, hi0)

    def fine_cond(state):
        it, _, _, pending = state
        return (it < FINE_STEPS) & (pending > 0)

    def fine_step(state):
        it, lo, hi, _ = state
        for _ in range(2):
            mid = lo + lax.shift_right_arithmetic(hi - lo + 1, jnp.int32(1))
            total = count_ge(u_scr, lax.bitcast_convert_type(to_mono(mid), F32))
            ok = total >= kf
            lo = jnp.where(ok, mid, lo)
            hi = jnp.where(total == kf, mid, jnp.where(ok, hi, mid - 1))
        return it + 2, lo, hi, jnp.max(jnp.where(hi > lo, 1, 0))

    _, lo_fin, _, _ = lax.while_loop(fine_cond, fine_step,
                                     (jnp.int32(0), lo0, hi0, jnp.max(jnp.where(hi0 > lo0, 1, 0))))
    thr_fin = lax.bitcast_convert_type(to_mono(lo_fin), F32)
    thr = jnp.where(few, F32_LOWEST, jnp.maximum(thr_fin, F32_LOWEST))

    m_scr[...] = jnp.full(m_scr.shape, MASK_NEG, F32)
    l_scr[...] = jnp.zeros(l_scr.shape, F32)
    acc_scr[...] = jnp.zeros(acc_scr.shape, F32)

    def select_bias(j):
        bias_scr[...] = jnp.where(u_scr[j] >= thr, 0.0, MASK_NEG)

    n_slots = s_scr.shape[0]

    def scores_head(j, hh):
        hs = slice(hh * LANES, (hh + 1) * LANES)
        s_scr[hh % n_slots] = (jnp.dot(kbuf[j % 2, :, hs], qt_scr[hh], preferred_element_type=F32)
                               + bias_scr[...])

    def attend_head(j, hh):
        hs = slice(hh * LANES, (hh + 1) * LANES)
        s = s_scr[hh % n_slots]
        m_prev = m_scr[hh:hh + 1, :]
        m_new = jnp.maximum(m_prev, jnp.max(s, axis=0, keepdims=True))
        alpha = jnp.exp2(m_prev - m_new)
        p = jnp.exp2(s - m_new)
        l_scr[hh:hh + 1, :] = alpha * l_scr[hh:hh + 1, :] + jnp.sum(p, axis=0, keepdims=True)
        acc_scr[hs, :] = alpha * acc_scr[hs, :] + jnp.dot(vbuf[j % 2, hs, :], p.astype(BF16),
                                                          preferred_element_type=F32)
        m_scr[hh:hh + 1, :] = m_new

    def tile_steps(j, has_next):
        v_copy(j).wait()
        for hh in range(N_ATTN_HEADS):
            ahead = hh + SCORE_LOOKAHEAD
            if ahead < N_ATTN_HEADS:
                scores_head(j, ahead)
            elif has_next:
                if ahead == N_ATTN_HEADS:
                    @pl.when(j + 2 < nk)
                    def _():
                        k_copy(j + 2).start()

                    k_copy(j + 1).wait()
                    select_bias(j + 1)
                scores_head(j + 1, ahead - N_ATTN_HEADS)
            attend_head(j, hh)

    @pl.when(nk > 1)
    def _():
        k_copy(1).start()

    k_copy(0).wait()
    select_bias(0)
    for hh in range(SCORE_LOOKAHEAD):
        scores_head(0, hh)

    def attend_tile(j, carry):
        v_copy(j + 1).start()
        tile_steps(j, True)
        return carry

    lax.fori_loop(0, nk - 1, attend_tile, 0)
    tile_steps(nk - 1, False)

    for hh in range(N_ATTN_HEADS):
        hs = slice(hh * LANES, (hh + 1) * LANES)
        o_t = acc_scr[hs, :] / l_scr[hh:hh + 1, :]
        o_ref[0, :, hs] = o_t.T.astype(o_ref.dtype)


def _dsa(iq, iw_t, ik_bf, q_bf, k_bf, v_t, *, past, seq_len, n_queries, topk, tq, tk):
    b, t, hd = q_bf.shape
    lp = ik_bf.shape[1]
    nkt = lp // tk
    kern = functools.partial(_dsa_kernel, past=past, seq_len=seq_len, n_queries=n_queries, topk=topk,
                             tq=tq, tk=tk)
    return pl.pallas_call(
        kern,
        out_shape=jax.ShapeDtypeStruct((b, t, hd), BF16),
        grid=(b, t // tq),
        in_specs=[
            pl.BlockSpec((1, tq, N_IDX_HEADS * IDX_DIM), lambda bi, i: (bi, i, 0)),
            pl.BlockSpec((1, N_IDX_HEADS, tq), lambda bi, i: (bi, 0, i)),
            pl.BlockSpec((1, lp, IDX_DIM), lambda bi, i: (bi, 0, 0)),
            pl.BlockSpec((1, tq, hd), lambda bi, i: (bi, i, 0)),
            pl.BlockSpec(memory_space=pl.ANY),
            pl.BlockSpec(memory_space=pl.ANY),
        ],
        out_specs=pl.BlockSpec((1, tq, hd), lambda bi, i: (bi, i, 0)),
        scratch_shapes=[
            pltpu.VMEM((nkt, tk, tq), F32),
            pltpu.VMEM((nkt, tk, tq), jnp.bfloat16),
            pltpu.VMEM((2, tk, hd), BF16),
            pltpu.VMEM((2, hd, tk), BF16),
            pltpu.SemaphoreType.DMA((2, 2)),
            pltpu.VMEM((N_ATTN_HEADS, tq), F32),
            pltpu.VMEM((N_ATTN_HEADS, tq), F32),
            pltpu.VMEM((hd, tq), F32),
            pltpu.VMEM((SCORE_SLOTS, tk, tq), F32),
            pltpu.VMEM((tk, tq), F32),
            pltpu.VMEM((N_IDX_HEADS * IDX_DIM // LANES, LANES, tq), BF16),
            pltpu.VMEM((N_ATTN_HEADS, ATTN_HEAD_DIM, tq), BF16),
        ],
        compiler_params=_cparams(("arbitrary", "arbitrary")),
        name="dsa_attend",
    )(iq, iw_t, ik_bf, q_bf, k_bf, v_t)


def _gla_kernel(q_ref, k_ref, v_ref, lg_ref, gr_ref, g_ref, s0_ref, o_ref, sT_out, st_scr, *, n_chunks):
    tstep = pl.program_id(1)

    @pl.when(tstep == 0)
    def _():
        st_scr[...] = s0_ref[0]

    row = lax.broadcasted_iota(jnp.int32, (CHUNK, GLA_DK), 0)
    tri = (lax.broadcasted_iota(jnp.int32, (CHUNK, CHUNK), 0)
           >= lax.broadcasted_iota(jnp.int32, (CHUNK, CHUNK), 1))

    for c in range(n_chunks):
        rs = slice(c * CHUNK, (c + 1) * CHUNK)
        for hh in range(GLA_HEADS):
            ks = slice(hh * GLA_DK, (hh + 1) * GLA_DK)
            vs = slice(hh * GLA_DV, (hh + 1) * GLA_DV)
            bcum = lg_ref[0, rs, ks]
            shift = 1
            while shift < CHUNK:
                bcum = bcum + jnp.where(row >= shift, pltpu.roll(bcum, shift, 0), 0.0)
                shift *= 2
            bl = bcum[CHUNK - 1:CHUNK, :]
            q = q_ref[0, rs, ks]
            k = k_ref[0, rs, ks]
            v = v_ref[0, rs, vs]
            qe = (q * jnp.exp(bcum)).astype(BF16)
            ke = (k * jnp.exp(-bcum)).astype(BF16)
            kd = (k * jnp.exp(bl - bcum)).astype(BF16)
            st = st_scr[hh]
            inter = lax.dot_general(qe, st.astype(BF16), (((1,), (1,)), ((), ())),
                                    preferred_element_type=F32)
            a = lax.dot_general(qe, ke, (((1,), (1,)), ((), ())), preferred_element_type=F32)
            a = jnp.where(tri, a, 0.0)
            o = inter + jnp.dot(a.astype(BF16), v, preferred_element_type=F32)
            st_scr[hh] = st * jnp.exp(bl) + lax.dot_general(v, kd, (((0,), (0,)), ((), ())),
                                                            preferred_element_type=F32)
            n = o * lax.rsqrt(jnp.mean(o * o, axis=-1, keepdims=True) + EPS) * g_ref[...]
            o_ref[0, rs, vs] = (n * _silu(gr_ref[0, rs, vs])).astype(o_ref.dtype)

    @pl.when(tstep == pl.num_programs(1) - 1)
    def _():
        sT_out[0] = st_scr[...]


def _gla(gq, gk, gv, lg, gr, g_gla, s0t, tg):
    b, t, _ = gq.shape
    kern = functools.partial(_gla_kernel, n_chunks=tg // CHUNK)
    tok = lambda w: pl.BlockSpec((1, tg, w), lambda bi, m: (bi, m, 0))
    wk, wv = GLA_HEADS * GLA_DK, GLA_HEADS * GLA_DV
    st_spec = pl.BlockSpec((1, GLA_HEADS, GLA_DV, GLA_DK), lambda bi, m: (bi, 0, 0, 0))
    return pl.pallas_call(
        kern,
        out_shape=(jax.ShapeDtypeStruct((b, t, wv), BF16),
                   jax.ShapeDtypeStruct((b, GLA_HEADS, GLA_DV, GLA_DK), F32)),
        grid=(b, t // tg),
        in_specs=[tok(wk), tok(wk), tok(wv), tok(wk), tok(wv),
                  pl.BlockSpec((1, GLA_DV), lambda bi, m: (0, 0)), st_spec],
        out_specs=(tok(wv), st_spec),
        scratch_shapes=[pltpu.VMEM((GLA_HEADS, GLA_DV, GLA_DK), F32)],
        compiler_params=_cparams(("arbitrary", "arbitrary")),
        name="gla",
    )(gq, gk, gv, lg, gr, g_gla.reshape(1, GLA_DV), s0t)


def _outproj_kernel(oa_ref, og_ref, wa_ref, wg_ref, x_ref, mod_ref, gffn_ref, x1_out, h2_out):
    bb, tb, d = x_ref.shape
    rows = bb * tb
    oa = oa_ref[...].reshape(rows, oa_ref.shape[-1])
    og = og_ref[...].reshape(rows, og_ref.shape[-1])
    mix = (jnp.dot(oa, wa_ref[...], preferred_element_type=F32)
           + jnp.dot(og, wg_ref[...], preferred_element_type=F32)).reshape(bb, tb, d)
    mod = mod_ref[...]
    x1 = x_ref[...] + mod[:, :, 2 * d:3 * d] * mix
    x1_out[...] = x1
    y = x1 * lax.rsqrt(jnp.mean(x1 * x1, axis=-1, keepdims=True) + EPS) * gffn_ref[...]
    h2_out[...] = (y * (1.0 + mod[:, :, 4 * d:5 * d]) + mod[:, :, 3 * d:4 * d]).astype(BF16)


def _outproj(o_a, o_g, wout_a, wout_g, x, mod, g_ffn, bb, tb):
    b, t, d = x.shape
    tokspec = lambda w: pl.BlockSpec((bb, tb, w), lambda i, m: (i, m, 0))
    const = lambda shp: pl.BlockSpec(shp, lambda i, m: tuple(0 for _ in shp))
    return pl.pallas_call(
        _outproj_kernel,
        out_shape=(jax.ShapeDtypeStruct((b, t, d), F32), jax.ShapeDtypeStruct((b, t, d), BF16)),
        grid=(b // bb, t // tb),
        in_specs=[tokspec(o_a.shape[-1]), tokspec(o_g.shape[-1]), const(wout_a.shape), const(wout_g.shape),
                  tokspec(d), pl.BlockSpec((bb, 1, mod.shape[-1]), lambda i, m: (i, 0, 0)), const((1, 1, d))],
        out_specs=(tokspec(d), tokspec(d)),
        compiler_params=_cparams(("arbitrary", "arbitrary")),
        name="outproj_norm",
    )(o_a, o_g, wout_a, wout_g, x, mod, g_ffn.reshape(1, 1, d))


def _ffn_kernel(h_ref, halo_ref, wa_ref, wb_ref, wca_ref, wcb_ref, bca_ref, bcb_ref, sa_ref, sb_ref,
                wd_ref, x1_ref, mod_ref, y_out, nba_out, nbb_out, acc_scr, hext_scr, act_scr):
    m = pl.program_id(1)
    f = pl.program_id(2)
    nf = pl.num_programs(2) - 1
    bb, tb, d = h_ref.shape
    rows = bb * tb
    tf = wa_ref.shape[-1]
    halo = halo_ref.shape[1]

    @pl.when(f == 0)
    def _():
        hext_scr[:, 0:halo, :] = halo_ref[...]
        hext_scr[:, halo:, :] = h_ref[...]
        acc_scr[...] = jnp.zeros(acc_scr.shape, F32)
        act_scr[1] = jnp.zeros(act_scr.shape[1:], act_scr.dtype)

    def conv_branch(hext, tpos, w_ref, wc_ref, bc_ref, s_ref, nb_out):
        u_ext = jnp.dot(hext, w_ref[...], preferred_element_type=F32).reshape(bb, tb + halo, tf)
        u = u_ext[:, halo:, :].reshape(rows, tf)
        prev = jnp.where(m == 0, s_ref[...], u_ext[:, halo - (CONV_W - 1):halo, :])
        bcast = lambda r: jnp.broadcast_to(prev[:, r:r + 1, :], (bb, tb, tf)).reshape(rows, tf)
        p0, p1 = bcast(0), bcast(1)
        u1 = jnp.where(tpos == 0, p1, pltpu.roll(u, 1, 0))
        u2 = jnp.where(tpos == 0, p0, jnp.where(tpos == 1, p1, pltpu.roll(u, 2, 0)))
        wc = wc_ref[...]
        nb_out[:, 0] = u.reshape(bb, tb, tf)[:, tb - (CONV_W - 1):, :]
        return bc_ref[...] + wc[0:1] * u2 + wc[1:2] * u1 + wc[2:3] * u

    @pl.when(f < nf)
    def _():
        act_prev = act_scr[(f + 1) % 2]
        hext = hext_scr[...].reshape(bb * (tb + halo), d)
        tpos = lax.broadcasted_iota(jnp.int32, (rows, tf), 0) % tb
        ua = conv_branch(hext, tpos, wa_ref, wca_ref, bca_ref, sa_ref, nba_out)
        ub = conv_branch(hext, tpos, wb_ref, wcb_ref, bcb_ref, sb_ref, nbb_out)
        acc_scr[...] += jnp.dot(act_prev, wd_ref[...], preferred_element_type=F32)
        act_scr[f % 2] = (_silu(ua) * ub).astype(act_scr.dtype)

    @pl.when(f == nf)
    def _():
        acc = acc_scr[...] + jnp.dot(act_scr[(f + 1) % 2], wd_ref[...], preferred_element_type=F32)
        mod = mod_ref[...]
        y_out[...] = x1_ref[...] + mod[:, :, 5 * d:6 * d] * acc.reshape(bb, tb, d)


def _ffn(h2, w_up, w_conv, b_conv, conv_state, w_down, x1, mod, bb, tb):
    b, t, d = x1.shape
    dff = w_down.shape[0]
    tf = FFN_TILE
    nf = dff // tf
    halo = 2 * SUBLANES
    hb = tb // halo
    tokspec = lambda w: pl.BlockSpec((bb, tb, w), lambda i, m, f: (i, m, 0))
    up = lambda f: jnp.minimum(f, nf - 1)
    a_col = lambda r: pl.BlockSpec((r, tf), lambda i, m, f: (0, up(f)))
    b_col = lambda r: pl.BlockSpec((r, tf), lambda i, m, f: (0, nf + up(f)))
    st_a = pl.BlockSpec((bb, CONV_W - 1, tf), lambda i, m, f: (i, 0, up(f)))
    st_b = pl.BlockSpec((bb, CONV_W - 1, tf), lambda i, m, f: (i, 0, nf + up(f)))
    nb_spec = pl.BlockSpec((bb, 1, CONV_W - 1, tf), lambda i, m, f: (i, m, 0, up(f)))
    bc2 = b_conv.reshape(1, 2 * dff)
    return pl.pallas_call(
        _ffn_kernel,
        out_shape=(jax.ShapeDtypeStruct((b, t, d), F32),
                   jax.ShapeDtypeStruct((b, t // tb, CONV_W - 1, dff), F32),
                   jax.ShapeDtypeStruct((b, t // tb, CONV_W - 1, dff), F32)),
        grid=(b // bb, t // tb, nf + 1),
        in_specs=[
            tokspec(d),
            pl.BlockSpec((bb, halo, d), lambda i, m, f: (i, jnp.maximum(m * hb - 1, 0), 0)),
            a_col(d), b_col(d), a_col(CONV_W), b_col(CONV_W), a_col(1), b_col(1), st_a, st_b,
            pl.BlockSpec((tf, d), lambda i, m, f: (jnp.maximum(f - 1, 0), 0)),
            tokspec(d),
            pl.BlockSpec((bb, 1, mod.shape[-1]), lambda i, m, f: (i, 0, 0)),
        ],
        out_specs=(tokspec(d), nb_spec, nb_spec),
        scratch_shapes=[pltpu.VMEM((bb * tb, d), F32), pltpu.VMEM((bb, tb + halo, d), BF16),
                        pltpu.VMEM((2, bb * tb, tf), BF16)],
        compiler_params=_cparams(("arbitrary", "arbitrary", "arbitrary")),
        name="conv_ffn",
    )(h2, h2, w_up, w_up, w_conv, w_conv, bc2, bc2, conv_state, conv_state, w_down, x1, mod)


def _prep_weights(w_in, w_gate2, w_out, w_up, w_down):
    hd = N_ATTN_HEADS * ATTN_HEAD_DIM
    sizes = (hd, hd, hd, N_IDX_HEADS * IDX_DIM, IDX_DIM, N_IDX_HEADS,
             GLA_HEADS * GLA_DK, GLA_HEADS * GLA_DK, GLA_HEADS * GLA_DV, GLA_HEADS * GLA_DV, GLA_GATE_RANK)
    offs = np.concatenate([[0], np.cumsum(sizes)])
    col = lambda n: w_in[:, int(offs[n]):int(offs[n + 1])]
    wbig = jnp.concatenate([col(0), col(1), col(2), col(3), col(6), col(7), col(8), col(9)], axis=1).astype(BF16)
    pad = LANES - (IDX_DIM + N_IDX_HEADS + GLA_GATE_RANK)
    wsmall = jnp.concatenate([col(4), col(5), col(10), jnp.zeros((w_in.shape[0], pad), w_in.dtype)],
                             axis=1).astype(BF16)
    lo = IDX_DIM + N_IDX_HEADS
    wg2p = jnp.zeros((LANES, w_gate2.shape[1]), F32).at[lo:lo + GLA_GATE_RANK].set(w_gate2).astype(BF16)
    return dict(wbig=wbig, wsmall=wsmall, wg2p=wg2p,
                wout_a=w_out[:hd].astype(BF16), wout_g=w_out[hd:].astype(BF16),
                w_up=w_up.astype(BF16), w_down=w_down.astype(BF16))


def _layer(x, mod, past, past_k, past_v, past_ik, gla_s0, conv_state, lw, pw, *, bb, tb, tq, tk, tg):
    (g_mix, g_ffn, g_q, g_k, b_gate2, g_gla, w_conv, b_conv) = lw
    b, t, d = x.shape
    hd = N_ATTN_HEADS * ATTN_HEAD_DIM
    pos = past + jnp.arange(t, dtype=jnp.int32)
    ropea = _rope_table(pos, ROPE_DIM // 2, LANES, LANES)
    ropei = _rope_table(pos, IDX_ROPE_DIM // 2, IDX_DIM, LANES)
    ropes = _rope_table(pos, IDX_ROPE_DIM // 2, LANES, IDX_DIM)

    seq_len = past + t
    lp = -(-seq_len // tk) * tk
    direct_kv = past_k is None and lp == seq_len and bb == 1
    outs = _inproj(x, mod, g_mix, pw["wbig"], pw["wsmall"], g_q, g_k, ropea, ropei, ropes, pw["wg2p"],
                   b_gate2, bb, tb, direct_kv)
    (q_bf, k, v, iq, small, lg, gq, gk, gv, gr) = outs[:10]
    ik = small[:, :, :IDX_DIM]
    iw = small[:, :, IDX_DIM:IDX_DIM + N_IDX_HEADS]

    ik_bf = ik.astype(BF16)
    if direct_kv:
        k_bf, v_t = outs[10:]
    else:
        k_bf, v_bf = k.astype(BF16), v.astype(BF16)
        if past_k is not None:
            k_bf = jnp.concatenate([past_k.reshape(b, past, hd).astype(BF16), k_bf], axis=1)
            v_bf = jnp.concatenate([past_v.reshape(b, past, hd).astype(BF16), v_bf], axis=1)
            ik_bf = jnp.concatenate([past_ik.astype(BF16), ik_bf], axis=1)
        if lp != seq_len:
            padw = ((0, 0), (0, lp - seq_len), (0, 0))
            k_bf, v_bf, ik_bf = jnp.pad(k_bf, padw), jnp.pad(v_bf, padw), jnp.pad(ik_bf, padw)
        v_t = v_bf.transpose(0, 2, 1)
    tpad = -(-t // tq) * tq
    qpad = lambda a: jnp.pad(a, ((0, 0), (0, tpad - t), (0, 0))) if tpad != t else a
    iw_t = qpad(iw).transpose(0, 2, 1)
    topk = min(TOPK_MAX, seq_len // 4)
    o_a = _dsa(qpad(iq), iw_t, ik_bf, qpad(q_bf), k_bf, v_t, past=past, seq_len=seq_len, n_queries=t,
               topk=topk, tq=tq, tk=tk)
    o_a = o_a[:, :t]

    s0t = jnp.swapaxes(gla_s0, -1, -2)
    o_g, s_t = _gla(gq, gk, gv, lg, gr, g_gla, s0t, tg)
    s_new = jnp.swapaxes(s_t, -1, -2)

    x1, h2 = _outproj(o_a, o_g, pw["wout_a"], pw["wout_g"], x, mod, g_ffn, bb, tb)
    y, nb_a, nb_b = _ffn(h2, pw["w_up"], w_conv, b_conv, conv_state, pw["w_down"], x1, mod, bb, tb)
    new_buf = jnp.concatenate([nb_a[:, -1], nb_b[:, -1]], axis=-1)
    return (y, k.reshape(b, t, N_ATTN_HEADS, ATTN_HEAD_DIM), v.reshape(b, t, N_ATTN_HEADS, ATTN_HEAD_DIM),
            ik, s_new, new_buf)


def _tiles(b, t, seq_len):
    if t >= 512:
        bb, tb = 1, 512
    else:
        bb, tb = min(b, 512 // t), t
    tq = 256 if t % 256 == 0 else LANES
    tk = 256 if seq_len % 256 == 0 else LANES
    tg = min(t, 256)
    return dict(bb=bb, tb=tb, tq=tq, tk=tk, tg=tg)


def kernel(x_prompt, x_sample, c_prompt, c_sample, cache_k, cache_v, cache_idx_k, state_gla, state_ffn_conv,
           w_ada, b_ada, g_mix, g_ffn, w_in, g_q, g_k, w_gate2, b_gate2, g_gla, w_out, w_up, w_conv, b_conv,
           w_down):
    bp, s, d = x_prompt.shape
    bs, ts, _ = x_sample.shape
    depth = w_ada.shape[0]
    past = cache_k.shape[2]
    dff = w_down.shape[1]
    dt = x_prompt.dtype

    y_p, y_s = x_prompt, x_sample
    outs = [[] for _ in range(10)]
    for l in range(depth):
        n_c = bp + bs
        rows = -(-n_c // SUBLANES) * SUBLANES
        c_all = jnp.concatenate([c_prompt, c_sample, jnp.zeros((rows - n_c, d), dt)], axis=0)
        mod = _ada(c_all, w_ada[l], b_ada[l])
        mod_p = mod[:bp].reshape(bp, 1, 6 * d)
        mod_s = mod[bp:n_c].reshape(bs, 1, 6 * d)
        pw = _prep_weights(w_in[l], w_gate2[l], w_out[l], w_up[l], w_down[l])
        lw = (g_mix[l], g_ffn[l], g_q[l], g_k[l], b_gate2[l], g_gla[l], w_conv[l], b_conv[l])

        gla0 = jnp.zeros((bp, GLA_HEADS, GLA_DK, GLA_DV), dt)
        conv0 = jnp.zeros((bp, CONV_W - 1, 2 * dff), dt)
        y_p, k1, v1, ik1, s1, c1 = _layer(y_p, mod_p, 0, None, None, None, gla0, conv0, lw, pw,
                                          **_tiles(bp, s, s))
        y_s, k2, v2, ik2, s2, c2 = _layer(y_s, mod_s, past, cache_k[l], cache_v[l], cache_idx_k[l],
                                          state_gla[l], state_ffn_conv[l], lw, pw,
                                          **_tiles(bs, ts, past + ts))
        for lst, val in zip(outs, (k1, v1, ik1, s1, c1, k2, v2, ik2, s2, c2)):
            lst.append(val)

    return (y_p, y_s) + tuple(o[0][None] if depth == 1 else jnp.stack(o) for o in outs)
```
